```python
import jax
import jax.numpy as jnp
from jax import lax
import numpy as np

D_MODEL = 2048
BATCH = 2
SEQ = 16384
DEPTH = 4
DEC_BATCH = 1
DEC_SEQ = 16384
PAST_LEN = 128

HEAD_DIM = 128
SCALE = HEAD_DIM ** -0.5
Q_BLOCK = 128
GRID_W = 64
A_GROUPS = ((128, 1), (512, 4), (2048, 16))
A_HEADS_PER_GROUP = 4
A_HEADS = 12
B_Q_HEADS = 8
B_KV_HEADS = 2
ROPE_THETA = 10000.0
C_Q_HEADS = 8
C_KV_HEADS = 2
C_HALF_WINDOW = 128
N_BRANCHES = 3
D_FF = 5632
NORM_EPS = 1e-6
NEG_INF = -1e30
SPLIT_SIZES = (A_HEADS * HEAD_DIM, A_HEADS * HEAD_DIM, A_HEADS * HEAD_DIM,
               B_Q_HEADS * HEAD_DIM, B_KV_HEADS * HEAD_DIM, B_KV_HEADS * HEAD_DIM,
               C_Q_HEADS * HEAD_DIM, C_KV_HEADS * HEAD_DIM, C_KV_HEADS * HEAD_DIM,
               N_BRANCHES * D_MODEL)
IN_COLS = 13824
A_OUT = A_HEADS_PER_GROUP * HEAD_DIM
B_OUT = B_Q_HEADS * HEAD_DIM
C_OUT = C_Q_HEADS * HEAD_DIM

kernel_name = "hybrid_dilated_axial_window_encoder"


def rms_norm(x, g):
    xf = x.astype(jnp.float32)
    y = xf * lax.rsqrt(jnp.mean(xf * xf, axis=-1, keepdims=True) + NORM_EPS)
    return (y * g.astype(jnp.float32)).astype(x.dtype)


def swiglu(x, w_in, w_out):
    a, b = jnp.split(x @ w_in, 2, axis=-1)
    return (jax.nn.silu(a) * b) @ w_out


def alibi_slopes(n):
    return 2.0 ** (-8.0 * jnp.arange(1, n + 1, dtype=jnp.float32) / n)


def banded_attention(q, k, v, half_w, slopes, dist_scale, sink=None):
    bb, n, hq, hd = q.shape
    hk = k.shape[2]
    g = hq // hk
    nb = -(-n // Q_BLOCK)
    n_pad = nb * Q_BLOCK
    span = Q_BLOCK + 2 * half_w
    qp = jnp.pad(q, ((0, 0), (0, n_pad - n), (0, 0), (0, 0))).reshape(bb, nb, Q_BLOCK, hk, g, hd)
    pad_k = ((0, 0), (half_w, n_pad - n + half_w), (0, 0), (0, 0))
    kidx = jnp.arange(nb)[:, None] * Q_BLOCK + jnp.arange(span)[None, :]
    kb = jnp.pad(k, pad_k)[:, kidx]
    vb = jnp.pad(v, pad_k)[:, kidx]
    qpos = jnp.arange(nb)[:, None] * Q_BLOCK + jnp.arange(Q_BLOCK)[None, :]
    kpos = kidx - half_w
    rel = jnp.abs(kpos[:, None, :] - qpos[:, :, None])
    valid = (rel <= half_w) & (kpos >= 0)[:, None, :] & (kpos < n)[:, None, :]
    dist = (rel * dist_scale).astype(jnp.float32)
    s = jnp.einsum('bnqkgd,bnskd->bnkgqs', qp, kb).astype(jnp.float32) * SCALE
    s = s - slopes.reshape(1, 1, hk, g, 1, 1) * dist[None, :, None, None]
    s = jnp.where(valid[None, :, None, None], s, NEG_INF)
    m = jnp.max(s, axis=-1, keepdims=True)
    if sink is not None:
        sk = sink.astype(jnp.float32).reshape(1, 1, hk, g, 1, 1)
        m = jnp.maximum(m, sk)
    p = jnp.exp(s - m)
    denom = jnp.sum(p, axis=-1, keepdims=True)
    if sink is not None:
        denom = denom + jnp.exp(sk - m)
    out = jnp.einsum('bnkgqs,bnskd->bnqkgd', (p / denom).astype(vb.dtype), vb)
    out = out.reshape(bb, n_pad, hq, hd)[:, :n]
    lse = (m + jnp.log(denom))[..., 0]
    lse = lse.transpose(0, 1, 4, 2, 3).reshape(bb, n_pad, hq)[:, :n]
    return out.astype(q.dtype), lse


def to_sub(x, r):
    b, t = x.shape[:2]
    rest = x.shape[2:]
    return jnp.moveaxis(x.reshape(b, t // r, r, *rest), 2, 1).reshape(b * r, t // r, *rest)


def from_sub(x, r, b, t):
    rest = x.shape[2:]
    return jnp.moveaxis(x.reshape(b, r, t // r, *rest), 1, 2).reshape(b, t, *rest)


def dilated_attention(q, k, v, slopes):
    b, t = q.shape[:2]
    outs, lses = [], []
    for gi, (window, r) in enumerate(A_GROUPS):
        hs = slice(gi * A_HEADS_PER_GROUP, (gi + 1) * A_HEADS_PER_GROUP)
        o, l = banded_attention(to_sub(q[:, :, hs], r), to_sub(k[:, :, hs], r), to_sub(v[:, :, hs], r),
                                (window // 2) // r, slopes[hs], r)
        outs.append(from_sub(o, r, b, t))
        lses.append(from_sub(l, r, b, t))
    alpha = jax.nn.softmax(jnp.stack(lses), axis=0)
    out = jnp.sum(alpha[..., None] * jnp.stack(outs).astype(jnp.float32), axis=0)
    return out.astype(q.dtype)


def axial_rope(x, t):
    rows = t // GRID_W
    row = jnp.repeat(jnp.arange(rows, dtype=jnp.float32), GRID_W)
    col = jnp.tile(jnp.arange(GRID_W, dtype=jnp.float32), rows)
    n_freq = HEAD_DIM // 4
    inv = ROPE_THETA ** (-jnp.arange(n_freq, dtype=jnp.float32) / n_freq)
    ang = jnp.concatenate([row[:, None] * inv, col[:, None] * inv], axis=-1)
    cos = jnp.cos(ang)[None, :, None, :]
    sin = jnp.sin(ang)[None, :, None, :]
    xp = x.reshape(*x.shape[:-1], HEAD_DIM // 2, 2)
    x0, x1 = xp[..., 0], xp[..., 1]
    return jnp.stack([x0 * cos - x1 * sin, x0 * sin + x1 * cos], axis=-1).reshape(x.shape)


def dense_gqa(q, k, v):
    b, t, hq, hd = q.shape
    hk = k.shape[2]
    g = hq // hk
    nb = t // Q_BLOCK
    qb = q.reshape(b, nb, Q_BLOCK, hk, g, hd).transpose(1, 0, 2, 3, 4, 5)

    def block(qi):
        s = jnp.einsum('bqkgd,btkd->bkgqt', qi, k).astype(jnp.float32) * SCALE
        p = jax.nn.softmax(s, axis=-1)
        return jnp.einsum('bkgqt,btkd->bqkgd', p.astype(v.dtype), v).astype(q.dtype)

    o = lax.map(block, qb)
    return o.transpose(1, 0, 2, 3, 4, 5).reshape(b, t, hq, hd)


def mixer(u, w_in, qn_g, kn_g, sink, w_a, w_b, w_c, w_o):
    b, t, _ = u.shape
    points = np.cumsum(SPLIT_SIZES)[:-1].tolist()
    aq, ak, av, bq, bk, bv, cq, ck, cv, gl = jnp.split(u @ w_in, points, axis=-1)
    heads = lambda z: z.reshape(b, t, -1, HEAD_DIM)
    o_a = dilated_attention(heads(aq), heads(ak), heads(av), alibi_slopes(A_HEADS))
    o_a = o_a.reshape(b, t, A_OUT) @ w_a
    qb_ = axial_rope(rms_norm(heads(bq), qn_g).astype(jnp.float32), t).astype(u.dtype)
    kb_ = axial_rope(rms_norm(heads(bk), kn_g).astype(jnp.float32), t).astype(u.dtype)
    o_b = dense_gqa(qb_, kb_, heads(bv)).reshape(b, t, B_OUT) @ w_b
    o_c, _ = banded_attention(heads(cq), heads(ck), heads(cv), C_HALF_WINDOW, alibi_slopes(C_Q_HEADS), 1, sink)
    o_c = o_c.reshape(b, t, C_OUT) @ w_c
    gates = jax.nn.sigmoid(gl.astype(jnp.float32)).reshape(b, t, N_BRANCHES, D_MODEL).astype(u.dtype)
    merged = gates[:, :, 0] * o_a + gates[:, :, 1] * o_b + gates[:, :, 2] * o_c
    return merged @ w_o


def trunk(x, norm_ffn1, ffn1_w_in, ffn1_w_out, norm_mix, w_in, qk_norm_q, qk_norm_k, sink_c,
          w_br_a, w_br_b, w_br_c, w_out, norm_ffn2, ffn2_w_in, ffn2_w_out, norm_final):
    for l in range(DEPTH):
        x = x + 0.5 * swiglu(rms_norm(x, norm_ffn1[l]), ffn1_w_in[l], ffn1_w_out[l])
        x = x + mixer(rms_norm(x, norm_mix[l]), w_in[l], qk_norm_q[l], qk_norm_k[l], sink_c[l],
                      w_br_a[l], w_br_b[l], w_br_c[l], w_out[l])
        x = x + 0.5 * swiglu(rms_norm(x, norm_ffn2[l]), ffn2_w_in[l], ffn2_w_out[l])
    return rms_norm(x, norm_final)


def setup_inputs(seed: int = 0) -> dict:
    key = jax.random.key(seed)
    ks = jax.random.split(key, 20)
    f32 = jnp.float32

    def w(k, shape, fan_in):
        return jax.random.normal(k, shape, f32) * fan_in ** -0.5

    def gain(k, shape):
        return 1.0 + 0.02 * jax.random.normal(k, shape, f32)

    return {
        "x_prompt": jax.random.normal(ks[0], (BATCH, SEQ, D_MODEL), f32),
        "x_sample": jax.random.normal(ks[1], (DEC_BATCH, DEC_SEQ, D_MODEL), f32),
        "norm_ffn1": gain(ks[2], (DEPTH, D_MODEL)),
        "ffn1_w_in": w(ks[3], (DEPTH, D_MODEL, 2 * D_FF), D_MODEL),
        "ffn1_w_out": w(ks[4], (DEPTH, D_FF, D_MODEL), D_FF),
        "norm_mix": gain(ks[5], (DEPTH, D_MODEL)),
        "w_in": w(ks[6], (DEPTH, D_MODEL, IN_COLS), D_MODEL),
        "qk_norm_q": gain(ks[7], (DEPTH, HEAD_DIM)),
        "qk_norm_k": gain(ks[8], (DEPTH, HEAD_DIM)),
        "sink_c": jax.random.normal(ks[9], (DEPTH, C_Q_HEADS), f32),
        "w_br_a": w(ks[10], (DEPTH, A_OUT, D_MODEL), A_OUT),
        "w_br_b": w(ks[11], (DEPTH, B_OUT, D_MODEL), B_OUT),
        "w_br_c": w(ks[12], (DEPTH, C_OUT, D_MODEL), C_OUT),
        "w_out": w(ks[13], (DEPTH, D_MODEL, D_MODEL), D_MODEL),
        "norm_ffn2": gain(ks[14], (DEPTH, D_MODEL)),
        "ffn2_w_in": w(ks[15], (DEPTH, D_MODEL, 2 * D_FF), D_MODEL),
        "ffn2_w_out": w(ks[16], (DEPTH, D_FF, D_MODEL), D_FF),
        "norm_final": gain(ks[17], (D_MODEL,)),
    }


def reference(x_prompt, x_sample, norm_ffn1, ffn1_w_in, ffn1_w_out, norm_mix, w_in, qk_norm_q, qk_norm_k,
              sink_c, w_br_a, w_br_b, w_br_c, w_out, norm_ffn2, ffn2_w_in, ffn2_w_out, norm_final):
    y_prompt = trunk(x_prompt, norm_ffn1, ffn1_w_in, ffn1_w_out, norm_mix, w_in, qk_norm_q, qk_norm_k, sink_c,
                     w_br_a, w_br_b, w_br_c, w_out, norm_ffn2, ffn2_w_in, ffn2_w_out, norm_final)
    y_sample = trunk(x_sample, norm_ffn1, ffn1_w_in, ffn1_w_out, norm_mix, w_in, qk_norm_q, qk_norm_k, sink_c,
                     w_br_a, w_br_b, w_br_c, w_out, norm_ffn2, ffn2_w_in, ffn2_w_out, norm_final)
    return (y_prompt, y_sample)
```

```python
import functools

import jax
import jax.numpy as jnp
from jax import lax
from jax.experimental import pallas as pl
from jax.experimental.pallas import tpu as pltpu

F32 = jnp.float32
BF16 = jnp.bfloat16

HEAD_DIM = 128
SCALE = HEAD_DIM ** -0.5
NORM_EPS = 1e-6
NEG_INF = -1e30
ROPE_THETA = 10000.0
GRID_W = 64
A_GROUPS = ((128, 1), (512, 4), (2048, 16))
A_HEADS_PER_GROUP = 4
A_HEADS = 12
B_Q_HEADS = 8
B_KV_HEADS = 2
C_Q_HEADS = 8
C_KV_HEADS = 2
C_HALF_WINDOW = 128
N_BRANCHES = 3

A_Q0, A_K0, A_V0 = 0, 1536, 3072
B_Q0, B_K0, B_V0 = 4608, 5632, 5888
C_Q0, C_K0, C_V0 = 6144, 7168, 7424
QKV_COLS = 7680
GROUP_W = 4 * HEAD_DIM
A_OUT = A_HEADS_PER_GROUP * HEAD_DIM
B_OUT = B_Q_HEADS * HEAD_DIM
C_OUT = C_Q_HEADS * HEAD_DIM

V7X_VMEM_LIMIT_BYTES = 56 * 1024 * 1024


def _tile(n, pref):
    t = min(pref, n)
    while n % t:
        t //= 2
    return t


def _params(*sem):
    return pltpu.CompilerParams(dimension_semantics=sem, vmem_limit_bytes=V7X_VMEM_LIMIT_BYTES)


def _rms(x, g):
    ms = jnp.mean(x * x, axis=-1, keepdims=True)
    return x * lax.rsqrt(ms + NORM_EPS) * g


def _alibi_slopes(n):
    return tuple(2.0 ** (-8.0 * i / n) for i in range(1, n + 1))


def _ffn_kernel(x_ref, g_ref, wa_ref, wb_ref, wo_ref, o_ref, xn_ref):
    j = pl.program_id(1)

    @pl.when(j == 0)
    def _():
        x = x_ref[...]
        xn_ref[...] = _rms(x, g_ref[...]).astype(BF16)
        o_ref[...] = x

    xn = xn_ref[...]
    a = jnp.dot(xn, wa_ref[...], preferred_element_type=F32)
    b = jnp.dot(xn, wb_ref[...], preferred_element_type=F32)
    h = (0.5 * a * jax.nn.sigmoid(a) * b).astype(BF16)
    o_ref[...] += jnp.dot(h, wo_ref[...], preferred_element_type=F32)


def _ffn(x, g, w_in, w_out):
    n, d = x.shape
    d_ff = w_out.shape[0]
    tm = _tile(n, 512)
    tf = _tile(d_ff, 512)
    nf = d_ff // tf
    return pl.pallas_call(
        _ffn_kernel,
        grid=(n // tm, nf),
        in_specs=[
            pl.BlockSpec((tm, d), lambda i, j: (i, 0)),
            pl.BlockSpec((1, d), lambda i, j: (0, 0)),
            pl.BlockSpec((d, tf), lambda i, j: (0, j)),
            pl.BlockSpec((d, tf), lambda i, j: (0, j + nf)),
            pl.BlockSpec((tf, d), lambda i, j: (j, 0)),
        ],
        out_specs=pl.BlockSpec((tm, d), lambda i, j: (i, 0)),
        out_shape=jax.ShapeDtypeStruct((n, d), F32),
        scratch_shapes=[pltpu.VMEM((tm, d), BF16)],
        compiler_params=_params("parallel", "arbitrary"),
        name="ffn",
    )(x, g, w_in, w_in, w_out)


def _norm_rope(y, gain, cos, sin_signed):
    yn = _rms(y, gain)
    lane = lax.broadcasted_iota(jnp.int32, yn.shape, 1)
    partner = jnp.where((lane & 1) == 0, pltpu.roll(yn, HEAD_DIM - 1, 1), pltpu.roll(yn, 1, 1))
    return yn * cos + partner * sin_signed


def _qkv_kernel(x_ref, g_ref, w_ref, cos_ref, sin_ref, qg_ref, kg_ref, o_ref, u_ref):
    j = pl.program_id(1)

    @pl.when(j == 0)
    def _():
        u_ref[...] = _rms(x_ref[...], g_ref[...]).astype(BF16)

    acc = jnp.dot(u_ref[...], w_ref[...], preferred_element_type=F32)
    bq_lo, bq_hi = B_Q0 // GROUP_W, B_K0 // GROUP_W
    is_bq = (j >= bq_lo) & (j < bq_hi)
    is_bkv = j == B_K0 // GROUP_W
    is_q = (j < A_K0 // GROUP_W) | ((j >= C_Q0 // GROUP_W) & (j < C_K0 // GROUP_W))

    @pl.when(jnp.logical_not(is_bq | is_bkv))
    def _():
        sc = jnp.where(is_q, SCALE, 1.0).astype(F32)
        o_ref[...] = (acc * sc).astype(BF16)

    @pl.when(is_bq)
    def _():
        for h in range(GROUP_W // HEAD_DIM):
            sl = slice(h * HEAD_DIM, (h + 1) * HEAD_DIM)
            y = _norm_rope(acc[:, sl], qg_ref[...], cos_ref[...], sin_ref[...])
            o_ref[:, sl] = (y * SCALE).astype(BF16)

    @pl.when(is_bkv)
    def _():
        for h in range(B_KV_HEADS):
            sl = slice(h * HEAD_DIM, (h + 1) * HEAD_DIM)
            y = _norm_rope(acc[:, sl], kg_ref[...], cos_ref[...], sin_ref[...])
            o_ref[:, sl] = y.astype(BF16)
        v0 = B_KV_HEADS * HEAD_DIM
        o_ref[:, v0:] = acc[:, v0:].astype(BF16)


def _qkv(x, g, w_in, cos, sin_signed, qg, kg, seq):
    n, d = x.shape
    tm = _tile(seq, 512)
    nseq = seq // tm
    return pl.pallas_call(
        _qkv_kernel,
        grid=(n // tm, QKV_COLS // GROUP_W),
        in_specs=[
            pl.BlockSpec((tm, d), lambda i, j: (i, 0)),
            pl.BlockSpec((1, d), lambda i, j: (0, 0)),
            pl.BlockSpec((d, GROUP_W), lambda i, j: (0, j)),
            pl.BlockSpec((tm, HEAD_DIM), lambda i, j: (i % nseq, 0)),
            pl.BlockSpec((tm, HEAD_DIM), lambda i, j: (i % nseq, 0)),
            pl.BlockSpec((1, HEAD_DIM), lambda i, j: (0, 0)),
            pl.BlockSpec((1, HEAD_DIM), lambda i, j: (0, 0)),
        ],
        out_specs=pl.BlockSpec((tm, GROUP_W), lambda i, j: (i, j)),
        out_shape=jax.ShapeDtypeStruct((n, QKV_COLS), BF16),
        scratch_shapes=[pltpu.VMEM((tm, d), BF16)],
        compiler_params=_params("parallel", "arbitrary"),
        name="qkv_proj",
    )(x, g, w_in, cos, sin_signed, qg, kg)


def _band_kernel(*refs, tq, hb, half_w, n_sub, slopes, dist_scale, n_kv, has_sink, emit_lse):
    if has_sink:
        sink_ref, refs = refs[0], refs[1:]
    q_ref, kp_ref, km_ref, kn_ref, vp_ref, vm_ref, vn_ref = refs[:7]
    o_ref = refs[7]
    n_heads = len(slopes)
    group = n_heads // n_kv
    qi = pl.program_id(2)
    span = tq + 2 * hb
    row = lax.broadcasted_iota(jnp.int32, (tq, span), 0)
    col = lax.broadcasted_iota(jnp.int32, (tq, span), 1)
    arel = jnp.abs(col - hb - row)
    kpos = qi * tq - hb + col
    valid = (arel <= half_w) & (kpos >= 0) & (kpos < n_sub)
    dist = (arel * dist_scale).astype(F32)
    lse_tile = None
    if emit_lse:
        lane = lax.broadcasted_iota(jnp.int32, (tq, HEAD_DIM), 1)
        lse_tile = jnp.zeros((tq, HEAD_DIM), F32)
    for h in range(n_heads):
        sl = slice(h * HEAD_DIM, (h + 1) * HEAD_DIM)
        ksl = slice((h // group) * HEAD_DIM, (h // group + 1) * HEAD_DIM)
        q = q_ref[:, sl]
        kcat = jnp.concatenate([kp_ref[:, ksl], km_ref[:, ksl], kn_ref[:, ksl]], axis=0)
        vcat = jnp.concatenate([vp_ref[:, ksl], vm_ref[:, ksl], vn_ref[:, ksl]], axis=0)
        s = lax.dot_general(q, kcat, (((1,), (1,)), ((), ())), preferred_element_type=F32)
        s = jnp.where(valid, s - slopes[h] * dist, NEG_INF)
        m = jnp.max(s, axis=-1, keepdims=True)
        if has_sink:
            sk = sink_ref[0, h]
            m = jnp.maximum(m, sk)
        p = jnp.exp(s - m)
        den = jnp.sum(p, axis=-1, keepdims=True)
        if has_sink:
            den = den + jnp.exp(sk - m)
        o = jnp.dot((p / den).astype(BF16), vcat, preferred_element_type=F32)
        o_ref[:, sl] = o.astype(o_ref.dtype)
        if emit_lse:
            lse_tile = jnp.where(lane == h, m + jnp.log(den), lse_tile)
    if emit_lse:
        refs[8][...] = lse_tile


def _band_attention(qkv, *, seq, r, q_col, k_col, v_col, n_heads, n_kv, half_w, slopes, dist_scale,
                    sink, emit_lse, out_dtype):
    bsz = qkv.shape[0]
    n_sub = seq // r
    view = qkv.reshape(bsz, n_sub, r * QKV_COLS)
    hb = half_w
    tq = _tile(n_sub, 256)
    assert tq % hb == 0 and n_sub % tq == 0
    per = tq // hb
    n_hb = n_sub // hb
    qw, kw = n_heads * HEAD_DIM, n_kv * HEAD_DIM
    assert q_col % qw == 0 and k_col % kw == 0 and v_col % kw == 0
    assert r == 1 or (QKV_COLS % qw == 0 and QKV_COLS % kw == 0)
    qs, ks = QKV_COLS // qw, QKV_COLS // kw

    def prev_map(c):
        return lambda b, p, i: (b, jnp.maximum(i * per - 1, 0), p * ks + c // kw)

    def main_map(c, stride, w):
        return lambda b, p, i: (b, i, p * stride + c // w)

    def next_map(c):
        return lambda b, p, i: (b, jnp.minimum((i + 1) * per, n_hb - 1), p * ks + c // kw)

    in_specs = [
        pl.BlockSpec((None, tq, qw), main_map(q_col, qs, qw)),
        pl.BlockSpec((None, hb, kw), prev_map(k_col)),
        pl.BlockSpec((None, tq, kw), main_map(k_col, ks, kw)),
        pl.BlockSpec((None, hb, kw), next_map(k_col)),
        pl.BlockSpec((None, hb, kw), prev_map(v_col)),
        pl.BlockSpec((None, tq, kw), main_map(v_col, ks, kw)),
        pl.BlockSpec((None, hb, kw), next_map(v_col)),
    ]
    args = [view] * 7
    if sink is not None:
        in_specs = [pl.BlockSpec(memory_space=pltpu.SMEM)] + in_specs
        args = [sink] + args
    out_specs = [pl.BlockSpec((None, tq, qw), lambda b, p, i: (b, i, p))]
    out_shape = [jax.ShapeDtypeStruct((bsz, n_sub, r * qw), out_dtype)]
    if emit_lse:
        out_specs.append(pl.BlockSpec((None, tq, HEAD_DIM), lambda b, p, i: (b, i, p)))
        out_shape.append(jax.ShapeDtypeStruct((bsz, n_sub, r * HEAD_DIM), F32))
    outs = pl.pallas_call(
        functools.partial(_band_kernel, tq=tq, hb=hb, half_w=half_w, n_sub=n_sub, slopes=slopes,
                          dist_scale=dist_scale, n_kv=n_kv, has_sink=sink is not None, emit_lse=emit_lse),
        grid=(bsz, r, n_sub // tq),
        in_specs=in_specs,
        out_specs=out_specs,
        out_shape=out_shape,
        compiler_params=_params("parallel", "parallel", "arbitrary"),
        name=f"band_attn_r{r}_w{half_w}",
    )(*args)
    o = outs[0].reshape(bsz * seq, qw)
    if emit_lse:
        return o, outs[1].reshape(bsz * seq, HEAD_DIM)
    return o


def _merge_kernel(o0_ref, o1_ref, o2_ref, l0_ref, l1_ref, l2_ref, out_ref):
    l0, l1, l2 = l0_ref[...], l1_ref[...], l2_ref[...]
    mx = jnp.maximum(jnp.maximum(l0, l1), l2)
    w0, w1, w2 = jnp.exp(l0 - mx), jnp.exp(l1 - mx), jnp.exp(l2 - mx)
    den = w0 + w1 + w2
    a0, a1, a2 = w0 / den, w1 / den, w2 / den
    for h in range(A_HEADS_PER_GROUP):
        sl = slice(h * HEAD_DIM, (h + 1) * HEAD_DIM)
        out = a0[:, h:h + 1] * o0_ref[:, sl] + a1[:, h:h + 1] * o1_ref[:, sl] + a2[:, h:h + 1] * o2_ref[:, sl]
        out_ref[:, sl] = out.astype(BF16)


def _merge_groups(outs, lses):
    n = outs[0].shape[0]
    tm = _tile(n, 512)
    o_spec = pl.BlockSpec((tm, A_OUT), lambda i: (i, 0))
    l_spec = pl.BlockSpec((tm, HEAD_DIM), lambda i: (i, 0))
    return pl.pallas_call(
        _merge_kernel,
        grid=(n // tm,),
        in_specs=[o_spec] * 3 + [l_spec] * 3,
        out_specs=o_spec,
        out_shape=jax.ShapeDtypeStruct((n, A_OUT), BF16),
        compiler_params=_params("parallel"),
        name="merge_dilated",
    )(*outs, *lses)


def _dense_kernel(q_ref, k_ref, v_ref, o_ref, qt_ref, m_ref, l_ref, acc_ref, *, tq, tk, seq):
    group = B_Q_HEADS // B_KV_HEADS
    for g in range(group):
        qg = q_ref[:, g * HEAD_DIM:(g + 1) * HEAD_DIM].astype(F32)
        qt_ref[:, g * tq:(g + 1) * tq] = qg.T.astype(BF16)
    m_ref[...] = jnp.full(m_ref.shape, -jnp.inf, F32)
    l_ref[...] = jnp.zeros(l_ref.shape, F32)
    acc_ref[...] = jnp.zeros(acc_ref.shape, F32)

    def body(c, carry):
        k0 = pl.multiple_of(c * tk, tk)
        k = k_ref[pl.ds(k0, tk), :]
        v = v_ref[pl.ds(k0, tk), :]
        st = jnp.dot(k, qt_ref[...], preferred_element_type=F32)
        m_prev = m_ref[...]
        m_new = jnp.maximum(m_prev, jnp.max(st, axis=0, keepdims=True))
        alpha = jnp.exp(m_prev - m_new)
        p = jnp.exp(st - m_new)
        l_ref[...] = alpha * l_ref[...] + jnp.sum(p, axis=0, keepdims=True)
        pv = lax.dot_general(v, p.astype(BF16), (((0,), (0,)), ((), ())), preferred_element_type=F32)
        acc_ref[...] = alpha * acc_ref[...] + pv
        m_ref[...] = m_new
        return carry

    lax.fori_loop(0, seq // tk, body, 0)
    out = acc_ref[...] / l_ref[...]
    for g in range(group):
        o_ref[:, g * HEAD_DIM:(g + 1) * HEAD_DIM] = out[:, g * tq:(g + 1) * tq].T.astype(BF16)


def _dense_attention(qkv, seq):
    bsz = qkv.shape[0]
    tq = _tile(seq, 256)
    tk = _tile(seq, 512)
    group = B_Q_HEADS // B_KV_HEADS
    qw = group * HEAD_DIM
    out = pl.pallas_call(
        functools.partial(_dense_kernel, tq=tq, tk=tk, seq=seq),
        grid=(bsz, B_KV_HEADS, seq // tq),
        in_specs=[
            pl.BlockSpec((None, tq, qw), lambda b, h, i: (b, i, B_Q0 // qw + h)),
            pl.BlockSpec((None, seq, HEAD_DIM), lambda b, h, i: (b, 0, B_K0 // HEAD_DIM + h)),
            pl.BlockSpec((None, seq, HEAD_DIM), lambda b, h, i: (b, 0, B_V0 // HEAD_DIM + h)),
        ],
        out_specs=pl.BlockSpec((None, tq, qw), lambda b, h, i: (b, i, h)),
        out_shape=jax.ShapeDtypeStruct((bsz, seq, B_OUT), BF16),
        scratch_shapes=[
            pltpu.VMEM((HEAD_DIM, group * tq), BF16),
            pltpu.VMEM((1, group * tq), F32),
            pltpu.VMEM((1, group * tq), F32),
            pltpu.VMEM((HEAD_DIM, group * tq), F32),
        ],
        compiler_params=_params("parallel", "parallel", "arbitrary"),
        name="dense_attn",
    )(qkv, qkv, qkv)
    return out.reshape(bsz * seq, B_OUT)


def _out_kernel(x_ref, g_ref, wg0_ref, wg1_ref, wg2_ref, oa_ref, ob_ref, oc_ref,
                wa_ref, wb_ref, wc_ref, wo_ref, o_ref, u_ref):
    j = pl.program_id(1)

    @pl.when(j == 0)
    def _():
        x = x_ref[...]
        u_ref[...] = _rms(x, g_ref[...]).astype(BF16)
        o_ref[...] = x

    u = u_ref[...]

    def branch(wg_ref, br_ref, w_ref):
        gate = jax.nn.sigmoid(jnp.dot(u, wg_ref[...], preferred_element_type=F32))
        return gate * jnp.dot(br_ref[...], w_ref[...], preferred_element_type=F32)

    merged = branch(wg0_ref, oa_ref, wa_ref) + branch(wg1_ref, ob_ref, wb_ref) + branch(wg2_ref, oc_ref, wc_ref)
    o_ref[...] += jnp.dot(merged.astype(BF16), wo_ref[...], preferred_element_type=F32)


def _mixer_out(x, g, w_in, o_a, o_b, o_c, w_a, w_b, w_c, w_o):
    n, d = x.shape
    tm = _tile(n, 512)
    tc = _tile(d, 256)
    assert QKV_COLS % tc == 0
    g0 = QKV_COLS // tc
    gstep = d // tc

    def gate_spec(br):
        return pl.BlockSpec((d, tc), lambda i, j: (0, g0 + br * gstep + j))

    return pl.pallas_call(
        _out_kernel,
        grid=(n // tm, d // tc),
        in_specs=[
            pl.BlockSpec((tm, d), lambda i, j: (i, 0)),
            pl.BlockSpec((1, d), lambda i, j: (0, 0)),
            gate_spec(0), gate_spec(1), gate_spec(2),
            pl.BlockSpec((tm, A_OUT), lambda i, j: (i, 0)),
            pl.BlockSpec((tm, B_OUT), lambda i, j: (i, 0)),
            pl.BlockSpec((tm, C_OUT), lambda i, j: (i, 0)),
            pl.BlockSpec((A_OUT, tc), lambda i, j: (0, j)),
            pl.BlockSpec((B_OUT, tc), lambda i, j: (0, j)),
            pl.BlockSpec((C_OUT, tc), lambda i, j: (0, j)),
            pl.BlockSpec((tc, d), lambda i, j: (j, 0)),
        ],
        out_specs=pl.BlockSpec((tm, d), lambda i, j: (i, 0)),
        out_shape=jax.ShapeDtypeStruct((n, d), F32),
        scratch_shapes=[pltpu.VMEM((tm, d), BF16)],
        compiler_params=_params("parallel", "arbitrary"),
        name="mixer_out",
    )(x, g, w_in, w_in, w_in, o_a, o_b, o_c, w_a, w_b, w_c, w_o)


def _final_norm_kernel(x_ref, g_ref, o_ref):
    o_ref[...] = _rms(x_ref[...], g_ref[...])


def _final_norm(x, g):
    n, d = x.shape
    tm = _tile(n, 1024)
    return pl.pallas_call(
        _final_norm_kernel,
        grid=(n // tm,),
        in_specs=[pl.BlockSpec((tm, d), lambda i: (i, 0)), pl.BlockSpec((1, d), lambda i: (0, 0))],
        out_specs=pl.BlockSpec((tm, d), lambda i: (i, 0)),
        out_shape=jax.ShapeDtypeStruct((n, d), F32),
        compiler_params=_params("parallel"),
        name="final_norm",
    )(x, g)


def _rope_tables(seq):
    n_freq = HEAD_DIM // 4
    t = lax.broadcasted_iota(jnp.int32, (seq, HEAD_DIM), 0)
    lane = lax.broadcasted_iota(jnp.int32, (seq, HEAD_DIM), 1)
    pair = lane // 2
    pos = jnp.where(pair < n_freq, t // GRID_W, t % GRID_W).astype(F32)
    inv = ROPE_THETA ** (-(pair % n_freq).astype(F32) / n_freq)
    ang = pos * inv
    sign = jnp.where(lane % 2 == 0, -1.0, 1.0).astype(F32)
    return jnp.cos(ang), jnp.sin(ang) * sign


def _mixer(x, lw, cos, sin_signed, bsz, seq):
    qkv = _qkv(x, lw["norm_mix"], lw["w_in"], cos, sin_signed, lw["qk_norm_q"], lw["qk_norm_k"], seq)
    qkv = qkv.reshape(bsz, seq, QKV_COLS)
    a_slopes = _alibi_slopes(A_HEADS)
    outs, lses = [], []
    for gi, (window, r) in enumerate(A_GROUPS):
        hs = slice(gi * A_HEADS_PER_GROUP, (gi + 1) * A_HEADS_PER_GROUP)
        o, l = _band_attention(
            qkv, seq=seq, r=r, q_col=A_Q0 + gi * GROUP_W, k_col=A_K0 + gi * GROUP_W, v_col=A_V0 + gi * GROUP_W,
            n_heads=A_HEADS_PER_GROUP, n_kv=A_HEADS_PER_GROUP, half_w=(window // 2) // r, slopes=a_slopes[hs],
            dist_scale=r, sink=None, emit_lse=True, out_dtype=F32)
        outs.append(o)
        lses.append(l)
    o_a = _merge_groups(outs, lses)
    o_b = _dense_attention(qkv, seq)
    o_c = _band_attention(
        qkv, seq=seq, r=1, q_col=C_Q0, k_col=C_K0, v_col=C_V0, n_heads=C_Q_HEADS, n_kv=C_KV_HEADS,
        half_w=C_HALF_WINDOW, slopes=_alibi_slopes(C_Q_HEADS), dist_scale=1, sink=lw["sink_c"],
        emit_lse=False, out_dtype=BF16)
    return _mixer_out(x, lw["norm_mix"], lw["w_in"], o_a, o_b, o_c, lw["w_br_a"], lw["w_br_b"], lw["w_br_c"],
                      lw["w_out"])


def _trunk(x, layers, norm_final):
    bsz, seq, d = x.shape
    cos, sin_signed = _rope_tables(seq)
    xf = x.reshape(bsz * seq, d)

    def layer(xf, lw):
        xf = _ffn(xf, lw["norm_ffn1"], lw["ffn1_w_in"], lw["ffn1_w_out"])
        xf = _mixer(xf, lw, cos, sin_signed, bsz, seq)
        xf = _ffn(xf, lw["norm_ffn2"], lw["ffn2_w_in"], lw["ffn2_w_out"])
        return xf, None

    xf, _ = lax.scan(layer, xf, layers)
    return _final_norm(xf, norm_final.reshape(1, d)).reshape(bsz, seq, d)


def kernel(x_prompt, x_sample, norm_ffn1, ffn1_w_in, ffn1_w_out, norm_mix, w_in, qk_norm_q, qk_norm_k, sink_c,
           w_br_a, w_br_b, w_br_c, w_out, norm_ffn2, ffn2_w_in, ffn2_w_out, norm_final):
    depth = norm_ffn1.shape[0]
    row = lambda v: v.reshape(depth, 1, v.shape[-1])
    layers = {
        "norm_ffn1": row(norm_ffn1), "ffn1_w_in": ffn1_w_in.astype(BF16), "ffn1_w_out": ffn1_w_out.astype(BF16),
        "norm_mix": row(norm_mix), "w_in": w_in.astype(BF16),
        "qk_norm_q": row(qk_norm_q), "qk_norm_k": row(qk_norm_k), "sink_c": row(sink_c),
        "w_br_a": w_br_a.astype(BF16), "w_br_b": w_br_b.astype(BF16), "w_br_c": w_br_c.astype(BF16),
        "w_out": w_out.astype(BF16),
        "norm_ffn2": row(norm_ffn2), "ffn2_w_in": ffn2_w_in.astype(BF16), "ffn2_w_out": ffn2_w_out.astype(BF16),
    }
    assert x_prompt.shape[1:] == x_sample.shape[1:]
    n_prompt = x_prompt.shape[0]
    y = _trunk(jnp.concatenate([x_prompt, x_sample], axis=0), layers, norm_final)
    return (y[:n_prompt], y[n_prompt:])
```

```python
import functools

import jax
import jax.numpy as jnp
from jax import lax
from jax.experimental import pallas as pl
from jax.experimental.pallas import tpu as pltpu

F32 = jnp.float32
BF16 = jnp.bfloat16

HEAD_DIM = 128
SCALE = HEAD_DIM ** -0.5
LOG2_E = 1.4426950408889634
Q_SCALE = SCALE * LOG2_E
NORM_EPS = 1e-6
NEG_INF = -1e30
ROPE_THETA = 10000.0
GRID_W = 64
A_GROUPS = ((128, 1), (512, 4), (2048, 16))
A_HEADS_PER_GROUP = 4
A_HEADS = 12
B_Q_HEADS = 8
B_KV_HEADS = 2
C_Q_HEADS = 8
C_KV_HEADS = 2
C_HALF_WINDOW = 128
N_BRANCHES = 3

QKV_COLS = 7680
PROJ_W = 1536
GROUP_W = 4 * HEAD_DIM
A_OUT = A_HEADS_PER_GROUP * HEAD_DIM
B_OUT = B_Q_HEADS * HEAD_DIM
C_OUT = C_Q_HEADS * HEAD_DIM
B_K0, B_V0 = B_OUT, B_OUT + B_KV_HEADS * HEAD_DIM
C_K0, C_V0 = C_OUT, C_OUT + C_KV_HEADS * HEAD_DIM

V7X_VMEM_LIMIT_BYTES = 56 * 1024 * 1024


def _tile(n, pref):
    t = min(pref, n)
    while n % t:
        t //= 2
    return t


def _params(*sem):
    return pltpu.CompilerParams(dimension_semantics=sem, vmem_limit_bytes=V7X_VMEM_LIMIT_BYTES)


def _rms(x, g):
    ms = jnp.mean(x * x, axis=-1, keepdims=True)
    return x * lax.rsqrt(ms + NORM_EPS) * g


def _alibi_slopes(n):
    return tuple(2.0 ** (-8.0 * i / n) for i in range(1, n + 1))


def _ffn_kernel(x_ref, g_ref, wa_ref, wb_ref, wo_ref, o_ref, xn_ref):
    j = pl.program_id(1)

    @pl.when(j == 0)
    def _():
        x = x_ref[...]
        xn_ref[...] = _rms(x, g_ref[...]).astype(BF16)
        o_ref[...] = x

    xn = xn_ref[...]
    a = jnp.dot(xn, wa_ref[...], preferred_element_type=F32)
    b = jnp.dot(xn, wb_ref[...], preferred_element_type=F32)
    h = (0.5 * a * jax.nn.sigmoid(a) * b).astype(BF16)
    o_ref[...] += jnp.dot(h, wo_ref[...], preferred_element_type=F32)


def _ffn(x, g, w_in, w_out):
    n, d = x.shape
    d_ff = w_out.shape[0]
    tm = _tile(n, 512)
    tf = _tile(d_ff, 512)
    nf = d_ff // tf
    return pl.pallas_call(
        _ffn_kernel,
        grid=(n // tm, nf),
        in_specs=[
            pl.BlockSpec((tm, d), lambda i, j: (i, 0)),
            pl.BlockSpec((1, d), lambda i, j: (0, 0)),
            pl.BlockSpec((d, tf), lambda i, j: (0, j)),
            pl.BlockSpec((d, tf), lambda i, j: (0, j + nf)),
            pl.BlockSpec((tf, d), lambda i, j: (j, 0)),
        ],
        out_specs=pl.BlockSpec((tm, d), lambda i, j: (i, 0)),
        out_shape=jax.ShapeDtypeStruct((n, d), F32),
        scratch_shapes=[pltpu.VMEM((tm, d), BF16)],
        compiler_params=_params("parallel", "arbitrary"),
        name="ffn",
    )(x, g, w_in, w_in, w_out)


def _norm_rope(y, gain, cos, sin_signed):
    yn = _rms(y, gain)
    lane = lax.broadcasted_iota(jnp.int32, yn.shape, 1)
    partner = jnp.where((lane & 1) == 0, pltpu.roll(yn, HEAD_DIM - 1, 1), pltpu.roll(yn, 1, 1))
    return yn * cos + partner * sin_signed


def _qkv_kernel(x_ref, g_ref, w_ref, cos_ref, sin_ref, qg_ref, kg_ref,
                a0_ref, a1_ref, a2_ref, b_ref, c_ref, u_ref, split_ref):
    j = pl.program_id(1)
    tm = x_ref.shape[0]

    @pl.when(j == 0)
    def _():
        u_ref[...] = _rms(x_ref[...], g_ref[...]).astype(BF16)

    acc = jnp.dot(u_ref[...], w_ref[...], preferred_element_type=F32)

    @pl.when(j < 3)
    def _():
        y = acc * jnp.where(j == 0, Q_SCALE, 1.0).astype(F32)
        a0_ref[...] = y[:, :GROUP_W].astype(BF16)
        for c in range(split_ref.shape[0]):
            split_ref[c] = y[:, GROUP_W + c * HEAD_DIM:GROUP_W + (c + 1) * HEAD_DIM]
        for gi, dst_ref in ((1, a1_ref), (2, a2_ref)):
            r = A_GROUPS[gi][1]
            for p in range(r):
                for h in range(A_HEADS_PER_GROUP):
                    rows = split_ref[(gi - 1) * A_HEADS_PER_GROUP + h, pl.ds(p, tm // r, stride=r), :]
                    dst_ref[p, :, h * HEAD_DIM:(h + 1) * HEAD_DIM] = rows.astype(BF16)

    @pl.when(j == 3)
    def _():
        for h in range(B_Q_HEADS + B_KV_HEADS):
            sl = slice(h * HEAD_DIM, (h + 1) * HEAD_DIM)
            is_q = h < B_Q_HEADS
            y = _norm_rope(acc[:, sl], (qg_ref if is_q else kg_ref)[...], cos_ref[...], sin_ref[...])
            b_ref[:, sl] = ((y * Q_SCALE) if is_q else y).astype(BF16)
        b_ref[:, B_V0:] = acc[:, B_V0:].astype(BF16)

    @pl.when(j == 4)
    def _():
        c_ref[:, :C_OUT] = (acc[:, :C_OUT] * Q_SCALE).astype(BF16)
        c_ref[:, C_OUT:] = acc[:, C_OUT:].astype(BF16)


def _qkv(x, g, w_in, cos, sin_signed, qg, kg, bsz, seq):
    n, d = x.shape
    tm = _tile(seq, 512)
    nseq = seq // tm
    r1, r2 = A_GROUPS[1][1], A_GROUPS[2][1]
    assert tm % (16 * r2) == 0
    a_part = lambda i, j: jnp.minimum(j, 2)

    def split_spec(r):
        return pl.BlockSpec((None, r, tm // r, GROUP_W), lambda i, j: (i // nseq, 0, i % nseq, a_part(i, j)))

    return pl.pallas_call(
        _qkv_kernel,
        grid=(n // tm, QKV_COLS // PROJ_W),
        in_specs=[
            pl.BlockSpec((tm, d), lambda i, j: (i, 0)),
            pl.BlockSpec((1, d), lambda i, j: (0, 0)),
            pl.BlockSpec((d, PROJ_W), lambda i, j: (0, j)),
            pl.BlockSpec((tm, HEAD_DIM), lambda i, j: (i % nseq, 0)),
            pl.BlockSpec((tm, HEAD_DIM), lambda i, j: (i % nseq, 0)),
            pl.BlockSpec((1, HEAD_DIM), lambda i, j: (0, 0)),
            pl.BlockSpec((1, HEAD_DIM), lambda i, j: (0, 0)),
        ],
        out_specs=[
            pl.BlockSpec((tm, GROUP_W), lambda i, j: (i, a_part(i, j))),
            split_spec(r1),
            split_spec(r2),
            pl.BlockSpec((tm, PROJ_W), lambda i, j: (i, 0)),
            pl.BlockSpec((tm, PROJ_W), lambda i, j: (i, 0)),
        ],
        out_shape=[
            jax.ShapeDtypeStruct((n, PROJ_W), BF16),
            jax.ShapeDtypeStruct((bsz, r1, seq // r1, PROJ_W), BF16),
            jax.ShapeDtypeStruct((bsz, r2, seq // r2, PROJ_W), BF16),
            jax.ShapeDtypeStruct((n, PROJ_W), BF16),
            jax.ShapeDtypeStruct((n, PROJ_W), BF16),
        ],
        scratch_shapes=[pltpu.VMEM((tm, d), BF16),
                        pltpu.VMEM(((PROJ_W - GROUP_W) // HEAD_DIM, tm, HEAD_DIM), F32)],
        compiler_params=_params("parallel", "arbitrary"),
        name="qkv_proj",
    )(x, g, w_in, cos, sin_signed, qg, kg)


def _band_kernel(*refs, tq, sub, hb, half_w, n_sub, slopes, dist_scale, n_kv, has_sink, emit_lse):
    if has_sink:
        sink_ref, refs = refs[0], refs[1:]
    q_ref, kp_ref, km_ref, kn_ref, vp_ref, vm_ref, vn_ref, o_ref = refs[:8]
    n_heads = len(slopes)
    group = n_heads // n_kv
    qi = pl.program_id(1)
    span = sub + 2 * hb
    row = lax.broadcasted_iota(jnp.int32, (sub, span), 0)
    col = lax.broadcasted_iota(jnp.int32, (sub, span), 1)
    arel = jnp.abs(col - hb - row)
    in_band = arel <= half_w
    dist = (arel * dist_scale).astype(F32)
    biases = [jnp.where(in_band, (-slopes[h] * LOG2_E) * dist, NEG_INF) for h in range(n_heads)]
    kcat, vcat = [], []
    for kv in range(n_kv):
        ksl = slice(kv * HEAD_DIM, (kv + 1) * HEAD_DIM)
        kcat.append(jnp.concatenate([kp_ref[:, ksl], km_ref[:, ksl], kn_ref[:, ksl]], axis=0))
        vcat.append(jnp.concatenate([vp_ref[:, ksl], vm_ref[:, ksl], vn_ref[:, ksl]], axis=0))
    if emit_lse:
        lane = lax.broadcasted_iota(jnp.int32, (sub, HEAD_DIM), 1)
    for s in range(tq // sub):
        rows = slice(s * sub, (s + 1) * sub)
        kpos = qi * tq + (s * sub - hb) + col
        in_seq = (kpos >= 0) & (kpos < n_sub)
        lse_tile = jnp.zeros((sub, HEAD_DIM), F32) if emit_lse else None
        for h in range(n_heads):
            sl = slice(h * HEAD_DIM, (h + 1) * HEAD_DIM)
            ks = kcat[h // group][s * sub:s * sub + span]
            vs = vcat[h // group][s * sub:s * sub + span]
            sc = lax.dot_general(q_ref[rows, sl], ks, (((1,), (1,)), ((), ())), preferred_element_type=F32)
            sc = jnp.where(in_seq, sc + biases[h], NEG_INF)
            m = jnp.max(sc, axis=-1, keepdims=True)
            if has_sink:
                sk = sink_ref[0, h] * LOG2_E
                m = jnp.maximum(m, sk)
            p = jnp.exp2(sc - m)
            den = jnp.sum(p, axis=-1, keepdims=True)
            if has_sink:
                den = den + jnp.exp2(sk - m)
            o = jnp.dot(p.astype(BF16), vs, preferred_element_type=F32) * (1.0 / den)
            o_ref[rows, sl] = o.astype(o_ref.dtype)
            if emit_lse:
                lse_tile = jnp.where(lane == h, m + jnp.log2(den), lse_tile)
        if emit_lse:
            refs[8][rows, :] = lse_tile


def _band_attention(arr, *, q_col, k_col, v_col, n_heads, n_kv, half_w, slopes, dist_scale, sink, emit_lse,
                    out_dtype):
    nb, n_sub, _ = arr.shape
    hb = half_w
    sub = min(HEAD_DIM, n_sub)
    tq = _tile(n_sub, 512)
    assert sub % hb == 0 and tq % sub == 0 and n_sub % tq == 0
    per = tq // hb
    n_hb = n_sub // hb
    qw, kw = n_heads * HEAD_DIM, n_kv * HEAD_DIM
    assert q_col % qw == 0 and k_col % kw == 0 and v_col % kw == 0

    def halo_specs(c):
        return [
            pl.BlockSpec((None, hb, kw), lambda b, i: (b, jnp.maximum(i * per - 1, 0), c // kw)),
            pl.BlockSpec((None, tq, kw), lambda b, i: (b, i, c // kw)),
            pl.BlockSpec((None, hb, kw), lambda b, i: (b, jnp.minimum((i + 1) * per, n_hb - 1), c // kw)),
        ]

    in_specs = [pl.BlockSpec((None, tq, qw), lambda b, i: (b, i, q_col // qw))] + halo_specs(k_col) + halo_specs(v_col)
    args = [arr] * 7
    if sink is not None:
        in_specs = [pl.BlockSpec(memory_space=pltpu.SMEM)] + in_specs
        args = [sink] + args
    out_specs = [pl.BlockSpec((None, tq, qw), lambda b, i: (b, i, 0))]
    out_shape = [jax.ShapeDtypeStruct((nb, n_sub, qw), out_dtype)]
    if emit_lse:
        out_specs.append(pl.BlockSpec((None, tq, HEAD_DIM), lambda b, i: (b, i, 0)))
        out_shape.append(jax.ShapeDtypeStruct((nb, n_sub, HEAD_DIM), F32))
    return pl.pallas_call(
        functools.partial(_band_kernel, tq=tq, sub=sub, hb=hb, half_w=half_w, n_sub=n_sub, slopes=slopes,
                          dist_scale=dist_scale, n_kv=n_kv, has_sink=sink is not None, emit_lse=emit_lse),
        grid=(nb, n_sub // tq),
        in_specs=in_specs,
        out_specs=out_specs,
        out_shape=out_shape,
        compiler_params=_params("parallel", "arbitrary"),
        name=f"band_attn_n{n_sub}_w{half_w}",
    )(*args)


def _merge_kernel(o0_ref, o1_ref, o2_ref, l0_ref, l1_ref, l2_ref, out_ref, ob1_ref, ob2_ref, lb1_ref, lb2_ref):
    tm = out_ref.shape[0]
    for src_ref, dst_ref in ((o1_ref, ob1_ref), (o2_ref, ob2_ref), (l1_ref, lb1_ref), (l2_ref, lb2_ref)):
        r = src_ref.shape[0]
        for p in range(r):
            for h in range(dst_ref.shape[0]):
                dst_ref[h, pl.ds(p, tm // r, stride=r), :] = src_ref[p, :, h * HEAD_DIM:(h + 1) * HEAD_DIM]
    l0, l1, l2 = l0_ref[...], lb1_ref[0], lb2_ref[0]
    mx = jnp.maximum(jnp.maximum(l0, l1), l2)
    w0, w1, w2 = jnp.exp2(l0 - mx), jnp.exp2(l1 - mx), jnp.exp2(l2 - mx)
    den = w0 + w1 + w2
    a0, a1, a2 = w0 / den, w1 / den, w2 / den
    for h in range(A_HEADS_PER_GROUP):
        sl = slice(h * HEAD_DIM, (h + 1) * HEAD_DIM)
        out = a0[:, h:h + 1] * o0_ref[:, sl] + a1[:, h:h + 1] * ob1_ref[h] + a2[:, h:h + 1] * ob2_ref[h]
        out_ref[:, sl] = out.astype(BF16)


def _merge_groups(outs, lses, bsz, seq):
    n = bsz * seq
    tm = _tile(seq, 512)
    nseq = seq // tm
    r1, r2 = A_GROUPS[1][1], A_GROUPS[2][1]

    def specs(w):
        return [
            pl.BlockSpec((tm, w), lambda i: (i, 0)),
            pl.BlockSpec((None, r1, tm // r1, w), lambda i: (i // nseq, 0, i % nseq, 0)),
            pl.BlockSpec((None, r2, tm // r2, w), lambda i: (i // nseq, 0, i % nseq, 0)),
        ]

    def views(xs, w):
        return [xs[0].reshape(n, w), xs[1].reshape(bsz, r1, seq // r1, w), xs[2].reshape(bsz, r2, seq // r2, w)]

    return pl.pallas_call(
        _merge_kernel,
        grid=(n // tm,),
        in_specs=specs(A_OUT) + specs(HEAD_DIM),
        out_specs=pl.BlockSpec((tm, A_OUT), lambda i: (i, 0)),
        out_shape=jax.ShapeDtypeStruct((n, A_OUT), BF16),
        scratch_shapes=[pltpu.VMEM((A_HEADS_PER_GROUP, tm, HEAD_DIM), F32),
                        pltpu.VMEM((A_HEADS_PER_GROUP, tm, HEAD_DIM), F32),
                        pltpu.VMEM((1, tm, HEAD_DIM), F32), pltpu.VMEM((1, tm, HEAD_DIM), F32)],
        compiler_params=_params("parallel"),
        name="merge_dilated",
    )(*views(outs, A_OUT), *views(lses, HEAD_DIM))


def _dense_kernel(q_ref, k_ref, v_ref, o_ref, qt_ref, st_ref, m_ref, l_ref, acc_ref, *, tq, tk, seq):
    group = B_Q_HEADS // B_KV_HEADS
    n_blk = seq // tk
    for g in range(group):
        qg = q_ref[:, g * HEAD_DIM:(g + 1) * HEAD_DIM].astype(F32)
        qt_ref[:, g * tq:(g + 1) * tq] = qg.T.astype(BF16)
    m_ref[...] = jnp.full(m_ref.shape, -jnp.inf, F32)
    l_ref[...] = jnp.zeros(l_ref.shape, F32)
    acc_ref[...] = jnp.zeros(acc_ref.shape, F32)

    def scores(blk):
        k0 = pl.multiple_of(blk * tk, tk)
        return jnp.dot(k_ref[pl.ds(k0, tk), :], qt_ref[...], preferred_element_type=F32)

    st_ref[0] = scores(0)

    def step(blk, cur, nxt):
        st_ref[nxt] = scores(jnp.minimum(blk + 1, n_blk - 1))
        st = st_ref[cur]
        v0 = pl.multiple_of(blk * tk, tk)
        v = v_ref[pl.ds(v0, tk), :]
        m_prev = m_ref[...]
        m_new = jnp.maximum(m_prev, jnp.max(st, axis=0, keepdims=True))
        alpha = jnp.exp2(m_prev - m_new)
        p = jnp.exp2(st - m_new)
        l_ref[...] = alpha * l_ref[...] + jnp.sum(p, axis=0, keepdims=True)
        pv = lax.dot_general(v, p.astype(BF16), (((0,), (0,)), ((), ())), preferred_element_type=F32)
        acc_ref[...] = alpha * acc_ref[...] + pv
        m_ref[...] = m_new

    def body(c, carry):
        step(2 * c, 0, 1)
        step(2 * c + 1, 1, 0)
        return carry

    lax.fori_loop(0, n_blk // 2, body, 0)
    out = acc_ref[...] / l_ref[...]
    for g in range(group):
        o_ref[:, g * HEAD_DIM:(g + 1) * HEAD_DIM] = out[:, g * tq:(g + 1) * tq].T.astype(BF16)


def _dense_attention(arr):
    bsz, seq, _ = arr.shape
    tq = _tile(seq, 256)
    tk = _tile(seq // 2, 512)
    assert (seq // tk) % 2 == 0
    group = B_Q_HEADS // B_KV_HEADS
    qw = group * HEAD_DIM
    out = pl.pallas_call(
        functools.partial(_dense_kernel, tq=tq, tk=tk, seq=seq),
        grid=(bsz, B_KV_HEADS, seq // tq),
        in_specs=[
            pl.BlockSpec((None, tq, qw), lambda b, h, i: (b, i, h)),
            pl.BlockSpec((None, seq, HEAD_DIM), lambda b, h, i: (b, 0, B_K0 // HEAD_DIM + h)),
            pl.BlockSpec((None, seq, HEAD_DIM), lambda b, h, i: (b, 0, B_V0 // HEAD_DIM + h)),
        ],
        out_specs=pl.BlockSpec((None, tq, qw), lambda b, h, i: (b, i, h)),
        out_shape=jax.ShapeDtypeStruct((bsz, seq, B_OUT), BF16),
        scratch_shapes=[
            pltpu.VMEM((HEAD_DIM, group * tq), BF16),
            pltpu.VMEM((2, tk, group * tq), F32),
            pltpu.VMEM((1, group * tq), F32),
            pltpu.VMEM((1, group * tq), F32),
            pltpu.VMEM((HEAD_DIM, group * tq), F32),
        ],
        compiler_params=_params("parallel", "parallel", "arbitrary"),
        name="dense_attn",
    )(arr, arr, arr)
    return out.reshape(bsz * seq, B_OUT)


def _out_kernel(x_ref, g_ref, wg0_ref, wg1_ref, wg2_ref, oa_ref, ob_ref, oc_ref,
                wa_ref, wb_ref, wc_ref, wo_ref, o_ref, u_ref):
    j = pl.program_id(1)

    @pl.when(j == 0)
    def _():
        x = x_ref[...]
        u_ref[...] = _rms(x, g_ref[...]).astype(BF16)
        o_ref[...] = x

    u = u_ref[...]

    def branch(wg_ref, br_ref, w_ref):
        gate = jax.nn.sigmoid(jnp.dot(u, wg_ref[...], preferred_element_type=F32))
        return gate * jnp.dot(br_ref[...], w_ref[...], preferred_element_type=F32)

    merged = branch(wg0_ref, oa_ref, wa_ref) + branch(wg1_ref, ob_ref, wb_ref) + branch(wg2_ref, oc_ref, wc_ref)
    o_ref[...] += jnp.dot(merged.astype(BF16), wo_ref[...], preferred_element_type=F32)


def _mixer_out(x, g, w_in, o_a, o_b, o_c, w_a, w_b, w_c, w_o):
    n, d = x.shape
    tm = _tile(n, 512)
    tc = _tile(d, 256)
    assert QKV_COLS % tc == 0
    g0 = QKV_COLS // tc
    gstep = d // tc

    def gate_spec(br):
        return pl.BlockSpec((d, tc), lambda i, j: (0, g0 + br * gstep + j))

    return pl.pallas_call(
        _out_kernel,
        grid=(n // tm, d // tc),
        in_specs=[
            pl.BlockSpec((tm, d), lambda i, j: (i, 0)),
            pl.BlockSpec((1, d), lambda i, j: (0, 0)),
            gate_spec(0), gate_spec(1), gate_spec(2),
            pl.BlockSpec((tm, A_OUT), lambda i, j: (i, 0)),
            pl.BlockSpec((tm, B_OUT), lambda i, j: (i, 0)),
            pl.BlockSpec((tm, C_OUT), lambda i, j: (i, 0)),
            pl.BlockSpec((A_OUT, tc), lambda i, j: (0, j)),
            pl.BlockSpec((B_OUT, tc), lambda i, j: (0, j)),
            pl.BlockSpec((C_OUT, tc), lambda i, j: (0, j)),
            pl.BlockSpec((tc, d), lambda i, j: (j, 0)),
        ],
        out_specs=pl.BlockSpec((tm, d), lambda i, j: (i, 0)),
        out_shape=jax.ShapeDtypeStruct((n, d), F32),
        scratch_shapes=[pltpu.VMEM((tm, d), BF16)],
        compiler_params=_params("parallel", "arbitrary"),
        name="mixer_out",
    )(x, g, w_in, w_in, w_in, o_a, o_b, o_c, w_a, w_b, w_c, w_o)


def _final_norm_kernel(x_ref, g_ref, o_ref):
    o_ref[...] = _rms(x_ref[...], g_ref[...])


def _final_norm(x, g):
    n, d = x.shape
    tm = _tile(n, 1024)
    return pl.pallas_call(
        _final_norm_kernel,
        grid=(n // tm,),
        in_specs=[pl.BlockSpec((tm, d), lambda i: (i, 0)), pl.BlockSpec((1, d), lambda i: (0, 0))],
        out_specs=pl.BlockSpec((tm, d), lambda i: (i, 0)),
        out_shape=jax.ShapeDtypeStruct((n, d), F32),
        compiler_params=_params("parallel"),
        name="final_norm",
    )(x, g)


def _rope_tables(seq):
    n_freq = HEAD_DIM // 4
    t = lax.broadcasted_iota(jnp.int32, (seq, HEAD_DIM), 0)
    lane = lax.broadcasted_iota(jnp.int32, (seq, HEAD_DIM), 1)
    pair = lane // 2
    pos = jnp.where(pair < n_freq, t // GRID_W, t % GRID_W).astype(F32)
    inv = ROPE_THETA ** (-(pair % n_freq).astype(F32) / n_freq)
    ang = pos * inv
    sign = jnp.where(lane % 2 == 0, -1.0, 1.0).astype(F32)
    return jnp.cos(ang), jnp.sin(ang) * sign


def _mixer(x, lw, cos, sin_signed, bsz, seq):
    a0, a1, a2, bqkv, cqkv = _qkv(x, lw["norm_mix"], lw["w_in"], cos, sin_signed, lw["qk_norm_q"],
                                  lw["qk_norm_k"], bsz, seq)
    a_slopes = _alibi_slopes(A_HEADS)
    outs, lses = [], []
    for gi, (arr, (window, r)) in enumerate(zip((a0, a1, a2), A_GROUPS)):
        hs = slice(gi * A_HEADS_PER_GROUP, (gi + 1) * A_HEADS_PER_GROUP)
        o, l = _band_attention(
            arr.reshape(bsz * r, seq // r, PROJ_W), q_col=0, k_col=GROUP_W, v_col=2 * GROUP_W,
            n_heads=A_HEADS_PER_GROUP, n_kv=A_HEADS_PER_GROUP, half_w=(window // 2) // r, slopes=a_slopes[hs],
            dist_scale=r, sink=None, emit_lse=True, out_dtype=F32)
        outs.append(o)
        lses.append(l)
    o_a = _merge_groups(outs, lses, bsz, seq)
    o_b = _dense_attention(bqkv.reshape(bsz, seq, PROJ_W))
    o_c, = _band_attention(
        cqkv.reshape(bsz, seq, PROJ_W), q_col=0, k_col=C_K0, v_col=C_V0, n_heads=C_Q_HEADS, n_kv=C_KV_HEADS,
        half_w=C_HALF_WINDOW, slopes=_alibi_slopes(C_Q_HEADS), dist_scale=1, sink=lw["sink_c"],
        emit_lse=False, out_dtype=BF16)
    return _mixer_out(x, lw["norm_mix"], lw["w_in"], o_a, o_b, o_c.reshape(bsz * seq, C_OUT), lw["w_br_a"],
                      lw["w_br_b"], lw["w_br_c"], lw["w_out"])


def _trunk(x, layers, norm_final):
    bsz, seq, d = x.shape
    cos, sin_signed = _rope_tables(seq)
    xf = x.reshape(bsz * seq, d)

    def layer(xf, lw):
        xf = _ffn(xf, lw["norm_ffn1"], lw["ffn1_w_in"], lw["ffn1_w_out"])
        xf = _mixer(xf, lw, cos, sin_signed, bsz, seq)
        xf = _ffn(xf, lw["norm_ffn2"], lw["ffn2_w_in"], lw["ffn2_w_out"])
        return xf, None

    xf, _ = lax.scan(layer, xf, layers)
    return _final_norm(xf, norm_final.reshape(1, d)).reshape(bsz, seq, d)


def kernel(x_prompt, x_sample, norm_ffn1, ffn1_w_in, ffn1_w_out, norm_mix, w_in, qk_norm_q, qk_norm_k, sink_c,
           w_br_a, w_br_b, w_br_c, w_out, norm_ffn2, ffn2_w_in, ffn2_w_out, norm_final):
    depth = norm_ffn1.shape[0]
    row = lambda v: v.reshape(depth, 1, v.shape[-1])
    layers = {
        "norm_ffn1": row(norm_ffn1), "ffn1_w_in": ffn1_w_in.astype(BF16), "ffn1_w_out": ffn1_w_out.astype(BF16),
        "norm_mix": row(norm_mix), "w_in": w_in.astype(BF16),
        "qk_norm_q": row(qk_norm_q), "qk_norm_k": row(qk_norm_k), "sink_c": row(sink_c),
        "w_br_a": w_br_a.astype(BF16), "w_br_b": w_br_b.astype(BF16), "w_br_c": w_br_c.astype(BF16),
        "w_out": w_out.astype(BF16),
        "norm_ffn2": row(norm_ffn2), "ffn2_w_in": ffn2_w_in.astype(BF16), "ffn2_w_out": ffn2_w_out.astype(BF16),
    }
    assert x_prompt.shape[1:] == x_sample.shape[1:]
    n_prompt = x_prompt.shape[0]
    y = _trunk(jnp.concatenate([x_prompt, x_sample], axis=0), layers, norm_final)
    return (y[:n_prompt], y[n_prompt:])
```

```python
import functools

import jax
import jax.numpy as jnp
from jax import lax
from jax.experimental import pallas as pl
from jax.experimental.pallas import tpu as pltpu

F32 = jnp.float32
BF16 = jnp.bfloat16

HEAD_DIM = 128
SCALE = HEAD_DIM ** -0.5
LOG2_E = 1.4426950408889634
Q_SCALE = SCALE * LOG2_E
NORM_EPS = 1e-6
NEG_INF = -1e30
F32_HUGE = 1e38
LANE_CHUNKS = 4
ROPE_THETA = 10000.0
GRID_W = 64
A_GROUPS = ((128, 1), (512, 4), (2048, 16))
A_HEADS_PER_GROUP = 4
A_HEADS = 12
B_Q_HEADS = 8
B_KV_HEADS = 2
C_Q_HEADS = 8
C_KV_HEADS = 2
C_HALF_WINDOW = 128
N_BRANCHES = 3

QKV_COLS = 7680
PROJ_W = 1536
GROUP_W = 4 * HEAD_DIM
A_OUT = A_HEADS_PER_GROUP * HEAD_DIM
B_OUT = B_Q_HEADS * HEAD_DIM
C_OUT = C_Q_HEADS * HEAD_DIM
B_K0, B_V0 = B_OUT, B_OUT + B_KV_HEADS * HEAD_DIM
C_K0, C_V0 = C_OUT, C_OUT + C_KV_HEADS * HEAD_DIM

V7X_VMEM_LIMIT_BYTES = 56 * 1024 * 1024


def _tile(n, pref):
    t = min(pref, n)
    while n % t:
        t //= 2
    return t


def _params(*sem):
    return pltpu.CompilerParams(dimension_semantics=sem, vmem_limit_bytes=V7X_VMEM_LIMIT_BYTES)


def _rms(x, g):
    ms = jnp.mean(x * x, axis=-1, keepdims=True)
    return x * lax.rsqrt(ms + NORM_EPS) * g


def _alibi_slopes(n):
    return tuple(2.0 ** (-8.0 * i / n) for i in range(1, n + 1))


def _ffn_kernel(x_ref, g_ref, wa_ref, wb_ref, wo_ref, o_ref, xn_ref):
    j = pl.program_id(1)

    @pl.when(j == 0)
    def _():
        x = x_ref[...]
        xn_ref[...] = _rms(x, g_ref[...]).astype(BF16)
        o_ref[...] = x

    xn = xn_ref[...]
    a = jnp.dot(xn, wa_ref[...], preferred_element_type=F32)
    b = jnp.dot(xn, wb_ref[...], preferred_element_type=F32)
    h = (0.5 * a * jax.nn.sigmoid(a) * b).astype(BF16)
    o_ref[...] += jnp.dot(h, wo_ref[...], preferred_element_type=F32)


def _ffn(x, g, w_in, w_out):
    n, d = x.shape
    d_ff = w_out.shape[0]
    tm = _tile(n, 512)
    tf = _tile(d_ff, 512)
    nf = d_ff // tf
    return pl.pallas_call(
        _ffn_kernel,
        grid=(n // tm, nf),
        in_specs=[
            pl.BlockSpec((tm, d), lambda i, j: (i, 0)),
            pl.BlockSpec((1, d), lambda i, j: (0, 0)),
            pl.BlockSpec((d, tf), lambda i, j: (0, j)),
            pl.BlockSpec((d, tf), lambda i, j: (0, j + nf)),
            pl.BlockSpec((tf, d), lambda i, j: (j, 0)),
        ],
        out_specs=pl.BlockSpec((tm, d), lambda i, j: (i, 0)),
        out_shape=jax.ShapeDtypeStruct((n, d), F32),
        scratch_shapes=[pltpu.VMEM((tm, d), BF16)],
        compiler_params=_params("parallel", "arbitrary"),
        name="ffn",
    )(x, g, w_in, w_in, w_out)


def _norm_rope(y, gain, cos, sin_signed):
    yn = _rms(y, gain)
    lane = lax.broadcasted_iota(jnp.int32, yn.shape, 1)
    partner = jnp.where((lane & 1) == 0, pltpu.roll(yn, HEAD_DIM - 1, 1), pltpu.roll(yn, 1, 1))
    return yn * cos + partner * sin_signed


def _qkv_kernel(x_ref, g_ref, w_ref, cos_ref, sin_ref, qg_ref, kg_ref,
                a0_ref, a1_ref, a2_ref, b_ref, c_ref, u_ref, split_ref):
    j = pl.program_id(1)
    tm = x_ref.shape[0]

    @pl.when(j == 0)
    def _():
        u_ref[...] = _rms(x_ref[...], g_ref[...]).astype(BF16)

    acc = jnp.dot(u_ref[...], w_ref[...], preferred_element_type=F32)

    @pl.when(j < 3)
    def _():
        y = acc * jnp.where(j == 0, Q_SCALE, 1.0).astype(F32)
        a0_ref[...] = y[:, :GROUP_W].astype(BF16)
        for c in range(split_ref.shape[0]):
            split_ref[c] = y[:, GROUP_W + c * HEAD_DIM:GROUP_W + (c + 1) * HEAD_DIM]
        for gi, dst_ref in ((1, a1_ref), (2, a2_ref)):
            r = A_GROUPS[gi][1]
            for p in range(r):
                for h in range(A_HEADS_PER_GROUP):
                    rows = split_ref[(gi - 1) * A_HEADS_PER_GROUP + h, pl.ds(p, tm // r, stride=r), :]
                    dst_ref[p, :, h * HEAD_DIM:(h + 1) * HEAD_DIM] = rows.astype(BF16)

    @pl.when(j == 3)
    def _():
        for h in range(B_Q_HEADS + B_KV_HEADS):
            sl = slice(h * HEAD_DIM, (h + 1) * HEAD_DIM)
            is_q = h < B_Q_HEADS
            y = _norm_rope(acc[:, sl], (qg_ref if is_q else kg_ref)[...], cos_ref[...], sin_ref[...])
            b_ref[:, sl] = ((y * Q_SCALE) if is_q else y).astype(BF16)
        b_ref[:, B_V0:] = acc[:, B_V0:].astype(BF16)

    @pl.when(j == 4)
    def _():
        c_ref[:, :C_OUT] = (acc[:, :C_OUT] * Q_SCALE).astype(BF16)
        c_ref[:, C_OUT:] = acc[:, C_OUT:].astype(BF16)


def _qkv(x, g, w_in, cos, sin_signed, qg, kg, bsz, seq):
    n, d = x.shape
    tm = _tile(seq, 512)
    nseq = seq // tm
    r1, r2 = A_GROUPS[1][1], A_GROUPS[2][1]
    assert tm % (16 * r2) == 0
    a_part = lambda i, j: jnp.minimum(j, 2)

    def split_spec(r):
        return pl.BlockSpec((None, r, tm // r, GROUP_W), lambda i, j: (i // nseq, 0, i % nseq, a_part(i, j)))

    return pl.pallas_call(
        _qkv_kernel,
        grid=(n // tm, QKV_COLS // PROJ_W),
        in_specs=[
            pl.BlockSpec((tm, d), lambda i, j: (i, 0)),
            pl.BlockSpec((1, d), lambda i, j: (0, 0)),
            pl.BlockSpec((d, PROJ_W), lambda i, j: (0, j)),
            pl.BlockSpec((tm, HEAD_DIM), lambda i, j: (i % nseq, 0)),
            pl.BlockSpec((tm, HEAD_DIM), lambda i, j: (i % nseq, 0)),
            pl.BlockSpec((1, HEAD_DIM), lambda i, j: (0, 0)),
            pl.BlockSpec((1, HEAD_DIM), lambda i, j: (0, 0)),
        ],
        out_specs=[
            pl.BlockSpec((tm, GROUP_W), lambda i, j: (i, a_part(i, j))),
            split_spec(r1),
            split_spec(r2),
            pl.BlockSpec((tm, PROJ_W), lambda i, j: (i, 0)),
            pl.BlockSpec((tm, PROJ_W), lambda i, j: (i, 0)),
        ],
        out_shape=[
            jax.ShapeDtypeStruct((n, PROJ_W), BF16),
            jax.ShapeDtypeStruct((bsz, r1, seq // r1, PROJ_W), BF16),
            jax.ShapeDtypeStruct((bsz, r2, seq // r2, PROJ_W), BF16),
            jax.ShapeDtypeStruct((n, PROJ_W), BF16),
            jax.ShapeDtypeStruct((n, PROJ_W), BF16),
        ],
        scratch_shapes=[pltpu.VMEM((tm, d), BF16),
                        pltpu.VMEM(((PROJ_W - GROUP_W) // HEAD_DIM, tm, HEAD_DIM), F32)],
        compiler_params=_params("parallel", "arbitrary"),
        name="qkv_proj",
    )(x, g, w_in, cos, sin_signed, qg, kg)


def _band_kernel(*refs, tq, sub, hb, half_w, n_sub, slopes, dist_scale, n_kv, has_sink, emit_lse):
    if has_sink:
        sink_ref, refs = refs[0], refs[1:]
    q_ref, kp_ref, km_ref, kn_ref, vp_ref, vm_ref, vn_ref, o_ref = refs[:8]
    n_heads = len(slopes)
    group = n_heads // n_kv
    qi = pl.program_id(1)
    span = sub + 2 * hb
    row = lax.broadcasted_iota(jnp.int32, (sub, span), 0)
    col = lax.broadcasted_iota(jnp.int32, (sub, span), 1)
    arel = jnp.abs(col - hb - row)
    in_band = arel <= half_w
    dist = (arel * dist_scale).astype(F32)
    biases = [jnp.where(in_band, (-slopes[h] * LOG2_E) * dist, NEG_INF) for h in range(n_heads)]
    kcat, vcat = [], []
    for kv in range(n_kv):
        ksl = slice(kv * HEAD_DIM, (kv + 1) * HEAD_DIM)
        kcat.append(jnp.concatenate([kp_ref[:, ksl], km_ref[:, ksl], kn_ref[:, ksl]], axis=0))
        vcat.append(jnp.concatenate([vp_ref[:, ksl], vm_ref[:, ksl], vn_ref[:, ksl]], axis=0))
    if emit_lse:
        lane = lax.broadcasted_iota(jnp.int32, (sub, HEAD_DIM), 1)
    for s in range(tq // sub):
        rows = slice(s * sub, (s + 1) * sub)
        kpos = qi * tq + (s * sub - hb) + col
        in_seq = (kpos >= 0) & (kpos < n_sub)
        lse_tile = jnp.zeros((sub, HEAD_DIM), F32) if emit_lse else None
        for h in range(n_heads):
            sl = slice(h * HEAD_DIM, (h + 1) * HEAD_DIM)
            ks = kcat[h // group][s * sub:s * sub + span]
            vs = vcat[h // group][s * sub:s * sub + span]
            sc = lax.dot_general(q_ref[rows, sl], ks, (((1,), (1,)), ((), ())), preferred_element_type=F32)
            sc = jnp.where(in_seq, sc + biases[h], NEG_INF)
            m = jnp.max(sc, axis=-1, keepdims=True)
            if has_sink:
                sk = sink_ref[0, h] * LOG2_E
                m = jnp.maximum(m, sk)
            p = jnp.exp2(sc - m)
            den = jnp.sum(p, axis=-1, keepdims=True)
            if has_sink:
                den = den + jnp.exp2(sk - m)
            o = jnp.dot(p.astype(BF16), vs, preferred_element_type=F32) * (1.0 / den)
            o_ref[rows, sl] = o.astype(o_ref.dtype)
            if emit_lse:
                lse_tile = jnp.where(lane == h, m + jnp.log2(den), lse_tile)
        if emit_lse:
            refs[8][rows, :] = lse_tile


def _band_attention(arr, *, q_col, k_col, v_col, n_heads, n_kv, half_w, slopes, dist_scale, sink, emit_lse,
                    out_dtype):
    nb, n_sub, _ = arr.shape
    hb = half_w
    sub = min(HEAD_DIM, n_sub)
    tq = _tile(n_sub, 512)
    assert sub % hb == 0 and tq % sub == 0 and n_sub % tq == 0
    per = tq // hb
    n_hb = n_sub // hb
    qw, kw = n_heads * HEAD_DIM, n_kv * HEAD_DIM
    assert q_col % qw == 0 and k_col % kw == 0 and v_col % kw == 0

    def halo_specs(c):
        return [
            pl.BlockSpec((None, hb, kw), lambda b, i: (b, jnp.maximum(i * per - 1, 0), c // kw)),
            pl.BlockSpec((None, tq, kw), lambda b, i: (b, i, c // kw)),
            pl.BlockSpec((None, hb, kw), lambda b, i: (b, jnp.minimum((i + 1) * per, n_hb - 1), c // kw)),
        ]

    in_specs = [pl.BlockSpec((None, tq, qw), lambda b, i: (b, i, q_col // qw))] + halo_specs(k_col) + halo_specs(v_col)
    args = [arr] * 7
    if sink is not None:
        in_specs = [pl.BlockSpec(memory_space=pltpu.SMEM)] + in_specs
        args = [sink] + args
    out_specs = [pl.BlockSpec((None, tq, qw), lambda b, i: (b, i, 0))]
    out_shape = [jax.ShapeDtypeStruct((nb, n_sub, qw), out_dtype)]
    if emit_lse:
        out_specs.append(pl.BlockSpec((None, tq, HEAD_DIM), lambda b, i: (b, i, 0)))
        out_shape.append(jax.ShapeDtypeStruct((nb, n_sub, HEAD_DIM), F32))
    return pl.pallas_call(
        functools.partial(_band_kernel, tq=tq, sub=sub, hb=hb, half_w=half_w, n_sub=n_sub, slopes=slopes,
                          dist_scale=dist_scale, n_kv=n_kv, has_sink=sink is not None, emit_lse=emit_lse),
        grid=(nb, n_sub // tq),
        in_specs=in_specs,
        out_specs=out_specs,
        out_shape=out_shape,
        compiler_params=_params("parallel", "arbitrary"),
        name=f"band_attn_n{n_sub}_w{half_w}",
    )(*args)


def _merge_kernel(o0_ref, o1_ref, o2_ref, l0_ref, l1_ref, l2_ref, out_ref, ob1_ref, ob2_ref, lb1_ref, lb2_ref):
    tm = out_ref.shape[0]
    for src_ref, dst_ref in ((o1_ref, ob1_ref), (o2_ref, ob2_ref), (l1_ref, lb1_ref), (l2_ref, lb2_ref)):
        r = src_ref.shape[0]
        for p in range(r):
            for h in range(dst_ref.shape[0]):
                dst_ref[h, pl.ds(p, tm // r, stride=r), :] = src_ref[p, :, h * HEAD_DIM:(h + 1) * HEAD_DIM]
    l0, l1, l2 = l0_ref[...], lb1_ref[0], lb2_ref[0]
    mx = jnp.maximum(jnp.maximum(l0, l1), l2)
    w0, w1, w2 = jnp.exp2(l0 - mx), jnp.exp2(l1 - mx), jnp.exp2(l2 - mx)
    den = w0 + w1 + w2
    a0, a1, a2 = w0 / den, w1 / den, w2 / den
    for h in range(A_HEADS_PER_GROUP):
        sl = slice(h * HEAD_DIM, (h + 1) * HEAD_DIM)
        out = a0[:, h:h + 1] * o0_ref[:, sl] + a1[:, h:h + 1] * ob1_ref[h] + a2[:, h:h + 1] * ob2_ref[h]
        out_ref[:, sl] = out.astype(BF16)


def _merge_groups(outs, lses, bsz, seq):
    n = bsz * seq
    tm = _tile(seq, 512)
    nseq = seq // tm
    r1, r2 = A_GROUPS[1][1], A_GROUPS[2][1]

    def specs(w):
        return [
            pl.BlockSpec((tm, w), lambda i: (i, 0)),
            pl.BlockSpec((None, r1, tm // r1, w), lambda i: (i // nseq, 0, i % nseq, 0)),
            pl.BlockSpec((None, r2, tm // r2, w), lambda i: (i // nseq, 0, i % nseq, 0)),
        ]

    def views(xs, w):
        return [xs[0].reshape(n, w), xs[1].reshape(bsz, r1, seq // r1, w), xs[2].reshape(bsz, r2, seq // r2, w)]

    return pl.pallas_call(
        _merge_kernel,
        grid=(n // tm,),
        in_specs=specs(A_OUT) + specs(HEAD_DIM),
        out_specs=pl.BlockSpec((tm, A_OUT), lambda i: (i, 0)),
        out_shape=jax.ShapeDtypeStruct((n, A_OUT), BF16),
        scratch_shapes=[pltpu.VMEM((A_HEADS_PER_GROUP, tm, HEAD_DIM), F32),
                        pltpu.VMEM((A_HEADS_PER_GROUP, tm, HEAD_DIM), F32),
                        pltpu.VMEM((1, tm, HEAD_DIM), F32), pltpu.VMEM((1, tm, HEAD_DIM), F32)],
        compiler_params=_params("parallel"),
        name="merge_dilated",
    )(*views(outs, A_OUT), *views(lses, HEAD_DIM))


def _dense_kernel(q_ref, k_ref, v_ref, o_ref, qt_ref, st_ref, m_ref, l_ref, acc_ref, *, tq, tk, seq):
    group = B_Q_HEADS // B_KV_HEADS
    n_blk = seq // tk
    for g in range(group):
        qg = q_ref[:, g * HEAD_DIM:(g + 1) * HEAD_DIM].astype(F32)
        qt_ref[:, g * tq:(g + 1) * tq] = qg.T.astype(BF16)

    def scores(blk):
        k0 = pl.multiple_of(blk * tk, tk)
        return jnp.dot(k_ref[pl.ds(k0, tk), :], qt_ref[...], preferred_element_type=F32)

    def values(blk):
        v0 = pl.multiple_of(blk * tk, tk)
        return v_ref[pl.ds(v0, tk), :]

    def pv(v, p):
        return lax.dot_general(v, p.astype(BF16), (((0,), (0,)), ((), ())), preferred_element_type=F32)

    def write_out():
        out = acc_ref[...] / l_ref[...]
        for g in range(group):
            o_ref[:, g * HEAD_DIM:(g + 1) * HEAD_DIM] = out[:, g * tq:(g + 1) * tq].T.astype(BF16)

    m_ref[...] = jnp.max(scores(0), axis=0, keepdims=True)
    l_ref[...] = jnp.zeros(l_ref.shape, F32)
    acc_ref[...] = jnp.zeros(acc_ref.shape, F32)

    def lazy_step(blk):
        k0 = pl.multiple_of(blk * tk, tk)
        k = k_ref[pl.ds(k0, tk), :]
        v = values(blk)
        width = group * tq // LANE_CHUNKS
        chunks = [slice(c * width, (c + 1) * width) for c in range(LANE_CHUNKS)]
        sts = [jnp.dot(k, qt_ref[:, sl], preferred_element_type=F32) for sl in chunks]
        for sl, st in zip(chunks, sts):
            p = jnp.exp2(st - m_ref[:, sl])
            l_ref[:, sl] += jnp.sum(p, axis=0, keepdims=True)
            acc_ref[:, sl] += pv(v, p)

    def lazy_body(c, carry):
        lazy_step(2 * c)
        lazy_step(2 * c + 1)
        return carry

    lax.fori_loop(0, n_blk // 2, lazy_body, 0)
    l = l_ref[...]
    sums_ok = jnp.min(jnp.where(jnp.abs(acc_ref[...]) < F32_HUGE, 1.0, 0.0))
    sums_ok = sums_ok * jnp.min(jnp.where((l > 0.0) & (l < F32_HUGE), 1.0, 0.0))
    exceeded = sums_ok < 0.5

    @pl.when(jnp.logical_not(exceeded))
    def _():
        write_out()

    @pl.when(exceeded)
    def _():
        _dense_online(scores, values, pv, write_out, st_ref, m_ref, l_ref, acc_ref, n_blk)


def _dense_online(scores, values, pv, write_out, st_ref, m_ref, l_ref, acc_ref, n_blk):
    m_ref[...] = jnp.full(m_ref.shape, -jnp.inf, F32)
    l_ref[...] = jnp.zeros(l_ref.shape, F32)
    acc_ref[...] = jnp.zeros(acc_ref.shape, F32)
    st_ref[0] = scores(0)

    def step(blk, cur, nxt):
        st_ref[nxt] = scores(jnp.minimum(blk + 1, n_blk - 1))
        st = st_ref[cur]
        m_prev = m_ref[...]
        m_new = jnp.maximum(m_prev, jnp.max(st, axis=0, keepdims=True))
        alpha = jnp.exp2(m_prev - m_new)
        p = jnp.exp2(st - m_new)
        l_ref[...] = alpha * l_ref[...] + jnp.sum(p, axis=0, keepdims=True)
        acc_ref[...] = alpha * acc_ref[...] + pv(values(blk), p)
        m_ref[...] = m_new

    def body(c, carry):
        step(2 * c, 0, 1)
        step(2 * c + 1, 1, 0)
        return carry

    lax.fori_loop(0, n_blk // 2, body, 0)
    write_out()


def _dense_attention(arr):
    bsz, seq, _ = arr.shape
    tq = _tile(seq, 256)
    tk = _tile(seq // 2, 512)
    assert (seq // tk) % 2 == 0
    group = B_Q_HEADS // B_KV_HEADS
    qw = group * HEAD_DIM
    out = pl.pallas_call(
        functools.partial(_dense_kernel, tq=tq, tk=tk, seq=seq),
        grid=(bsz, B_KV_HEADS, seq // tq),
        in_specs=[
            pl.BlockSpec((None, tq, qw), lambda b, h, i: (b, i, h)),
            pl.BlockSpec((None, seq, HEAD_DIM), lambda b, h, i: (b, 0, B_K0 // HEAD_DIM + h)),
            pl.BlockSpec((None, seq, HEAD_DIM), lambda b, h, i: (b, 0, B_V0 // HEAD_DIM + h)),
        ],
        out_specs=pl.BlockSpec((None, tq, qw), lambda b, h, i: (b, i, h)),
        out_shape=jax.ShapeDtypeStruct((bsz, seq, B_OUT), BF16),
        scratch_shapes=[
            pltpu.VMEM((HEAD_DIM, group * tq), BF16),
            pltpu.VMEM((2, tk, group * tq), F32),
            pltpu.VMEM((1, group * tq), F32),
            pltpu.VMEM((1, group * tq), F32),
            pltpu.VMEM((HEAD_DIM, group * tq), F32),
        ],
        compiler_params=_params("parallel", "parallel", "arbitrary"),
        name="dense_attn",
    )(arr, arr, arr)
    return out.reshape(bsz * seq, B_OUT)


def _out_kernel(x_ref, g_ref, wg0_ref, wg1_ref, wg2_ref, oa_ref, ob_ref, oc_ref,
                wa_ref, wb_ref, wc_ref, wo_ref, o_ref, u_ref):
    j = pl.program_id(1)

    @pl.when(j == 0)
    def _():
        x = x_ref[...]
        u_ref[...] = _rms(x, g_ref[...]).astype(BF16)
        o_ref[...] = x

    u = u_ref[...]

    def branch(wg_ref, br_ref, w_ref):
        gate = jax.nn.sigmoid(jnp.dot(u, wg_ref[...], preferred_element_type=F32))
        return gate * jnp.dot(br_ref[...], w_ref[...], preferred_element_type=F32)

    merged = branch(wg0_ref, oa_ref, wa_ref) + branch(wg1_ref, ob_ref, wb_ref) + branch(wg2_ref, oc_ref, wc_ref)
    o_ref[...] += jnp.dot(merged.astype(BF16), wo_ref[...], preferred_element_type=F32)


def _mixer_out(x, g, w_in, o_a, o_b, o_c, w_a, w_b, w_c, w_o):
    n, d = x.shape
    tm = _tile(n, 512)
    tc = _tile(d, 256)
    assert QKV_COLS % tc == 0
    g0 = QKV_COLS // tc
    gstep = d // tc

    def gate_spec(br):
        return pl.BlockSpec((d, tc), lambda i, j: (0, g0 + br * gstep + j))

    return pl.pallas_call(
        _out_kernel,
        grid=(n // tm, d // tc),
        in_specs=[
            pl.BlockSpec((tm, d), lambda i, j: (i, 0)),
            pl.BlockSpec((1, d), lambda i, j: (0, 0)),
            gate_spec(0), gate_spec(1), gate_spec(2),
            pl.BlockSpec((tm, A_OUT), lambda i, j: (i, 0)),
            pl.BlockSpec((tm, B_OUT), lambda i, j: (i, 0)),
            pl.BlockSpec((tm, C_OUT), lambda i, j: (i, 0)),
            pl.BlockSpec((A_OUT, tc), lambda i, j: (0, j)),
            pl.BlockSpec((B_OUT, tc), lambda i, j: (0, j)),
            pl.BlockSpec((C_OUT, tc), lambda i, j: (0, j)),
            pl.BlockSpec((tc, d), lambda i, j: (j, 0)),
        ],
        out_specs=pl.BlockSpec((tm, d), lambda i, j: (i, 0)),
        out_shape=jax.ShapeDtypeStruct((n, d), F32),
        scratch_shapes=[pltpu.VMEM((tm, d), BF16)],
        compiler_params=_params("parallel", "arbitrary"),
        name="mixer_out",
    )(x, g, w_in, w_in, w_in, o_a, o_b, o_c, w_a, w_b, w_c, w_o)


def _final_norm_kernel(x_ref, g_ref, o_ref):
    o_ref[...] = _rms(x_ref[...], g_ref[...])


def _final_norm(x, g):
    n, d = x.shape
    tm = _tile(n, 1024)
    return pl.pallas_call(
        _final_norm_kernel,
        grid=(n // tm,),
        in_specs=[pl.BlockSpec((tm, d), lambda i: (i, 0)), pl.BlockSpec((1, d), lambda i: (0, 0))],
        out_specs=pl.BlockSpec((tm, d), lambda i: (i, 0)),
        out_shape=jax.ShapeDtypeStruct((n, d), F32),
        compiler_params=_params("parallel"),
        name="final_norm",
    )(x, g)


def _rope_tables(seq):
    n_freq = HEAD_DIM // 4
    t = lax.broadcasted_iota(jnp.int32, (seq, HEAD_DIM), 0)
    lane = lax.broadcasted_iota(jnp.int32, (seq, HEAD_DIM), 1)
    pair = lane // 2
    pos = jnp.where(pair < n_freq, t // GRID_W, t % GRID_W).astype(F32)
    inv = ROPE_THETA ** (-(pair % n_freq).astype(F32) / n_freq)
    ang = pos * inv
    sign = jnp.where(lane % 2 == 0, -1.0, 1.0).astype(F32)
    return jnp.cos(ang), jnp.sin(ang) * sign


def _mixer(x, lw, cos, sin_signed, bsz, seq):
    a0, a1, a2, bqkv, cqkv = _qkv(x, lw["norm_mix"], lw["w_in"], cos, sin_signed, lw["qk_norm_q"],
                                  lw["qk_norm_k"], bsz, seq)
    a_slopes = _alibi_slopes(A_HEADS)
    outs, lses = [], []
    for gi, (arr, (window, r)) in enumerate(zip((a0, a1, a2), A_GROUPS)):
        hs = slice(gi * A_HEADS_PER_GROUP, (gi + 1) * A_HEADS_PER_GROUP)
        o, l = _band_attention(
            arr.reshape(bsz * r, seq // r, PROJ_W), q_col=0, k_col=GROUP_W, v_col=2 * GROUP_W,
            n_heads=A_HEADS_PER_GROUP, n_kv=A_HEADS_PER_GROUP, half_w=(window // 2) // r, slopes=a_slopes[hs],
            dist_scale=r, sink=None, emit_lse=True, out_dtype=F32)
        outs.append(o)
        lses.append(l)
    o_a = _merge_groups(outs, lses, bsz, seq)
    o_b = _dense_attention(bqkv.reshape(bsz, seq, PROJ_W))
    o_c, = _band_attention(
        cqkv.reshape(bsz, seq, PROJ_W), q_col=0, k_col=C_K0, v_col=C_V0, n_heads=C_Q_HEADS, n_kv=C_KV_HEADS,
        half_w=C_HALF_WINDOW, slopes=_alibi_slopes(C_Q_HEADS), dist_scale=1, sink=lw["sink_c"],
        emit_lse=False, out_dtype=BF16)
    return _mixer_out(x, lw["norm_mix"], lw["w_in"], o_a, o_b, o_c.reshape(bsz * seq, C_OUT), lw["w_br_a"],
                      lw["w_br_b"], lw["w_br_c"], lw["w_out"])


def _trunk(x, layers, norm_final):
    bsz, seq, d = x.shape
    cos, sin_signed = _rope_tables(seq)
    xf = x.reshape(bsz * seq, d)

    def layer(xf, lw):
        xf = _ffn(xf, lw["norm_ffn1"], lw["ffn1_w_in"], lw["ffn1_w_out"])
        xf = _mixer(xf, lw, cos, sin_signed, bsz, seq)
        xf = _ffn(xf, lw["norm_ffn2"], lw["ffn2_w_in"], lw["ffn2_w_out"])
        return xf, None

    xf, _ = lax.scan(layer, xf, layers)
    return _final_norm(xf, norm_final.reshape(1, d)).reshape(bsz, seq, d)


def kernel(x_prompt, x_sample, norm_ffn1, ffn1_w_in, ffn1_w_out, norm_mix, w_in, qk_norm_q, qk_norm_k, sink_c,
           w_br_a, w_br_b, w_br_c, w_out, norm_ffn2, ffn2_w_in, ffn2_w_out, norm_final):
    depth = norm_ffn1.shape[0]
    row = lambda v: v.reshape(depth, 1, v.shape[-1])
    layers = {
        "norm_ffn1": row(norm_ffn1), "ffn1_w_in": ffn1_w_in.astype(BF16), "ffn1_w_out": ffn1_w_out.astype(BF16),
        "norm_mix": row(norm_mix), "w_in": w_in.astype(BF16),
        "qk_norm_q": row(qk_norm_q), "qk_norm_k": row(qk_norm_k), "sink_c": row(sink_c),
        "w_br_a": w_br_a.astype(BF16), "w_br_b": w_br_b.astype(BF16), "w_br_c": w_br_c.astype(BF16),
        "w_out": w_out.astype(BF16),
        "norm_ffn2": row(norm_ffn2), "ffn2_w_in": ffn2_w_in.astype(BF16), "ffn2_w_out": ffn2_w_out.astype(BF16),
    }
    assert x_prompt.shape[1:] == x_sample.shape[1:]
    n_prompt = x_prompt.shape[0]
    y = _trunk(jnp.concatenate([x_prompt, x_sample], axis=0), layers, norm_final)
    return (y[:n_prompt], y[n_prompt:])
```

```python
import functools

import jax
import jax.numpy as jnp
from jax import lax
from jax.experimental import pallas as pl
from jax.experimental.pallas import tpu as pltpu

F32 = jnp.float32
BF16 = jnp.bfloat16

HEAD_DIM = 128
SCALE = HEAD_DIM ** -0.5
LOG2_E = 1.4426950408889634
Q_SCALE = SCALE * LOG2_E
NORM_EPS = 1e-6
NEG_INF = -1e30
F32_HUGE = 1e38
LANE_CHUNK_W = 256
ROPE_THETA = 10000.0
GRID_W = 64
A_GROUPS = ((128, 1), (512, 4), (2048, 16))
A_HEADS_PER_GROUP = 4
A_HEADS = 12
B_Q_HEADS = 8
B_KV_HEADS = 2
C_Q_HEADS = 8
C_KV_HEADS = 2
C_HALF_WINDOW = 128
N_BRANCHES = 3

QKV_COLS = 7680
PROJ_W = 1536
GROUP_W = 4 * HEAD_DIM
A_OUT = A_HEADS_PER_GROUP * HEAD_DIM
B_OUT = B_Q_HEADS * HEAD_DIM
C_OUT = C_Q_HEADS * HEAD_DIM
B_K0, B_V0 = B_OUT, B_OUT + B_KV_HEADS * HEAD_DIM
C_K0, C_V0 = C_OUT, C_OUT + C_KV_HEADS * HEAD_DIM

V7X_VMEM_LIMIT_BYTES = 56 * 1024 * 1024


def _tile(n, pref):
    t = min(pref, n)
    while n % t:
        t //= 2
    return t


def _params(*sem):
    return pltpu.CompilerParams(dimension_semantics=sem, vmem_limit_bytes=V7X_VMEM_LIMIT_BYTES)


def _rms(x, g):
    ms = jnp.mean(x * x, axis=-1, keepdims=True)
    return x * lax.rsqrt(ms + NORM_EPS) * g


def _alibi_slopes(n):
    return tuple(2.0 ** (-8.0 * i / n) for i in range(1, n + 1))


def _ffn_kernel(x_ref, g_ref, wa_ref, wb_ref, wo_ref, o_ref, xn_ref):
    j = pl.program_id(1)

    @pl.when(j == 0)
    def _():
        x = x_ref[...]
        xn_ref[...] = _rms(x, g_ref[...]).astype(BF16)
        o_ref[...] = x

    xn = xn_ref[...]
    a = jnp.dot(xn, wa_ref[...], preferred_element_type=F32)
    b = jnp.dot(xn, wb_ref[...], preferred_element_type=F32)
    h = (0.5 * a * jax.nn.sigmoid(a) * b).astype(BF16)
    o_ref[...] += jnp.dot(h, wo_ref[...], preferred_element_type=F32)


def _ffn(x, g, w_in, w_out):
    n, d = x.shape
    d_ff = w_out.shape[0]
    tm = _tile(n, 512)
    tf = _tile(d_ff, 512)
    nf = d_ff // tf
    return pl.pallas_call(
        _ffn_kernel,
        grid=(n // tm, nf),
        in_specs=[
            pl.BlockSpec((tm, d), lambda i, j: (i, 0)),
            pl.BlockSpec((1, d), lambda i, j: (0, 0)),
            pl.BlockSpec((d, tf), lambda i, j: (0, j)),
            pl.BlockSpec((d, tf), lambda i, j: (0, j + nf)),
            pl.BlockSpec((tf, d), lambda i, j: (j, 0)),
        ],
        out_specs=pl.BlockSpec((tm, d), lambda i, j: (i, 0)),
        out_shape=jax.ShapeDtypeStruct((n, d), F32),
        scratch_shapes=[pltpu.VMEM((tm, d), BF16)],
        compiler_params=_params("parallel", "arbitrary"),
        name="ffn",
    )(x, g, w_in, w_in, w_out)


def _norm_rope(y, gain, cos, sin_signed):
    yn = _rms(y, gain)
    lane = lax.broadcasted_iota(jnp.int32, yn.shape, 1)
    partner = jnp.where((lane & 1) == 0, pltpu.roll(yn, HEAD_DIM - 1, 1), pltpu.roll(yn, 1, 1))
    return yn * cos + partner * sin_signed


def _qkv_kernel(x_ref, g_ref, w_ref, cos_ref, sin_ref, qg_ref, kg_ref,
                a0_ref, a1_ref, a2_ref, b_ref, c_ref, u_ref, split_ref):
    j = pl.program_id(1)
    tm = x_ref.shape[0]

    @pl.when(j == 0)
    def _():
        u_ref[...] = _rms(x_ref[...], g_ref[...]).astype(BF16)

    def proj(c0, width):
        return jnp.dot(u_ref[...], w_ref[:, c0:c0 + width], preferred_element_type=F32)

    @pl.when(j < 3)
    def _():
        sc = jnp.where(j == 0, Q_SCALE, 1.0).astype(F32)
        a0_ref[...] = (proj(0, GROUP_W) * sc).astype(BF16)
        for gi, dst_ref in ((1, a1_ref), (2, a2_ref)):
            r = A_GROUPS[gi][1]
            y = proj(gi * GROUP_W, GROUP_W) * sc
            for h in range(A_HEADS_PER_GROUP):
                split_ref[(gi - 1) * A_HEADS_PER_GROUP + h] = y[:, h * HEAD_DIM:(h + 1) * HEAD_DIM]
            for p in range(r):
                for h in range(A_HEADS_PER_GROUP):
                    rows = split_ref[(gi - 1) * A_HEADS_PER_GROUP + h, pl.ds(p, tm // r, stride=r), :]
                    dst_ref[p, :, h * HEAD_DIM:(h + 1) * HEAD_DIM] = rows.astype(BF16)

    @pl.when(j == 3)
    def _():
        pair = 2 * HEAD_DIM
        for c0 in range(0, B_V0, pair):
            acc = proj(c0, pair)
            for h0 in range(0, pair, HEAD_DIM):
                is_q = c0 + h0 < B_K0
                y = _norm_rope(acc[:, h0:h0 + HEAD_DIM], (qg_ref if is_q else kg_ref)[...], cos_ref[...],
                               sin_ref[...])
                b_ref[:, c0 + h0:c0 + h0 + HEAD_DIM] = ((y * Q_SCALE) if is_q else y).astype(BF16)
        b_ref[:, B_V0:] = proj(B_V0, PROJ_W - B_V0).astype(BF16)

    @pl.when(j == 4)
    def _():
        for c0 in range(0, C_OUT, GROUP_W):
            c_ref[:, c0:c0 + GROUP_W] = (proj(c0, GROUP_W) * Q_SCALE).astype(BF16)
        c_ref[:, C_OUT:] = proj(C_OUT, PROJ_W - C_OUT).astype(BF16)


def _qkv(x, g, w_in, cos, sin_signed, qg, kg, bsz, seq):
    n, d = x.shape
    tm = _tile(seq, 512)
    nseq = seq // tm
    r1, r2 = A_GROUPS[1][1], A_GROUPS[2][1]
    assert tm % (16 * r2) == 0
    a_part = lambda i, j: jnp.minimum(j, 2)

    def split_spec(r):
        return pl.BlockSpec((None, r, tm // r, GROUP_W), lambda i, j: (i // nseq, 0, i % nseq, a_part(i, j)))

    return pl.pallas_call(
        _qkv_kernel,
        grid=(n // tm, QKV_COLS // PROJ_W),
        in_specs=[
            pl.BlockSpec((tm, d), lambda i, j: (i, 0)),
            pl.BlockSpec((1, d), lambda i, j: (0, 0)),
            pl.BlockSpec((d, PROJ_W), lambda i, j: (0, j)),
            pl.BlockSpec((tm, HEAD_DIM), lambda i, j: (i % nseq, 0)),
            pl.BlockSpec((tm, HEAD_DIM), lambda i, j: (i % nseq, 0)),
            pl.BlockSpec((1, HEAD_DIM), lambda i, j: (0, 0)),
            pl.BlockSpec((1, HEAD_DIM), lambda i, j: (0, 0)),
        ],
        out_specs=[
            pl.BlockSpec((tm, GROUP_W), lambda i, j: (i, a_part(i, j))),
            split_spec(r1),
            split_spec(r2),
            pl.BlockSpec((tm, PROJ_W), lambda i, j: (i, 0)),
            pl.BlockSpec((tm, PROJ_W), lambda i, j: (i, 0)),
        ],
        out_shape=[
            jax.ShapeDtypeStruct((n, PROJ_W), BF16),
            jax.ShapeDtypeStruct((bsz, r1, seq // r1, PROJ_W), BF16),
            jax.ShapeDtypeStruct((bsz, r2, seq // r2, PROJ_W), BF16),
            jax.ShapeDtypeStruct((n, PROJ_W), BF16),
            jax.ShapeDtypeStruct((n, PROJ_W), BF16),
        ],
        scratch_shapes=[pltpu.VMEM((tm, d), BF16),
                        pltpu.VMEM(((PROJ_W - GROUP_W) // HEAD_DIM, tm, HEAD_DIM), F32)],
        compiler_params=_params("parallel", "arbitrary"),
        name="qkv_proj",
    )(x, g, w_in, cos, sin_signed, qg, kg)


def _band_kernel(*refs, tq, sub, hb, half_w, n_sub, slopes, dist_scale, n_kv, has_sink, emit_lse):
    if has_sink:
        sink_ref, refs = refs[0], refs[1:]
    q_ref, kp_ref, km_ref, kn_ref, vp_ref, vm_ref, vn_ref, o_ref = refs[:8]
    n_heads = len(slopes)
    group = n_heads // n_kv
    qi = pl.program_id(1)
    span = sub + 2 * hb
    row = lax.broadcasted_iota(jnp.int32, (sub, span), 0)
    col = lax.broadcasted_iota(jnp.int32, (sub, span), 1)
    arel = jnp.abs(col - hb - row)
    in_band = arel <= half_w
    dist = (arel * dist_scale).astype(F32)
    biases = [jnp.where(in_band, (-slopes[h] * LOG2_E) * dist, NEG_INF) for h in range(n_heads)]
    kcat, vcat = [], []
    for kv in range(n_kv):
        ksl = slice(kv * HEAD_DIM, (kv + 1) * HEAD_DIM)
        kcat.append(jnp.concatenate([kp_ref[:, ksl], km_ref[:, ksl], kn_ref[:, ksl]], axis=0))
        vcat.append(jnp.concatenate([vp_ref[:, ksl], vm_ref[:, ksl], vn_ref[:, ksl]], axis=0))
    if emit_lse:
        lane = lax.broadcasted_iota(jnp.int32, (sub, HEAD_DIM), 1)
    for s in range(tq // sub):
        rows = slice(s * sub, (s + 1) * sub)
        kpos = qi * tq + (s * sub - hb) + col
        in_seq = (kpos >= 0) & (kpos < n_sub)
        lse_tile = jnp.zeros((sub, HEAD_DIM), F32) if emit_lse else None
        for h in range(n_heads):
            sl = slice(h * HEAD_DIM, (h + 1) * HEAD_DIM)
            ks = kcat[h // group][s * sub:s * sub + span]
            vs = vcat[h // group][s * sub:s * sub + span]
            sc = lax.dot_general(q_ref[rows, sl], ks, (((1,), (1,)), ((), ())), preferred_element_type=F32)
            sc = jnp.where(in_seq, sc + biases[h], NEG_INF)
            m = jnp.max(sc, axis=-1, keepdims=True)
            if has_sink:
                sk = sink_ref[0, h] * LOG2_E
                m = jnp.maximum(m, sk)
            p = jnp.exp2(sc - m)
            den = jnp.sum(p, axis=-1, keepdims=True)
            if has_sink:
                den = den + jnp.exp2(sk - m)
            o = jnp.dot(p.astype(BF16), vs, preferred_element_type=F32) * (1.0 / den)
            o_ref[rows, sl] = o.astype(o_ref.dtype)
            if emit_lse:
                lse_tile = jnp.where(lane == h, m + jnp.log2(den), lse_tile)
        if emit_lse:
            refs[8][rows, :] = lse_tile


def _band_attention(arr, *, q_col, k_col, v_col, n_heads, n_kv, half_w, slopes, dist_scale, sink, emit_lse,
                    out_dtype):
    nb, n_sub, _ = arr.shape
    hb = half_w
    sub = min(HEAD_DIM, n_sub)
    tq = _tile(n_sub, 512)
    assert sub % hb == 0 and tq % sub == 0 and n_sub % tq == 0
    per = tq // hb
    n_hb = n_sub // hb
    qw, kw = n_heads * HEAD_DIM, n_kv * HEAD_DIM
    assert q_col % qw == 0 and k_col % kw == 0 and v_col % kw == 0

    def halo_specs(c):
        return [
            pl.BlockSpec((None, hb, kw), lambda b, i: (b, jnp.maximum(i * per - 1, 0), c // kw)),
            pl.BlockSpec((None, tq, kw), lambda b, i: (b, i, c // kw)),
            pl.BlockSpec((None, hb, kw), lambda b, i: (b, jnp.minimum((i + 1) * per, n_hb - 1), c // kw)),
        ]

    in_specs = [pl.BlockSpec((None, tq, qw), lambda b, i: (b, i, q_col // qw))] + halo_specs(k_col) + halo_specs(v_col)
    args = [arr] * 7
    if sink is not None:
        in_specs = [pl.BlockSpec(memory_space=pltpu.SMEM)] + in_specs
        args = [sink] + args
    out_specs = [pl.BlockSpec((None, tq, qw), lambda b, i: (b, i, 0))]
    out_shape = [jax.ShapeDtypeStruct((nb, n_sub, qw), out_dtype)]
    if emit_lse:
        out_specs.append(pl.BlockSpec((None, tq, HEAD_DIM), lambda b, i: (b, i, 0)))
        out_shape.append(jax.ShapeDtypeStruct((nb, n_sub, HEAD_DIM), F32))
    return pl.pallas_call(
        functools.partial(_band_kernel, tq=tq, sub=sub, hb=hb, half_w=half_w, n_sub=n_sub, slopes=slopes,
                          dist_scale=dist_scale, n_kv=n_kv, has_sink=sink is not None, emit_lse=emit_lse),
        grid=(nb, n_sub // tq),
        in_specs=in_specs,
        out_specs=out_specs,
        out_shape=out_shape,
        compiler_params=_params("parallel", "arbitrary"),
        name=f"band_attn_n{n_sub}_w{half_w}",
    )(*args)


def _merge_kernel(o0_ref, o1_ref, o2_ref, l0_ref, l1_ref, l2_ref, out_ref, ob1_ref, ob2_ref, lb1_ref, lb2_ref):
    tm = out_ref.shape[0]
    for src_ref, dst_ref in ((o1_ref, ob1_ref), (o2_ref, ob2_ref), (l1_ref, lb1_ref), (l2_ref, lb2_ref)):
        r = src_ref.shape[0]
        for p in range(r):
            for h in range(dst_ref.shape[0]):
                dst_ref[h, pl.ds(p, tm // r, stride=r), :] = src_ref[p, :, h * HEAD_DIM:(h + 1) * HEAD_DIM]
    l0, l1, l2 = l0_ref[...], lb1_ref[0], lb2_ref[0]
    mx = jnp.maximum(jnp.maximum(l0, l1), l2)
    w0, w1, w2 = jnp.exp2(l0 - mx), jnp.exp2(l1 - mx), jnp.exp2(l2 - mx)
    den = w0 + w1 + w2
    a0, a1, a2 = w0 / den, w1 / den, w2 / den
    for h in range(A_HEADS_PER_GROUP):
        sl = slice(h * HEAD_DIM, (h + 1) * HEAD_DIM)
        out = a0[:, h:h + 1] * o0_ref[:, sl] + a1[:, h:h + 1] * ob1_ref[h] + a2[:, h:h + 1] * ob2_ref[h]
        out_ref[:, sl] = out.astype(BF16)


def _merge_groups(outs, lses, bsz, seq):
    n = bsz * seq
    tm = _tile(seq, 512)
    nseq = seq // tm
    r1, r2 = A_GROUPS[1][1], A_GROUPS[2][1]

    def specs(w):
        return [
            pl.BlockSpec((tm, w), lambda i: (i, 0)),
            pl.BlockSpec((None, r1, tm // r1, w), lambda i: (i // nseq, 0, i % nseq, 0)),
            pl.BlockSpec((None, r2, tm // r2, w), lambda i: (i // nseq, 0, i % nseq, 0)),
        ]

    def views(xs, w):
        return [xs[0].reshape(n, w), xs[1].reshape(bsz, r1, seq // r1, w), xs[2].reshape(bsz, r2, seq // r2, w)]

    return pl.pallas_call(
        _merge_kernel,
        grid=(n // tm,),
        in_specs=specs(A_OUT) + specs(HEAD_DIM),
        out_specs=pl.BlockSpec((tm, A_OUT), lambda i: (i, 0)),
        out_shape=jax.ShapeDtypeStruct((n, A_OUT), BF16),
        scratch_shapes=[pltpu.VMEM((A_HEADS_PER_GROUP, tm, HEAD_DIM), F32),
                        pltpu.VMEM((A_HEADS_PER_GROUP, tm, HEAD_DIM), F32),
                        pltpu.VMEM((1, tm, HEAD_DIM), F32), pltpu.VMEM((1, tm, HEAD_DIM), F32)],
        compiler_params=_params("parallel"),
        name="merge_dilated",
    )(*views(outs, A_OUT), *views(lses, HEAD_DIM))


def _dense_kernel(q_ref, k_ref, v_ref, o_ref, qt_ref, st_ref, m_ref, l_ref, acc_ref, *, tq, tk, seq):
    group = B_Q_HEADS // B_KV_HEADS
    n_blk = seq // tk
    for g in range(group):
        qg = q_ref[:, g * HEAD_DIM:(g + 1) * HEAD_DIM].astype(F32)
        qt_ref[:, g * tq:(g + 1) * tq] = qg.T.astype(BF16)

    def scores(blk):
        k0 = pl.multiple_of(blk * tk, tk)
        return jnp.dot(k_ref[pl.ds(k0, tk), :], qt_ref[...], preferred_element_type=F32)

    def values(blk):
        v0 = pl.multiple_of(blk * tk, tk)
        return v_ref[pl.ds(v0, tk), :]

    def pv(v, p):
        return lax.dot_general(v, p.astype(BF16), (((0,), (0,)), ((), ())), preferred_element_type=F32)

    def write_out():
        out = acc_ref[...] / l_ref[...]
        for g in range(group):
            o_ref[:, g * HEAD_DIM:(g + 1) * HEAD_DIM] = out[:, g * tq:(g + 1) * tq].T.astype(BF16)

    first = jnp.dot(k_ref[pl.ds(0, min(tk, HEAD_DIM)), :], qt_ref[...], preferred_element_type=F32)
    m_ref[...] = jnp.max(first, axis=0, keepdims=True)
    l_ref[...] = jnp.zeros(l_ref.shape, F32)
    acc_ref[...] = jnp.zeros(acc_ref.shape, F32)

    def lazy_step(blk):
        k0 = pl.multiple_of(blk * tk, tk)
        k = k_ref[pl.ds(k0, tk), :]
        v = values(blk)
        width = LANE_CHUNK_W
        chunks = [slice(c * width, (c + 1) * width) for c in range(group * tq // width)]
        sts = [jnp.dot(k, qt_ref[:, sl], preferred_element_type=F32) for sl in chunks]
        for sl, st in zip(chunks, sts):
            p = jnp.exp2(st - m_ref[:, sl])
            l_ref[:, sl] += jnp.sum(p, axis=0, keepdims=True)
            acc_ref[:, sl] += pv(v, p)

    def lazy_body(c, carry):
        lazy_step(2 * c)
        lazy_step(2 * c + 1)
        return carry

    lax.fori_loop(0, n_blk // 2, lazy_body, 0)
    l = l_ref[...]
    sums_ok = jnp.min(jnp.where(jnp.abs(acc_ref[...]) < F32_HUGE, 1.0, 0.0))
    sums_ok = sums_ok * jnp.min(jnp.where((l > 0.0) & (l < F32_HUGE), 1.0, 0.0))
    exceeded = sums_ok < 0.5

    @pl.when(jnp.logical_not(exceeded))
    def _():
        write_out()

    @pl.when(exceeded)
    def _():
        _dense_online(scores, values, pv, write_out, st_ref, m_ref, l_ref, acc_ref, n_blk)


def _dense_online(scores, values, pv, write_out, st_ref, m_ref, l_ref, acc_ref, n_blk):
    m_ref[...] = jnp.full(m_ref.shape, -jnp.inf, F32)
    l_ref[...] = jnp.zeros(l_ref.shape, F32)
    acc_ref[...] = jnp.zeros(acc_ref.shape, F32)
    st_ref[0] = scores(0)

    def step(blk, cur, nxt):
        st_ref[nxt] = scores(jnp.minimum(blk + 1, n_blk - 1))
        st = st_ref[cur]
        m_prev = m_ref[...]
        m_new = jnp.maximum(m_prev, jnp.max(st, axis=0, keepdims=True))
        alpha = jnp.exp2(m_prev - m_new)
        p = jnp.exp2(st - m_new)
        l_ref[...] = alpha * l_ref[...] + jnp.sum(p, axis=0, keepdims=True)
        acc_ref[...] = alpha * acc_ref[...] + pv(values(blk), p)
        m_ref[...] = m_new

    def body(c, carry):
        step(2 * c, 0, 1)
        step(2 * c + 1, 1, 0)
        return carry

    lax.fori_loop(0, n_blk // 2, body, 0)
    write_out()


def _dense_attention(arr):
    bsz, seq, _ = arr.shape
    tq = _tile(seq, 512)
    tk = _tile(seq // 2, 512)
    assert (seq // tk) % 2 == 0
    group = B_Q_HEADS // B_KV_HEADS
    qw = group * HEAD_DIM
    out = pl.pallas_call(
        functools.partial(_dense_kernel, tq=tq, tk=tk, seq=seq),
        grid=(bsz, B_KV_HEADS, seq // tq),
        in_specs=[
            pl.BlockSpec((None, tq, qw), lambda b, h, i: (b, i, h)),
            pl.BlockSpec((None, seq, HEAD_DIM), lambda b, h, i: (b, 0, B_K0 // HEAD_DIM + h)),
            pl.BlockSpec((None, seq, HEAD_DIM), lambda b, h, i: (b, 0, B_V0 // HEAD_DIM + h)),
        ],
        out_specs=pl.BlockSpec((None, tq, qw), lambda b, h, i: (b, i, h)),
        out_shape=jax.ShapeDtypeStruct((bsz, seq, B_OUT), BF16),
        scratch_shapes=[
            pltpu.VMEM((HEAD_DIM, group * tq), BF16),
            pltpu.VMEM((2, tk, group * tq), F32),
            pltpu.VMEM((1, group * tq), F32),
            pltpu.VMEM((1, group * tq), F32),
            pltpu.VMEM((HEAD_DIM, group * tq), F32),
        ],
        compiler_params=_params("parallel", "parallel", "arbitrary"),
        name="dense_attn",
    )(arr, arr, arr)
    return out.reshape(bsz * seq, B_OUT)


def _out_kernel(x_ref, g_ref, wg0_ref, wg1_ref, wg2_ref, oa_ref, ob_ref, oc_ref,
                wa_ref, wb_ref, wc_ref, wo_ref, o_ref, u_ref):
    j = pl.program_id(1)

    @pl.when(j == 0)
    def _():
        x = x_ref[...]
        u_ref[...] = _rms(x, g_ref[...]).astype(BF16)
        o_ref[...] = x

    u = u_ref[...]

    def branch(wg_ref, br_ref, w_ref):
        gate = jax.nn.sigmoid(jnp.dot(u, wg_ref[...], preferred_element_type=F32))
        return gate * jnp.dot(br_ref[...], w_ref[...], preferred_element_type=F32)

    merged = branch(wg0_ref, oa_ref, wa_ref) + branch(wg1_ref, ob_ref, wb_ref) + branch(wg2_ref, oc_ref, wc_ref)
    o_ref[...] += jnp.dot(merged.astype(BF16), wo_ref[...], preferred_element_type=F32)


def _mixer_out(x, g, w_in, o_a, o_b, o_c, w_a, w_b, w_c, w_o):
    n, d = x.shape
    tm = _tile(n, 512)
    tc = _tile(d, 256)
    assert QKV_COLS % tc == 0
    g0 = QKV_COLS // tc
    gstep = d // tc

    def gate_spec(br):
        return pl.BlockSpec((d, tc), lambda i, j: (0, g0 + br * gstep + j))

    return pl.pallas_call(
        _out_kernel,
        grid=(n // tm, d // tc),
        in_specs=[
            pl.BlockSpec((tm, d), lambda i, j: (i, 0)),
            pl.BlockSpec((1, d), lambda i, j: (0, 0)),
            gate_spec(0), gate_spec(1), gate_spec(2),
            pl.BlockSpec((tm, A_OUT), lambda i, j: (i, 0)),
            pl.BlockSpec((tm, B_OUT), lambda i, j: (i, 0)),
            pl.BlockSpec((tm, C_OUT), lambda i, j: (i, 0)),
            pl.BlockSpec((A_OUT, tc), lambda i, j: (0, j)),
            pl.BlockSpec((B_OUT, tc), lambda i, j: (0, j)),
            pl.BlockSpec((C_OUT, tc), lambda i, j: (0, j)),
            pl.BlockSpec((tc, d), lambda i, j: (j, 0)),
        ],
        out_specs=pl.BlockSpec((tm, d), lambda i, j: (i, 0)),
        out_shape=jax.ShapeDtypeStruct((n, d), F32),
        scratch_shapes=[pltpu.VMEM((tm, d), BF16)],
        compiler_params=_params("parallel", "arbitrary"),
        name="mixer_out",
    )(x, g, w_in, w_in, w_in, o_a, o_b, o_c, w_a, w_b, w_c, w_o)


def _final_norm_kernel(x_ref, g_ref, o_ref):
    o_ref[...] = _rms(x_ref[...], g_ref[...])


def _final_norm(x, g):
    n, d = x.shape
    tm = _tile(n, 1024)
    return pl.pallas_call(
        _final_norm_kernel,
        grid=(n // tm,),
        in_specs=[pl.BlockSpec((tm, d), lambda i: (i, 0)), pl.BlockSpec((1, d), lambda i: (0, 0))],
        out_specs=pl.BlockSpec((tm, d), lambda i: (i, 0)),
        out_shape=jax.ShapeDtypeStruct((n, d), F32),
        compiler_params=_params("parallel"),
        name="final_norm",
    )(x, g)


def _rope_tables(seq):
    n_freq = HEAD_DIM // 4
    t = lax.broadcasted_iota(jnp.int32, (seq, HEAD_DIM), 0)
    lane = lax.broadcasted_iota(jnp.int32, (seq, HEAD_DIM), 1)
    pair = lane // 2
    pos = jnp.where(pair < n_freq, t // GRID_W, t % GRID_W).astype(F32)
    inv = ROPE_THETA ** (-(pair % n_freq).astype(F32) / n_freq)
    ang = pos * inv
    sign = jnp.where(lane % 2 == 0, -1.0, 1.0).astype(F32)
    return jnp.cos(ang), jnp.sin(ang) * sign


def _mixer(x, lw, cos, sin_signed, bsz, seq):
    a0, a1, a2, bqkv, cqkv = _qkv(x, lw["norm_mix"], lw["w_in"], cos, sin_signed, lw["qk_norm_q"],
                                  lw["qk_norm_k"], bsz, seq)
    a_slopes = _alibi_slopes(A_HEADS)
    outs, lses = [], []
    for gi, (arr, (window, r)) in enumerate(zip((a0, a1, a2), A_GROUPS)):
        hs = slice(gi * A_HEADS_PER_GROUP, (gi + 1) * A_HEADS_PER_GROUP)
        o, l = _band_attention(
            arr.reshape(bsz * r, seq // r, PROJ_W), q_col=0, k_col=GROUP_W, v_col=2 * GROUP_W,
            n_heads=A_HEADS_PER_GROUP, n_kv=A_HEADS_PER_GROUP, half_w=(window // 2) // r, slopes=a_slopes[hs],
            dist_scale=r, sink=None, emit_lse=True, out_dtype=F32)
        outs.append(o)
        lses.append(l)
    o_a = _merge_groups(outs, lses, bsz, seq)
    o_b = _dense_attention(bqkv.reshape(bsz, seq, PROJ_W))
    o_c, = _band_attention(
        cqkv.reshape(bsz, seq, PROJ_W), q_col=0, k_col=C_K0, v_col=C_V0, n_heads=C_Q_HEADS, n_kv=C_KV_HEADS,
        half_w=C_HALF_WINDOW, slopes=_alibi_slopes(C_Q_HEADS), dist_scale=1, sink=lw["sink_c"],
        emit_lse=False, out_dtype=BF16)
    return _mixer_out(x, lw["norm_mix"], lw["w_in"], o_a, o_b, o_c.reshape(bsz * seq, C_OUT), lw["w_br_a"],
                      lw["w_br_b"], lw["w_br_c"], lw["w_out"])


def _trunk(x, layers, norm_final):
    bsz, seq, d = x.shape
    cos, sin_signed = _rope_tables(seq)
    xf = x.reshape(bsz * seq, d)

    def layer(xf, lw):
        xf = _ffn(xf, lw["norm_ffn1"], lw["ffn1_w_in"], lw["ffn1_w_out"])
        xf = _mixer(xf, lw, cos, sin_signed, bsz, seq)
        xf = _ffn(xf, lw["norm_ffn2"], lw["ffn2_w_in"], lw["ffn2_w_out"])
        return xf, None

    xf, _ = lax.scan(layer, xf, layers)
    return _final_norm(xf, norm_final.reshape(1, d)).reshape(bsz, seq, d)


def kernel(x_prompt, x_sample, norm_ffn1, ffn1_w_in, ffn1_w_out, norm_mix, w_in, qk_norm_q, qk_norm_k, sink_c,
           w_br_a, w_br_b, w_br_c, w_out, norm_ffn2, ffn2_w_in, ffn2_w_out, norm_final):
    depth = norm_ffn1.shape[0]
    row = lambda v: v.reshape(depth, 1, v.shape[-1])
    layers = {
        "norm_ffn1": row(norm_ffn1), "ffn1_w_in": ffn1_w_in.astype(BF16), "ffn1_w_out": ffn1_w_out.astype(BF16),
        "norm_mix": row(norm_mix), "w_in": w_in.astype(BF16),
        "qk_norm_q": row(qk_norm_q), "qk_norm_k": row(qk_norm_k), "sink_c": row(sink_c),
        "w_br_a": w_br_a.astype(BF16), "w_br_b": w_br_b.astype(BF16), "w_br_c": w_br_c.astype(BF16),
        "w_out": w_out.astype(BF16),
        "norm_ffn2": row(norm_ffn2), "ffn2_w_in": ffn2_w_in.astype(BF16), "ffn2_w_out": ffn2_w_out.astype(BF16),
    }
    assert x_prompt.shape[1:] == x_sample.shape[1:]
    n_prompt = x_prompt.shape[0]
    y = _trunk(jnp.concatenate([x_prompt, x_sample], axis=0), layers, norm_final)
    return (y[:n_prompt], y[n_prompt:])
```

```python
import functools

import jax
import jax.numpy as jnp
from jax import lax
from jax.experimental import pallas as pl
from jax.experimental.pallas import tpu as pltpu

F32 = jnp.float32
BF16 = jnp.bfloat16

HEAD_DIM = 128
SCALE = HEAD_DIM ** -0.5
LOG2_E = 1.4426950408889634
Q_SCALE = SCALE * LOG2_E
NORM_EPS = 1e-6
NEG_INF = -1e30
F32_HUGE = 1e38
LANE_CHUNK_W = 256
ROPE_THETA = 10000.0
GRID_W = 64
A_GROUPS = ((128, 1), (512, 4), (2048, 16))
A_HEADS_PER_GROUP = 4
A_HEADS = 12
B_Q_HEADS = 8
B_KV_HEADS = 2
C_Q_HEADS = 8
C_KV_HEADS = 2
C_HALF_WINDOW = 128
N_BRANCHES = 3

QKV_COLS = 7680
PROJ_W = 1536
GROUP_W = 4 * HEAD_DIM
A_OUT = A_HEADS_PER_GROUP * HEAD_DIM
B_OUT = B_Q_HEADS * HEAD_DIM
C_OUT = C_Q_HEADS * HEAD_DIM
B_K0, B_V0 = B_OUT, B_OUT + B_KV_HEADS * HEAD_DIM
C_K0, C_V0 = C_OUT, C_OUT + C_KV_HEADS * HEAD_DIM

V7X_VMEM_LIMIT_BYTES = 56 * 1024 * 1024


def _tile(n, pref):
    t = min(pref, n)
    while n % t:
        t //= 2
    return t


def _params(*sem):
    return pltpu.CompilerParams(dimension_semantics=sem, vmem_limit_bytes=V7X_VMEM_LIMIT_BYTES)


def _rms(x, g):
    ms = jnp.mean(x * x, axis=-1, keepdims=True)
    return x * lax.rsqrt(ms + NORM_EPS) * g


def _alibi_slopes(n):
    return tuple(2.0 ** (-8.0 * i / n) for i in range(1, n + 1))


def _ffn_kernel(x_ref, g_ref, wa_ref, wb_ref, wo_ref, o_ref, xn_ref):
    j = pl.program_id(1)

    @pl.when(j == 0)
    def _():
        x = x_ref[...]
        xn_ref[...] = _rms(x, g_ref[...]).astype(BF16)
        o_ref[...] = x

    xn = xn_ref[...]
    a = jnp.dot(xn, wa_ref[...], preferred_element_type=F32)
    b = jnp.dot(xn, wb_ref[...], preferred_element_type=F32)
    h = (0.5 * a * jax.nn.sigmoid(a) * b).astype(BF16)
    o_ref[...] += jnp.dot(h, wo_ref[...], preferred_element_type=F32)


def _ffn(x, g, w_in, w_out):
    n, d = x.shape
    d_ff = w_out.shape[0]
    tm = _tile(n, 1024)
    tf = _tile(d_ff, 512)
    nf = d_ff // tf
    return pl.pallas_call(
        _ffn_kernel,
        grid=(n // tm, nf),
        in_specs=[
            pl.BlockSpec((tm, d), lambda i, j: (i, 0)),
            pl.BlockSpec((1, d), lambda i, j: (0, 0)),
            pl.BlockSpec((d, tf), lambda i, j: (0, j)),
            pl.BlockSpec((d, tf), lambda i, j: (0, j + nf)),
            pl.BlockSpec((tf, d), lambda i, j: (j, 0)),
        ],
        out_specs=pl.BlockSpec((tm, d), lambda i, j: (i, 0)),
        out_shape=jax.ShapeDtypeStruct((n, d), F32),
        scratch_shapes=[pltpu.VMEM((tm, d), BF16)],
        compiler_params=_params("parallel", "arbitrary"),
        name="ffn",
    )(x, g, w_in, w_in, w_out)


def _norm_rope(y, gain, cos, sin_signed):
    yn = _rms(y, gain)
    lane = lax.broadcasted_iota(jnp.int32, yn.shape, 1)
    partner = jnp.where((lane & 1) == 0, pltpu.roll(yn, HEAD_DIM - 1, 1), pltpu.roll(yn, 1, 1))
    return yn * cos + partner * sin_signed


def _qkv_kernel(x_ref, g_ref, w_ref, cos_ref, sin_ref, qg_ref, kg_ref,
                a0_ref, a1_ref, a2_ref, b_ref, c_ref, u_ref, split_ref):
    j = pl.program_id(1)
    tm = x_ref.shape[0]

    @pl.when(j == 0)
    def _():
        u_ref[...] = _rms(x_ref[...], g_ref[...]).astype(BF16)

    def proj(c0, width):
        return jnp.dot(u_ref[...], w_ref[:, c0:c0 + width], preferred_element_type=F32)

    @pl.when(j < 3)
    def _():
        sc = jnp.where(j == 0, Q_SCALE, 1.0).astype(F32)
        a0_ref[...] = (proj(0, GROUP_W) * sc).astype(BF16)
        for gi, dst_ref in ((1, a1_ref), (2, a2_ref)):
            r = A_GROUPS[gi][1]
            y = proj(gi * GROUP_W, GROUP_W) * sc
            for h in range(A_HEADS_PER_GROUP):
                split_ref[(gi - 1) * A_HEADS_PER_GROUP + h] = y[:, h * HEAD_DIM:(h + 1) * HEAD_DIM]
            for p in range(r):
                for h in range(A_HEADS_PER_GROUP):
                    rows = split_ref[(gi - 1) * A_HEADS_PER_GROUP + h, pl.ds(p, tm // r, stride=r), :]
                    dst_ref[p, :, h * HEAD_DIM:(h + 1) * HEAD_DIM] = rows.astype(BF16)

    @pl.when(j == 3)
    def _():
        pair = 2 * HEAD_DIM
        for c0 in range(0, B_V0, pair):
            acc = proj(c0, pair)
            for h0 in range(0, pair, HEAD_DIM):
                is_q = c0 + h0 < B_K0
                y = _norm_rope(acc[:, h0:h0 + HEAD_DIM], (qg_ref if is_q else kg_ref)[...], cos_ref[...],
                               sin_ref[...])
                b_ref[:, c0 + h0:c0 + h0 + HEAD_DIM] = ((y * Q_SCALE) if is_q else y).astype(BF16)
        b_ref[:, B_V0:] = proj(B_V0, PROJ_W - B_V0).astype(BF16)

    @pl.when(j == 4)
    def _():
        for c0 in range(0, C_OUT, GROUP_W):
            c_ref[:, c0:c0 + GROUP_W] = (proj(c0, GROUP_W) * Q_SCALE).astype(BF16)
        c_ref[:, C_OUT:] = proj(C_OUT, PROJ_W - C_OUT).astype(BF16)


def _qkv(x, g, w_in, cos, sin_signed, qg, kg, bsz, seq):
    n, d = x.shape
    tm = _tile(seq, 512)
    nseq = seq // tm
    r1, r2 = A_GROUPS[1][1], A_GROUPS[2][1]
    assert tm % (16 * r2) == 0
    a_part = lambda i, j: jnp.minimum(j, 2)

    def split_spec(r):
        return pl.BlockSpec((None, r, tm // r, GROUP_W), lambda i, j: (i // nseq, 0, i % nseq, a_part(i, j)))

    return pl.pallas_call(
        _qkv_kernel,
        grid=(n // tm, QKV_COLS // PROJ_W),
        in_specs=[
            pl.BlockSpec((tm, d), lambda i, j: (i, 0)),
            pl.BlockSpec((1, d), lambda i, j: (0, 0)),
            pl.BlockSpec((d, PROJ_W), lambda i, j: (0, j)),
            pl.BlockSpec((tm, HEAD_DIM), lambda i, j: (i % nseq, 0)),
            pl.BlockSpec((tm, HEAD_DIM), lambda i, j: (i % nseq, 0)),
            pl.BlockSpec((1, HEAD_DIM), lambda i, j: (0, 0)),
            pl.BlockSpec((1, HEAD_DIM), lambda i, j: (0, 0)),
        ],
        out_specs=[
            pl.BlockSpec((tm, GROUP_W), lambda i, j: (i, a_part(i, j))),
            split_spec(r1),
            split_spec(r2),
            pl.BlockSpec((tm, PROJ_W), lambda i, j: (i, 0)),
            pl.BlockSpec((tm, PROJ_W), lambda i, j: (i, 0)),
        ],
        out_shape=[
            jax.ShapeDtypeStruct((n, PROJ_W), BF16),
            jax.ShapeDtypeStruct((bsz, r1, seq // r1, PROJ_W), BF16),
            jax.ShapeDtypeStruct((bsz, r2, seq // r2, PROJ_W), BF16),
            jax.ShapeDtypeStruct((n, PROJ_W), BF16),
            jax.ShapeDtypeStruct((n, PROJ_W), BF16),
        ],
        scratch_shapes=[pltpu.VMEM((tm, d), BF16),
                        pltpu.VMEM(((PROJ_W - GROUP_W) // HEAD_DIM, tm, HEAD_DIM), F32)],
        compiler_params=_params("parallel", "arbitrary"),
        name="qkv_proj",
    )(x, g, w_in, cos, sin_signed, qg, kg)


def _band_kernel(*refs, tq, sub, hb, half_w, n_sub, slopes, dist_scale, n_kv, has_sink, emit_lse):
    if has_sink:
        sink_ref, refs = refs[0], refs[1:]
    q_ref, kp_ref, km_ref, kn_ref, vp_ref, vm_ref, vn_ref, o_ref = refs[:8]
    n_heads = len(slopes)
    group = n_heads // n_kv
    qi = pl.program_id(1)
    span = sub + 2 * hb
    row = lax.broadcasted_iota(jnp.int32, (sub, span), 0)
    col = lax.broadcasted_iota(jnp.int32, (sub, span), 1)
    arel = jnp.abs(col - hb - row)
    in_band = arel <= half_w
    dist = (arel * dist_scale).astype(F32)
    biases = [jnp.where(in_band, (-slopes[h] * LOG2_E) * dist, NEG_INF) for h in range(n_heads)]
    kcat, vcat = [], []
    for kv in range(n_kv):
        ksl = slice(kv * HEAD_DIM, (kv + 1) * HEAD_DIM)
        kcat.append(jnp.concatenate([kp_ref[:, ksl], km_ref[:, ksl], kn_ref[:, ksl]], axis=0))
        vcat.append(jnp.concatenate([vp_ref[:, ksl], vm_ref[:, ksl], vn_ref[:, ksl]], axis=0))
    if emit_lse:
        lane = lax.broadcasted_iota(jnp.int32, (sub, HEAD_DIM), 1)
    for s in range(tq // sub):
        rows = slice(s * sub, (s + 1) * sub)
        kpos = qi * tq + (s * sub - hb) + col
        in_seq = (kpos >= 0) & (kpos < n_sub)
        lse_tile = jnp.zeros((sub, HEAD_DIM), F32) if emit_lse else None
        for h in range(n_heads):
            sl = slice(h * HEAD_DIM, (h + 1) * HEAD_DIM)
            ks = kcat[h // group][s * sub:s * sub + span]
            vs = vcat[h // group][s * sub:s * sub + span]
            sc = lax.dot_general(q_ref[rows, sl], ks, (((1,), (1,)), ((), ())), preferred_element_type=F32)
            sc = jnp.where(in_seq, sc + biases[h], NEG_INF)
            m = jnp.max(sc, axis=-1, keepdims=True)
            if has_sink:
                sk = sink_ref[0, h] * LOG2_E
                m = jnp.maximum(m, sk)
            p = jnp.exp2(sc - m)
            den = jnp.sum(p, axis=-1, keepdims=True)
            if has_sink:
                den = den + jnp.exp2(sk - m)
            o = jnp.dot(p.astype(BF16), vs, preferred_element_type=F32) * (1.0 / den)
            o_ref[rows, sl] = o.astype(o_ref.dtype)
            if emit_lse:
                lse_tile = jnp.where(lane == h, m + jnp.log2(den), lse_tile)
        if emit_lse:
            refs[8][rows, :] = lse_tile


def _band_attention(arr, *, q_col, k_col, v_col, n_heads, n_kv, half_w, slopes, dist_scale, sink, emit_lse,
                    out_dtype):
    nb, n_sub, _ = arr.shape
    hb = half_w
    sub = min(HEAD_DIM, n_sub)
    tq = _tile(n_sub, 512)
    assert sub % hb == 0 and tq % sub == 0 and n_sub % tq == 0
    per = tq // hb
    n_hb = n_sub // hb
    qw, kw = n_heads * HEAD_DIM, n_kv * HEAD_DIM
    assert q_col % qw == 0 and k_col % kw == 0 and v_col % kw == 0

    def halo_specs(c):
        return [
            pl.BlockSpec((None, hb, kw), lambda b, i: (b, jnp.maximum(i * per - 1, 0), c // kw)),
            pl.BlockSpec((None, tq, kw), lambda b, i: (b, i, c // kw)),
            pl.BlockSpec((None, hb, kw), lambda b, i: (b, jnp.minimum((i + 1) * per, n_hb - 1), c // kw)),
        ]

    in_specs = [pl.BlockSpec((None, tq, qw), lambda b, i: (b, i, q_col // qw))] + halo_specs(k_col) + halo_specs(v_col)
    args = [arr] * 7
    if sink is not None:
        in_specs = [pl.BlockSpec(memory_space=pltpu.SMEM)] + in_specs
        args = [sink] + args
    out_specs = [pl.BlockSpec((None, tq, qw), lambda b, i: (b, i, 0))]
    out_shape = [jax.ShapeDtypeStruct((nb, n_sub, qw), out_dtype)]
    if emit_lse:
        out_specs.append(pl.BlockSpec((None, tq, HEAD_DIM), lambda b, i: (b, i, 0)))
        out_shape.append(jax.ShapeDtypeStruct((nb, n_sub, HEAD_DIM), F32))
    return pl.pallas_call(
        functools.partial(_band_kernel, tq=tq, sub=sub, hb=hb, half_w=half_w, n_sub=n_sub, slopes=slopes,
                          dist_scale=dist_scale, n_kv=n_kv, has_sink=sink is not None, emit_lse=emit_lse),
        grid=(nb, n_sub // tq),
        in_specs=in_specs,
        out_specs=out_specs,
        out_shape=out_shape,
        compiler_params=_params("parallel", "arbitrary"),
        name=f"band_attn_n{n_sub}_w{half_w}",
    )(*args)


def _merge_kernel(o0_ref, o1_ref, o2_ref, l0_ref, l1_ref, l2_ref, out_ref, ob1_ref, ob2_ref, lb1_ref, lb2_ref):
    tm = out_ref.shape[0]
    for src_ref, dst_ref in ((o1_ref, ob1_ref), (o2_ref, ob2_ref), (l1_ref, lb1_ref), (l2_ref, lb2_ref)):
        r = src_ref.shape[0]
        for p in range(r):
            for h in range(dst_ref.shape[0]):
                dst_ref[h, pl.ds(p, tm // r, stride=r), :] = src_ref[p, :, h * HEAD_DIM:(h + 1) * HEAD_DIM]
    l0, l1, l2 = l0_ref[...], lb1_ref[0], lb2_ref[0]
    mx = jnp.maximum(jnp.maximum(l0, l1), l2)
    w0, w1, w2 = jnp.exp2(l0 - mx), jnp.exp2(l1 - mx), jnp.exp2(l2 - mx)
    den = w0 + w1 + w2
    a0, a1, a2 = w0 / den, w1 / den, w2 / den
    for h in range(A_HEADS_PER_GROUP):
        sl = slice(h * HEAD_DIM, (h + 1) * HEAD_DIM)
        out = a0[:, h:h + 1] * o0_ref[:, sl] + a1[:, h:h + 1] * ob1_ref[h] + a2[:, h:h + 1] * ob2_ref[h]
        out_ref[:, sl] = out.astype(BF16)


def _merge_groups(outs, lses, bsz, seq):
    n = bsz * seq
    tm = _tile(seq, 512)
    nseq = seq // tm
    r1, r2 = A_GROUPS[1][1], A_GROUPS[2][1]

    def specs(w):
        return [
            pl.BlockSpec((tm, w), lambda i: (i, 0)),
            pl.BlockSpec((None, r1, tm // r1, w), lambda i: (i // nseq, 0, i % nseq, 0)),
            pl.BlockSpec((None, r2, tm // r2, w), lambda i: (i // nseq, 0, i % nseq, 0)),
        ]

    def views(xs, w):
        return [xs[0].reshape(n, w), xs[1].reshape(bsz, r1, seq // r1, w), xs[2].reshape(bsz, r2, seq // r2, w)]

    return pl.pallas_call(
        _merge_kernel,
        grid=(n // tm,),
        in_specs=specs(A_OUT) + specs(HEAD_DIM),
        out_specs=pl.BlockSpec((tm, A_OUT), lambda i: (i, 0)),
        out_shape=jax.ShapeDtypeStruct((n, A_OUT), BF16),
        scratch_shapes=[pltpu.VMEM((A_HEADS_PER_GROUP, tm, HEAD_DIM), F32),
                        pltpu.VMEM((A_HEADS_PER_GROUP, tm, HEAD_DIM), F32),
                        pltpu.VMEM((1, tm, HEAD_DIM), F32), pltpu.VMEM((1, tm, HEAD_DIM), F32)],
        compiler_params=_params("parallel"),
        name="merge_dilated",
    )(*views(outs, A_OUT), *views(lses, HEAD_DIM))


def _dense_kernel(q_ref, k_ref, v_ref, o_ref, qt_ref, st_ref, m_ref, l_ref, acc_ref, *, tq, tk, seq):
    group = B_Q_HEADS // B_KV_HEADS
    n_blk = seq // tk
    for g in range(group):
        qg = q_ref[:, g * HEAD_DIM:(g + 1) * HEAD_DIM].astype(F32)
        qt_ref[:, g * tq:(g + 1) * tq] = qg.T.astype(BF16)

    def scores(blk):
        k0 = pl.multiple_of(blk * tk, tk)
        return jnp.dot(k_ref[pl.ds(k0, tk), :], qt_ref[...], preferred_element_type=F32)

    def values(blk):
        v0 = pl.multiple_of(blk * tk, tk)
        return v_ref[pl.ds(v0, tk), :]

    def pv(v, p):
        return lax.dot_general(v, p.astype(BF16), (((0,), (0,)), ((), ())), preferred_element_type=F32)

    def write_out():
        out = acc_ref[...] / l_ref[...]
        for g in range(group):
            o_ref[:, g * HEAD_DIM:(g + 1) * HEAD_DIM] = out[:, g * tq:(g + 1) * tq].T.astype(BF16)

    first = jnp.dot(k_ref[pl.ds(0, min(tk, HEAD_DIM)), :], qt_ref[...], preferred_element_type=F32)
    m_ref[...] = jnp.max(first, axis=0, keepdims=True)
    l_ref[...] = jnp.zeros(l_ref.shape, F32)
    acc_ref[...] = jnp.zeros(acc_ref.shape, F32)

    def lazy_step(blk):
        k0 = pl.multiple_of(blk * tk, tk)
        k = k_ref[pl.ds(k0, tk), :]
        v = values(blk)
        width = LANE_CHUNK_W
        chunks = [slice(c * width, (c + 1) * width) for c in range(group * tq // width)]
        sts = [jnp.dot(k, qt_ref[:, sl], preferred_element_type=F32) for sl in chunks]
        for sl, st in zip(chunks, sts):
            p = jnp.exp2(st - m_ref[:, sl])
            l_ref[:, sl] += jnp.sum(p, axis=0, keepdims=True)
            acc_ref[:, sl] += pv(v, p)

    def lazy_body(c, carry):
        lazy_step(2 * c)
        lazy_step(2 * c + 1)
        return carry

    lax.fori_loop(0, n_blk // 2, lazy_body, 0)
    l = l_ref[...]
    sums_ok = jnp.min(jnp.where(jnp.abs(acc_ref[...]) < F32_HUGE, 1.0, 0.0))
    sums_ok = sums_ok * jnp.min(jnp.where((l > 0.0) & (l < F32_HUGE), 1.0, 0.0))
    exceeded = sums_ok < 0.5

    @pl.when(jnp.logical_not(exceeded))
    def _():
        write_out()

    @pl.when(exceeded)
    def _():
        _dense_online(scores, values, pv, write_out, st_ref, m_ref, l_ref, acc_ref, n_blk)


def _dense_online(scores, values, pv, write_out, st_ref, m_ref, l_ref, acc_ref, n_blk):
    m_ref[...] = jnp.full(m_ref.shape, -jnp.inf, F32)
    l_ref[...] = jnp.zeros(l_ref.shape, F32)
    acc_ref[...] = jnp.zeros(acc_ref.shape, F32)
    st_ref[0] = scores(0)

    def step(blk, cur, nxt):
        st_ref[nxt] = scores(jnp.minimum(blk + 1, n_blk - 1))
        st = st_ref[cur]
        m_prev = m_ref[...]
        m_new = jnp.maximum(m_prev, jnp.max(st, axis=0, keepdims=True))
        alpha = jnp.exp2(m_prev - m_new)
        p = jnp.exp2(st - m_new)
        l_ref[...] = alpha * l_ref[...] + jnp.sum(p, axis=0, keepdims=True)
        acc_ref[...] = alpha * acc_ref[...] + pv(values(blk), p)
        m_ref[...] = m_new

    def body(c, carry):
        step(2 * c, 0, 1)
        step(2 * c + 1, 1, 0)
        return carry

    lax.fori_loop(0, n_blk // 2, body, 0)
    write_out()


def _dense_attention(arr):
    bsz, seq, _ = arr.shape
    tq = _tile(seq, 1024)
    tk = _tile(seq // 2, 512)
    assert (seq // tk) % 2 == 0
    group = B_Q_HEADS // B_KV_HEADS
    qw = group * HEAD_DIM
    out = pl.pallas_call(
        functools.partial(_dense_kernel, tq=tq, tk=tk, seq=seq),
        grid=(bsz, B_KV_HEADS, seq // tq),
        in_specs=[
            pl.BlockSpec((None, tq, qw), lambda b, h, i: (b, i, h)),
            pl.BlockSpec((None, seq, HEAD_DIM), lambda b, h, i: (b, 0, B_K0 // HEAD_DIM + h)),
            pl.BlockSpec((None, seq, HEAD_DIM), lambda b, h, i: (b, 0, B_V0 // HEAD_DIM + h)),
        ],
        out_specs=pl.BlockSpec((None, tq, qw), lambda b, h, i: (b, i, h)),
        out_shape=jax.ShapeDtypeStruct((bsz, seq, B_OUT), BF16),
        scratch_shapes=[
            pltpu.VMEM((HEAD_DIM, group * tq), BF16),
            pltpu.VMEM((2, tk, group * tq), F32),
            pltpu.VMEM((1, group * tq), F32),
            pltpu.VMEM((1, group * tq), F32),
            pltpu.VMEM((HEAD_DIM, group * tq), F32),
        ],
        compiler_params=_params("parallel", "parallel", "arbitrary"),
        name="dense_attn",
    )(arr, arr, arr)
    return out.reshape(bsz * seq, B_OUT)


def _out_kernel(x_ref, g_ref, wg0_ref, wg1_ref, wg2_ref, oa_ref, ob_ref, oc_ref,
                wa_ref, wb_ref, wc_ref, wo_ref, o_ref, u_ref):
    j = pl.program_id(1)

    @pl.when(j == 0)
    def _():
        x = x_ref[...]
        u_ref[...] = _rms(x, g_ref[...]).astype(BF16)
        o_ref[...] = x

    u = u_ref[...]

    def branch(wg_ref, br_ref, w_ref):
        gate = jax.nn.sigmoid(jnp.dot(u, wg_ref[...], preferred_element_type=F32))
        return gate * jnp.dot(br_ref[...], w_ref[...], preferred_element_type=F32)

    merged = branch(wg0_ref, oa_ref, wa_ref) + branch(wg1_ref, ob_ref, wb_ref) + branch(wg2_ref, oc_ref, wc_ref)
    o_ref[...] += jnp.dot(merged.astype(BF16), wo_ref[...], preferred_element_type=F32)


def _mixer_out(x, g, w_in, o_a, o_b, o_c, w_a, w_b, w_c, w_o):
    n, d = x.shape
    tm = _tile(n, 512)
    tc = _tile(d, 512)
    assert QKV_COLS % tc == 0
    g0 = QKV_COLS // tc
    gstep = d // tc

    def gate_spec(br):
        return pl.BlockSpec((d, tc), lambda i, j: (0, g0 + br * gstep + j))

    return pl.pallas_call(
        _out_kernel,
        grid=(n // tm, d // tc),
        in_specs=[
            pl.BlockSpec((tm, d), lambda i, j: (i, 0)),
            pl.BlockSpec((1, d), lambda i, j: (0, 0)),
            gate_spec(0), gate_spec(1), gate_spec(2),
            pl.BlockSpec((tm, A_OUT), lambda i, j: (i, 0)),
            pl.BlockSpec((tm, B_OUT), lambda i, j: (i, 0)),
            pl.BlockSpec((tm, C_OUT), lambda i, j: (i, 0)),
            pl.BlockSpec((A_OUT, tc), lambda i, j: (0, j)),
            pl.BlockSpec((B_OUT, tc), lambda i, j: (0, j)),
            pl.BlockSpec((C_OUT, tc), lambda i, j: (0, j)),
            pl.BlockSpec((tc, d), lambda i, j: (j, 0)),
        ],
        out_specs=pl.BlockSpec((tm, d), lambda i, j: (i, 0)),
        out_shape=jax.ShapeDtypeStruct((n, d), F32),
        scratch_shapes=[pltpu.VMEM((tm, d), BF16)],
        compiler_params=_params("parallel", "arbitrary"),
        name="mixer_out",
    )(x, g, w_in, w_in, w_in, o_a, o_b, o_c, w_a, w_b, w_c, w_o)


def _final_norm_kernel(x_ref, g_ref, o_ref):
    o_ref[...] = _rms(x_ref[...], g_ref[...])


def _final_norm(x, g):
    n, d = x.shape
    tm = _tile(n, 1024)
    return pl.pallas_call(
        _final_norm_kernel,
        grid=(n // tm,),
        in_specs=[pl.BlockSpec((tm, d), lambda i: (i, 0)), pl.BlockSpec((1, d), lambda i: (0, 0))],
        out_specs=pl.BlockSpec((tm, d), lambda i: (i, 0)),
        out_shape=jax.ShapeDtypeStruct((n, d), F32),
        compiler_params=_params("parallel"),
        name="final_norm",
    )(x, g)


def _rope_tables(seq):
    n_freq = HEAD_DIM // 4
    t = lax.broadcasted_iota(jnp.int32, (seq, HEAD_DIM), 0)
    lane = lax.broadcasted_iota(jnp.int32, (seq, HEAD_DIM), 1)
    pair = lane // 2
    pos = jnp.where(pair < n_freq, t // GRID_W, t % GRID_W).astype(F32)
    inv = ROPE_THETA ** (-(pair % n_freq).astype(F32) / n_freq)
    ang = pos * inv
    sign = jnp.where(lane % 2 == 0, -1.0, 1.0).astype(F32)
    return jnp.cos(ang), jnp.sin(ang) * sign


def _mixer(x, lw, cos, sin_signed, bsz, seq):
    a0, a1, a2, bqkv, cqkv = _qkv(x, lw["norm_mix"], lw["w_in"], cos, sin_signed, lw["qk_norm_q"],
                                  lw["qk_norm_k"], bsz, seq)
    a_slopes = _alibi_slopes(A_HEADS)
    outs, lses = [], []
    for gi, (arr, (window, r)) in enumerate(zip((a0, a1, a2), A_GROUPS)):
        hs = slice(gi * A_HEADS_PER_GROUP, (gi + 1) * A_HEADS_PER_GROUP)
        o, l = _band_attention(
            arr.reshape(bsz * r, seq // r, PROJ_W), q_col=0, k_col=GROUP_W, v_col=2 * GROUP_W,
            n_heads=A_HEADS_PER_GROUP, n_kv=A_HEADS_PER_GROUP, half_w=(window // 2) // r, slopes=a_slopes[hs],
            dist_scale=r, sink=None, emit_lse=True, out_dtype=F32)
        outs.append(o)
        lses.append(l)
    o_a = _merge_groups(outs, lses, bsz, seq)
    o_b = _dense_attention(bqkv.reshape(bsz, seq, PROJ_W))
    o_c, = _band_attention(
        cqkv.reshape(bsz, seq, PROJ_W), q_col=0, k_col=C_K0, v_col=C_V0, n_heads=C_Q_HEADS, n_kv=C_KV_HEADS,
        half_w=C_HALF_WINDOW, slopes=_alibi_slopes(C_Q_HEADS), dist_scale=1, sink=lw["sink_c"],
        emit_lse=False, out_dtype=BF16)
    return _mixer_out(x, lw["norm_mix"], lw["w_in"], o_a, o_b, o_c.reshape(bsz * seq, C_OUT), lw["w_br_a"],
                      lw["w_br_b"], lw["w_br_c"], lw["w_out"])


def _trunk(x, layers, norm_final):
    bsz, seq, d = x.shape
    cos, sin_signed = _rope_tables(seq)
    xf = x.reshape(bsz * seq, d)

    def layer(xf, lw):
        xf = _ffn(xf, lw["norm_ffn1"], lw["ffn1_w_in"], lw["ffn1_w_out"])
        xf = _mixer(xf, lw, cos, sin_signed, bsz, seq)
        xf = _ffn(xf, lw["norm_ffn2"], lw["ffn2_w_in"], lw["ffn2_w_out"])
        return xf, None

    xf, _ = lax.scan(layer, xf, layers)
    return _final_norm(xf, norm_final.reshape(1, d)).reshape(bsz, seq, d)


def kernel(x_prompt, x_sample, norm_ffn1, ffn1_w_in, ffn1_w_out, norm_mix, w_in, qk_norm_q, qk_norm_k, sink_c,
           w_br_a, w_br_b, w_br_c, w_out, norm_ffn2, ffn2_w_in, ffn2_w_out, norm_final):
    depth = norm_ffn1.shape[0]
    row = lambda v: v.reshape(depth, 1, v.shape[-1])
    layers = {
        "norm_ffn1": row(norm_ffn1), "ffn1_w_in": ffn1_w_in.astype(BF16), "ffn1_w_out": ffn1_w_out.astype(BF16),
        "norm_mix": row(norm_mix), "w_in": w_in.astype(BF16),
        "qk_norm_q": row(qk_norm_q), "qk_norm_k": row(qk_norm_k), "sink_c": row(sink_c),
        "w_br_a": w_br_a.astype(BF16), "w_br_b": w_br_b.astype(BF16), "w_br_c": w_br_c.astype(BF16),
        "w_out": w_out.astype(BF16),
        "norm_ffn2": row(norm_ffn2), "ffn2_w_in": ffn2_w_in.astype(BF16), "ffn2_w_out": ffn2_w_out.astype(BF16),
    }
    assert x_prompt.shape[1:] == x_sample.shape[1:]
    n_prompt = x_prompt.shape[0]
    y = _trunk(jnp.concatenate([x_prompt, x_sample], axis=0), layers, norm_final)
    return (y[:n_prompt], y[n_prompt:])
```

```python
import functools

import jax
import jax.numpy as jnp
from jax import lax
from jax.experimental import pallas as pl
from jax.experimental.pallas import tpu as pltpu

F32 = jnp.float32
BF16 = jnp.bfloat16

HEAD_DIM = 128
SCALE = HEAD_DIM ** -0.5
LOG2_E = 1.4426950408889634
Q_SCALE = SCALE * LOG2_E
NORM_EPS = 1e-6
NEG_INF = -1e30
F32_HUGE = 1e38
LANE_CHUNK_W = 256
ROPE_THETA = 10000.0
GRID_W = 64
A_GROUPS = ((128, 1), (512, 4), (2048, 16))
A_HEADS_PER_GROUP = 4
A_HEADS = 12
B_Q_HEADS = 8
B_KV_HEADS = 2
C_Q_HEADS = 8
C_KV_HEADS = 2
C_HALF_WINDOW = 128
N_BRANCHES = 3

QKV_COLS = 7680
PROJ_W = 1536
GROUP_W = 4 * HEAD_DIM
A_OUT = A_HEADS_PER_GROUP * HEAD_DIM
B_OUT = B_Q_HEADS * HEAD_DIM
C_OUT = C_Q_HEADS * HEAD_DIM
B_K0, B_V0 = B_OUT, B_OUT + B_KV_HEADS * HEAD_DIM
C_K0, C_V0 = C_OUT, C_OUT + C_KV_HEADS * HEAD_DIM

V7X_VMEM_LIMIT_BYTES = 56 * 1024 * 1024


def _tile(n, pref):
    t = min(pref, n)
    while n % t:
        t //= 2
    return t


def _params(*sem):
    return pltpu.CompilerParams(dimension_semantics=sem, vmem_limit_bytes=V7X_VMEM_LIMIT_BYTES)


def _rms(x, g):
    ms = jnp.mean(x * x, axis=-1, keepdims=True)
    return x * lax.rsqrt(ms + NORM_EPS) * g


def _alibi_slopes(n):
    return tuple(2.0 ** (-8.0 * i / n) for i in range(1, n + 1))


def _ffn_kernel(x_ref, g_ref, wa_ref, wb_ref, wo_ref, *rest, final_norm):
    o_ref, xn_ref = rest[-2:]
    j = pl.program_id(1)

    @pl.when(j == 0)
    def _():
        x = x_ref[...]
        xn_ref[...] = _rms(x, g_ref[...]).astype(BF16)
        o_ref[...] = x

    xn = xn_ref[...]
    a = jnp.dot(xn, wa_ref[...], preferred_element_type=F32)
    b = jnp.dot(xn, wb_ref[...], preferred_element_type=F32)
    h = (0.5 * a * jax.nn.sigmoid(a) * b).astype(BF16)
    o_ref[...] += jnp.dot(h, wo_ref[...], preferred_element_type=F32)

    if final_norm:
        @pl.when(j == pl.num_programs(1) - 1)
        def _():
            o_ref[...] = _rms(o_ref[...], rest[0][...])


def _ffn(x, g, w_in, w_out, layer, final_gain=None):
    n, d = x.shape
    d_ff = w_out.shape[1]
    tm = _tile(n, 1024)
    tf = _tile(d_ff, 512)
    nf = d_ff // tf
    in_specs = [
        pl.BlockSpec((tm, d), lambda i, j: (i, 0)),
        pl.BlockSpec((None, 1, d), lambda i, j: (layer, 0, 0)),
        pl.BlockSpec((None, d, tf), lambda i, j: (layer, 0, j)),
        pl.BlockSpec((None, d, tf), lambda i, j: (layer, 0, j + nf)),
        pl.BlockSpec((None, tf, d), lambda i, j: (layer, j, 0)),
    ]
    args = [x, g, w_in, w_in, w_out]
    if final_gain is not None:
        in_specs.append(pl.BlockSpec((1, d), lambda i, j: (0, 0)))
        args.append(final_gain)
    return pl.pallas_call(
        functools.partial(_ffn_kernel, final_norm=final_gain is not None),
        grid=(n // tm, nf),
        in_specs=in_specs,
        out_specs=pl.BlockSpec((tm, d), lambda i, j: (i, 0)),
        out_shape=jax.ShapeDtypeStruct((n, d), F32),
        scratch_shapes=[pltpu.VMEM((tm, d), BF16)],
        compiler_params=_params("parallel", "arbitrary"),
        name="ffn",
    )(*args)


def _norm_rope(y, gain, cos, sin_signed):
    yn = _rms(y, gain)
    lane = lax.broadcasted_iota(jnp.int32, yn.shape, 1)
    partner = jnp.where((lane & 1) == 0, pltpu.roll(yn, HEAD_DIM - 1, 1), pltpu.roll(yn, 1, 1))
    return yn * cos + partner * sin_signed


def _qkv_kernel(x_ref, g_ref, w_ref, cos_ref, sin_ref, qg_ref, kg_ref,
                a0_ref, a1_ref, a2_ref, b_ref, c_ref, u_ref, split_ref):
    j = pl.program_id(1)
    tm = x_ref.shape[0]

    @pl.when(j == 0)
    def _():
        u_ref[...] = _rms(x_ref[...], g_ref[...]).astype(BF16)

    def proj(c0, width):
        return jnp.dot(u_ref[...], w_ref[:, c0:c0 + width], preferred_element_type=F32)

    @pl.when(j < 3)
    def _():
        sc = jnp.where(j == 0, Q_SCALE, 1.0).astype(F32)
        a0_ref[...] = (proj(0, GROUP_W) * sc).astype(BF16)
        for gi, dst_ref in ((1, a1_ref), (2, a2_ref)):
            r = A_GROUPS[gi][1]
            y = proj(gi * GROUP_W, GROUP_W) * sc
            for h in range(A_HEADS_PER_GROUP):
                split_ref[(gi - 1) * A_HEADS_PER_GROUP + h] = y[:, h * HEAD_DIM:(h + 1) * HEAD_DIM]
            for p in range(r):
                for h in range(A_HEADS_PER_GROUP):
                    rows = split_ref[(gi - 1) * A_HEADS_PER_GROUP + h, pl.ds(p, tm // r, stride=r), :]
                    dst_ref[p, :, h * HEAD_DIM:(h + 1) * HEAD_DIM] = rows.astype(BF16)

    @pl.when(j == 3)
    def _():
        pair = 2 * HEAD_DIM
        for c0 in range(0, B_V0, pair):
            acc = proj(c0, pair)
            for h0 in range(0, pair, HEAD_DIM):
                is_q = c0 + h0 < B_K0
                y = _norm_rope(acc[:, h0:h0 + HEAD_DIM], (qg_ref if is_q else kg_ref)[...], cos_ref[...],
                               sin_ref[...])
                b_ref[:, c0 + h0:c0 + h0 + HEAD_DIM] = ((y * Q_SCALE) if is_q else y).astype(BF16)
        b_ref[:, B_V0:] = proj(B_V0, PROJ_W - B_V0).astype(BF16)

    @pl.when(j == 4)
    def _():
        for c0 in range(0, C_OUT, GROUP_W):
            c_ref[:, c0:c0 + GROUP_W] = (proj(c0, GROUP_W) * Q_SCALE).astype(BF16)
        c_ref[:, C_OUT:] = proj(C_OUT, PROJ_W - C_OUT).astype(BF16)


def _qkv(x, g, w_in, cos, sin_signed, qg, kg, bsz, seq, layer):
    n, d = x.shape
    tm = _tile(seq, 512)
    nseq = seq // tm
    r1, r2 = A_GROUPS[1][1], A_GROUPS[2][1]
    assert tm % (16 * r2) == 0
    a_part = lambda i, j: jnp.minimum(j, 2)

    def split_spec(r):
        return pl.BlockSpec((None, r, tm // r, GROUP_W), lambda i, j: (i // nseq, 0, i % nseq, a_part(i, j)))

    return pl.pallas_call(
        _qkv_kernel,
        grid=(n // tm, QKV_COLS // PROJ_W),
        in_specs=[
            pl.BlockSpec((tm, d), lambda i, j: (i, 0)),
            pl.BlockSpec((None, 1, d), lambda i, j: (layer, 0, 0)),
            pl.BlockSpec((None, d, PROJ_W), lambda i, j: (layer, 0, j)),
            pl.BlockSpec((tm, HEAD_DIM), lambda i, j: (i % nseq, 0)),
            pl.BlockSpec((tm, HEAD_DIM), lambda i, j: (i % nseq, 0)),
            pl.BlockSpec((None, 1, HEAD_DIM), lambda i, j: (layer, 0, 0)),
            pl.BlockSpec((None, 1, HEAD_DIM), lambda i, j: (layer, 0, 0)),
        ],
        out_specs=[
            pl.BlockSpec((tm, GROUP_W), lambda i, j: (i, a_part(i, j))),
            split_spec(r1),
            split_spec(r2),
            pl.BlockSpec((tm, PROJ_W), lambda i, j: (i, 0)),
            pl.BlockSpec((tm, PROJ_W), lambda i, j: (i, 0)),
        ],
        out_shape=[
            jax.ShapeDtypeStruct((n, PROJ_W), BF16),
            jax.ShapeDtypeStruct((bsz, r1, seq // r1, PROJ_W), BF16),
            jax.ShapeDtypeStruct((bsz, r2, seq // r2, PROJ_W), BF16),
            jax.ShapeDtypeStruct((n, PROJ_W), BF16),
            jax.ShapeDtypeStruct((n, PROJ_W), BF16),
        ],
        scratch_shapes=[pltpu.VMEM((tm, d), BF16),
                        pltpu.VMEM(((PROJ_W - GROUP_W) // HEAD_DIM, tm, HEAD_DIM), F32)],
        compiler_params=_params("parallel", "arbitrary"),
        name="qkv_proj",
    )(x, g, w_in, cos, sin_signed, qg, kg)


def _band_kernel(*refs, tq, sub, hb, half_w, n_sub, slopes, dist_scale, n_kv, has_sink, emit_lse):
    if has_sink:
        sink_ref, refs = refs[0], refs[1:]
    q_ref, kp_ref, km_ref, kn_ref, vp_ref, vm_ref, vn_ref, o_ref = refs[:8]
    n_heads = len(slopes)
    group = n_heads // n_kv
    qi = pl.program_id(1)
    span = sub + 2 * hb
    row = lax.broadcasted_iota(jnp.int32, (sub, span), 0)
    col = lax.broadcasted_iota(jnp.int32, (sub, span), 1)
    arel = jnp.abs(col - hb - row)
    in_band = arel <= half_w
    dist = (arel * dist_scale).astype(F32)
    biases = [jnp.where(in_band, (-slopes[h] * LOG2_E) * dist, NEG_INF) for h in range(n_heads)]
    kcat, vcat = [], []
    for kv in range(n_kv):
        ksl = slice(kv * HEAD_DIM, (kv + 1) * HEAD_DIM)
        kcat.append(jnp.concatenate([kp_ref[:, ksl], km_ref[:, ksl], kn_ref[:, ksl]], axis=0))
        vcat.append(jnp.concatenate([vp_ref[:, ksl], vm_ref[:, ksl], vn_ref[:, ksl]], axis=0))
    if emit_lse:
        lane = lax.broadcasted_iota(jnp.int32, (sub, HEAD_DIM), 1)
    for s in range(tq // sub):
        rows = slice(s * sub, (s + 1) * sub)
        kpos = qi * tq + (s * sub - hb) + col
        in_seq = (kpos >= 0) & (kpos < n_sub)
        lse_tile = jnp.zeros((sub, HEAD_DIM), F32) if emit_lse else None
        for h in range(n_heads):
            sl = slice(h * HEAD_DIM, (h + 1) * HEAD_DIM)
            ks = kcat[h // group][s * sub:s * sub + span]
            vs = vcat[h // group][s * sub:s * sub + span]
            sc = lax.dot_general(q_ref[rows, sl], ks, (((1,), (1,)), ((), ())), preferred_element_type=F32)
            sc = jnp.where(in_seq, sc + biases[h], NEG_INF)
            m = jnp.max(sc, axis=-1, keepdims=True)
            if has_sink:
                sk = sink_ref[0, h] * LOG2_E
                m = jnp.maximum(m, sk)
            p = jnp.exp2(sc - m)
            den = jnp.sum(p, axis=-1, keepdims=True)
            if has_sink:
                den = den + jnp.exp2(sk - m)
            o = jnp.dot(p.astype(BF16), vs, preferred_element_type=F32) * (1.0 / den)
            o_ref[rows, sl] = o.astype(o_ref.dtype)
            if emit_lse:
                lse_tile = jnp.where(lane == h, m + jnp.log2(den), lse_tile)
        if emit_lse:
            refs[8][rows, :] = lse_tile


def _band_gqa_kernel(*refs, tq, sub, hb, half_w, n_sub, slopes, dist_scale, n_kv, has_sink, emit_lse):
    assert not emit_lse
    if has_sink:
        sink_ref, refs = refs[0], refs[1:]
    q_ref, kp_ref, km_ref, kn_ref, vp_ref, vm_ref, vn_ref, o_ref = refs
    n_heads = len(slopes)
    group = n_heads // n_kv
    qi = pl.program_id(1)
    span = sub + 2 * hb
    head_sl = lambda h: slice(h * HEAD_DIM, (h + 1) * HEAD_DIM)
    key = lax.broadcasted_iota(jnp.int32, (span, sub), 0)
    qry = lax.broadcasted_iota(jnp.int32, (span, sub), 1)
    arel = jnp.abs(key - hb - qry)
    in_band = arel <= half_w
    dist = (arel * dist_scale).astype(F32)
    key_wide = lax.broadcasted_iota(jnp.int32, (span, group * sub), 0)
    biases, sinks, kcat, vcat = [], [], [], []
    for kv in range(n_kv):
        heads = range(kv * group, (kv + 1) * group)
        biases.append(jnp.concatenate(
            [jnp.where(in_band, (-slopes[h] * LOG2_E) * dist, NEG_INF) for h in heads], axis=1))
        if has_sink:
            sinks.append(jnp.concatenate(
                [jnp.full((1, sub), sink_ref[0, h] * LOG2_E, F32) for h in heads], axis=1))
        ksl = head_sl(kv)
        kcat.append(jnp.concatenate([kp_ref[:, ksl], km_ref[:, ksl], kn_ref[:, ksl]], axis=0))
        vcat.append(jnp.concatenate([vp_ref[:, ksl], vm_ref[:, ksl], vn_ref[:, ksl]], axis=0))
    for s in range(tq // sub):
        rows = slice(s * sub, (s + 1) * sub)
        kpos = qi * tq + (s * sub - hb) + key_wide
        in_seq = (kpos >= 0) & (kpos < n_sub)
        for kv in range(n_kv):
            heads = range(kv * group, (kv + 1) * group)
            ks = kcat[kv][s * sub:s * sub + span]
            vs = vcat[kv][s * sub:s * sub + span]
            q = jnp.concatenate([q_ref[rows, head_sl(h)] for h in heads], axis=0)
            st = lax.dot_general(ks, q, (((1,), (1,)), ((), ())), preferred_element_type=F32)
            st = jnp.where(in_seq, st + biases[kv], NEG_INF)
            m = jnp.max(st, axis=0, keepdims=True)
            if has_sink:
                m = jnp.maximum(m, sinks[kv])
            p = jnp.exp2(st - m)
            den = jnp.sum(p, axis=0, keepdims=True)
            if has_sink:
                den = den + jnp.exp2(sinks[kv] - m)
            ot = lax.dot_general(vs, p.astype(BF16), (((0,), (0,)), ((), ())), preferred_element_type=F32)
            ot = ot * (1.0 / den)
            for gi, h in enumerate(heads):
                o_ref[rows, head_sl(h)] = ot[:, gi * sub:(gi + 1) * sub].T.astype(o_ref.dtype)


def _band_attention(arr, *, q_col, k_col, v_col, n_heads, n_kv, half_w, slopes, dist_scale, sink, emit_lse,
                    out_dtype):
    nb, n_sub, _ = arr.shape
    hb = half_w
    sub = min(HEAD_DIM, n_sub)
    tq = _tile(n_sub, 512)
    assert sub % hb == 0 and tq % sub == 0 and n_sub % tq == 0
    per = tq // hb
    n_hb = n_sub // hb
    qw, kw = n_heads * HEAD_DIM, n_kv * HEAD_DIM
    assert q_col % qw == 0 and k_col % kw == 0 and v_col % kw == 0

    def halo_specs(c):
        return [
            pl.BlockSpec((None, hb, kw), lambda b, i: (b, jnp.maximum(i * per - 1, 0), c // kw)),
            pl.BlockSpec((None, tq, kw), lambda b, i: (b, i, c // kw)),
            pl.BlockSpec((None, hb, kw), lambda b, i: (b, jnp.minimum((i + 1) * per, n_hb - 1), c // kw)),
        ]

    in_specs = [pl.BlockSpec((None, tq, qw), lambda b, i: (b, i, q_col // qw))] + halo_specs(k_col) + halo_specs(v_col)
    args = [arr] * 7
    if sink is not None:
        in_specs = [pl.BlockSpec(memory_space=pltpu.SMEM)] + in_specs
        args = [sink] + args
    out_specs = [pl.BlockSpec((None, tq, qw), lambda b, i: (b, i, 0))]
    out_shape = [jax.ShapeDtypeStruct((nb, n_sub, qw), out_dtype)]
    if emit_lse:
        out_specs.append(pl.BlockSpec((None, tq, HEAD_DIM), lambda b, i: (b, i, 0)))
        out_shape.append(jax.ShapeDtypeStruct((nb, n_sub, HEAD_DIM), F32))
    body = _band_gqa_kernel if n_kv < n_heads else _band_kernel
    return pl.pallas_call(
        functools.partial(body, tq=tq, sub=sub, hb=hb, half_w=half_w, n_sub=n_sub, slopes=slopes,
                          dist_scale=dist_scale, n_kv=n_kv, has_sink=sink is not None, emit_lse=emit_lse),
        grid=(nb, n_sub // tq),
        in_specs=in_specs,
        out_specs=out_specs,
        out_shape=out_shape,
        compiler_params=_params("parallel", "arbitrary"),
        name=f"band_attn_n{n_sub}_w{half_w}",
    )(*args)


def _merge_kernel(o0_ref, o1_ref, o2_ref, l0_ref, l1_ref, l2_ref, out_ref, ob1_ref, ob2_ref, lb1_ref, lb2_ref):
    tm = out_ref.shape[0]
    for src_ref, dst_ref in ((o1_ref, ob1_ref), (o2_ref, ob2_ref), (l1_ref, lb1_ref), (l2_ref, lb2_ref)):
        r = src_ref.shape[0]
        for p in range(r):
            for h in range(dst_ref.shape[0]):
                dst_ref[h, pl.ds(p, tm // r, stride=r), :] = src_ref[p, :, h * HEAD_DIM:(h + 1) * HEAD_DIM]
    l0, l1, l2 = l0_ref[...], lb1_ref[0], lb2_ref[0]
    mx = jnp.maximum(jnp.maximum(l0, l1), l2)
    w0, w1, w2 = jnp.exp2(l0 - mx), jnp.exp2(l1 - mx), jnp.exp2(l2 - mx)
    den = w0 + w1 + w2
    a0, a1, a2 = w0 / den, w1 / den, w2 / den
    for h in range(A_HEADS_PER_GROUP):
        sl = slice(h * HEAD_DIM, (h + 1) * HEAD_DIM)
        out = a0[:, h:h + 1] * o0_ref[:, sl] + a1[:, h:h + 1] * ob1_ref[h] + a2[:, h:h + 1] * ob2_ref[h]
        out_ref[:, sl] = out.astype(BF16)


def _merge_groups(outs, lses, bsz, seq):
    n = bsz * seq
    tm = _tile(seq, 512)
    nseq = seq // tm
    r1, r2 = A_GROUPS[1][1], A_GROUPS[2][1]

    def specs(w):
        return [
            pl.BlockSpec((tm, w), lambda i: (i, 0)),
            pl.BlockSpec((None, r1, tm // r1, w), lambda i: (i // nseq, 0, i % nseq, 0)),
            pl.BlockSpec((None, r2, tm // r2, w), lambda i: (i // nseq, 0, i % nseq, 0)),
        ]

    def views(xs, w):
        return [xs[0].reshape(n, w), xs[1].reshape(bsz, r1, seq // r1, w), xs[2].reshape(bsz, r2, seq // r2, w)]

    return pl.pallas_call(
        _merge_kernel,
        grid=(n // tm,),
        in_specs=specs(A_OUT) + specs(HEAD_DIM),
        out_specs=pl.BlockSpec((tm, A_OUT), lambda i: (i, 0)),
        out_shape=jax.ShapeDtypeStruct((n, A_OUT), BF16),
        scratch_shapes=[pltpu.VMEM((A_HEADS_PER_GROUP, tm, HEAD_DIM), F32),
                        pltpu.VMEM((A_HEADS_PER_GROUP, tm, HEAD_DIM), F32),
                        pltpu.VMEM((1, tm, HEAD_DIM), F32), pltpu.VMEM((1, tm, HEAD_DIM), F32)],
        compiler_params=_params("parallel"),
        name="merge_dilated",
    )(*views(outs, A_OUT), *views(lses, HEAD_DIM))


def _dense_kernel(q_ref, k_ref, v_ref, o_ref, qt_ref, st_ref, m_ref, l_ref, acc_ref, *, tq, tk, seq):
    group = B_Q_HEADS // B_KV_HEADS
    n_blk = seq // tk
    for g in range(group):
        qg = q_ref[:, g * HEAD_DIM:(g + 1) * HEAD_DIM].astype(F32)
        qt_ref[:, g * tq:(g + 1) * tq] = qg.T.astype(BF16)

    def scores(blk):
        k0 = pl.multiple_of(blk * tk, tk)
        return jnp.dot(k_ref[pl.ds(k0, tk), :], qt_ref[...], preferred_element_type=F32)

    def values(blk):
        v0 = pl.multiple_of(blk * tk, tk)
        return v_ref[pl.ds(v0, tk), :]

    def pv(v, p):
        return lax.dot_general(v, p.astype(BF16), (((0,), (0,)), ((), ())), preferred_element_type=F32)

    def write_out():
        out = acc_ref[...] / l_ref[...]
        for g in range(group):
            o_ref[:, g * HEAD_DIM:(g + 1) * HEAD_DIM] = out[:, g * tq:(g + 1) * tq].T.astype(BF16)

    first = jnp.dot(k_ref[pl.ds(0, min(tk, HEAD_DIM)), :], qt_ref[...], preferred_element_type=F32)
    m_ref[...] = jnp.max(first, axis=0, keepdims=True)
    l_ref[...] = jnp.zeros(l_ref.shape, F32)
    acc_ref[...] = jnp.zeros(acc_ref.shape, F32)

    def lazy_step(blk):
        k0 = pl.multiple_of(blk * tk, tk)
        k = k_ref[pl.ds(k0, tk), :]
        v = values(blk)
        width = LANE_CHUNK_W
        chunks = [slice(c * width, (c + 1) * width) for c in range(group * tq // width)]
        sts = [jnp.dot(k, qt_ref[:, sl], preferred_element_type=F32) for sl in chunks]
        for sl, st in zip(chunks, sts):
            p = jnp.exp2(st - m_ref[:, sl])
            l_ref[:, sl] += jnp.sum(p, axis=0, keepdims=True)
            acc_ref[:, sl] += pv(v, p)

    def lazy_body(c, carry):
        lazy_step(2 * c)
        lazy_step(2 * c + 1)
        return carry

    lax.fori_loop(0, n_blk // 2, lazy_body, 0)
    l = l_ref[...]
    sums_ok = jnp.min(jnp.where(jnp.abs(acc_ref[...]) < F32_HUGE, 1.0, 0.0))
    sums_ok = sums_ok * jnp.min(jnp.where((l > 0.0) & (l < F32_HUGE), 1.0, 0.0))
    exceeded = sums_ok < 0.5

    @pl.when(jnp.logical_not(exceeded))
    def _():
        write_out()

    @pl.when(exceeded)
    def _():
        _dense_online(scores, values, pv, write_out, st_ref, m_ref, l_ref, acc_ref, n_blk)


def _dense_online(scores, values, pv, write_out, st_ref, m_ref, l_ref, acc_ref, n_blk):
    m_ref[...] = jnp.full(m_ref.shape, -jnp.inf, F32)
    l_ref[...] = jnp.zeros(l_ref.shape, F32)
    acc_ref[...] = jnp.zeros(acc_ref.shape, F32)
    st_ref[0] = scores(0)

    def step(blk, cur, nxt):
        st_ref[nxt] = scores(jnp.minimum(blk + 1, n_blk - 1))
        st = st_ref[cur]
        m_prev = m_ref[...]
        m_new = jnp.maximum(m_prev, jnp.max(st, axis=0, keepdims=True))
        alpha = jnp.exp2(m_prev - m_new)
        p = jnp.exp2(st - m_new)
        l_ref[...] = alpha * l_ref[...] + jnp.sum(p, axis=0, keepdims=True)
        acc_ref[...] = alpha * acc_ref[...] + pv(values(blk), p)
        m_ref[...] = m_new

    def body(c, carry):
        step(2 * c, 0, 1)
        step(2 * c + 1, 1, 0)
        return carry

    lax.fori_loop(0, n_blk // 2, body, 0)
    write_out()


def _dense_attention(arr):
    bsz, seq, _ = arr.shape
    tq = _tile(seq, 1024)
    tk = _tile(seq // 2, 512)
    assert (seq // tk) % 2 == 0
    group = B_Q_HEADS // B_KV_HEADS
    qw = group * HEAD_DIM
    out = pl.pallas_call(
        functools.partial(_dense_kernel, tq=tq, tk=tk, seq=seq),
        grid=(bsz, B_KV_HEADS, seq // tq),
        in_specs=[
            pl.BlockSpec((None, tq, qw), lambda b, h, i: (b, i, h)),
            pl.BlockSpec((None, seq, HEAD_DIM), lambda b, h, i: (b, 0, B_K0 // HEAD_DIM + h)),
            pl.BlockSpec((None, seq, HEAD_DIM), lambda b, h, i: (b, 0, B_V0 // HEAD_DIM + h)),
        ],
        out_specs=pl.BlockSpec((None, tq, qw), lambda b, h, i: (b, i, h)),
        out_shape=jax.ShapeDtypeStruct((bsz, seq, B_OUT), BF16),
        scratch_shapes=[
            pltpu.VMEM((HEAD_DIM, group * tq), BF16),
            pltpu.VMEM((2, tk, group * tq), F32),
            pltpu.VMEM((1, group * tq), F32),
            pltpu.VMEM((1, group * tq), F32),
            pltpu.VMEM((HEAD_DIM, group * tq), F32),
        ],
        compiler_params=_params("parallel", "parallel", "arbitrary"),
        name="dense_attn",
    )(arr, arr, arr)
    return out.reshape(bsz * seq, B_OUT)


def _out_kernel(x_ref, g_ref, wg0_ref, wg1_ref, wg2_ref, oa_ref, ob_ref, oc_ref,
                wa_ref, wb_ref, wc_ref, wo_ref, o_ref, u_ref):
    j = pl.program_id(1)

    @pl.when(j == 0)
    def _():
        x = x_ref[...]
        u_ref[...] = _rms(x, g_ref[...]).astype(BF16)
        o_ref[...] = x

    u = u_ref[...]

    def branch(wg_ref, br_ref, w_ref):
        gate = jax.nn.sigmoid(jnp.dot(u, wg_ref[...], preferred_element_type=F32))
        return gate * jnp.dot(br_ref[...], w_ref[...], preferred_element_type=F32)

    merged = branch(wg0_ref, oa_ref, wa_ref) + branch(wg1_ref, ob_ref, wb_ref) + branch(wg2_ref, oc_ref, wc_ref)
    o_ref[...] += jnp.dot(merged.astype(BF16), wo_ref[...], preferred_element_type=F32)


def _mixer_out(x, g, w_in, o_a, o_b, o_c, w_a, w_b, w_c, w_o, layer):
    n, d = x.shape
    tm = _tile(n, 512)
    tc = _tile(d, 512)
    assert QKV_COLS % tc == 0
    g0 = QKV_COLS // tc
    gstep = d // tc

    def gate_spec(br):
        return pl.BlockSpec((None, d, tc), lambda i, j: (layer, 0, g0 + br * gstep + j))

    return pl.pallas_call(
        _out_kernel,
        grid=(n // tm, d // tc),
        in_specs=[
            pl.BlockSpec((tm, d), lambda i, j: (i, 0)),
            pl.BlockSpec((None, 1, d), lambda i, j: (layer, 0, 0)),
            gate_spec(0), gate_spec(1), gate_spec(2),
            pl.BlockSpec((tm, A_OUT), lambda i, j: (i, 0)),
            pl.BlockSpec((tm, B_OUT), lambda i, j: (i, 0)),
            pl.BlockSpec((tm, C_OUT), lambda i, j: (i, 0)),
            pl.BlockSpec((None, A_OUT, tc), lambda i, j: (layer, 0, j)),
            pl.BlockSpec((None, B_OUT, tc), lambda i, j: (layer, 0, j)),
            pl.BlockSpec((None, C_OUT, tc), lambda i, j: (layer, 0, j)),
            pl.BlockSpec((None, tc, d), lambda i, j: (layer, j, 0)),
        ],
        out_specs=pl.BlockSpec((tm, d), lambda i, j: (i, 0)),
        out_shape=jax.ShapeDtypeStruct((n, d), F32),
        scratch_shapes=[pltpu.VMEM((tm, d), BF16)],
        compiler_params=_params("parallel", "arbitrary"),
        name="mixer_out",
    )(x, g, w_in, w_in, w_in, o_a, o_b, o_c, w_a, w_b, w_c, w_o)


def _rope_tables(seq):
    n_freq = HEAD_DIM // 4
    t = lax.broadcasted_iota(jnp.int32, (seq, HEAD_DIM), 0)
    lane = lax.broadcasted_iota(jnp.int32, (seq, HEAD_DIM), 1)
    pair = lane // 2
    pos = jnp.where(pair < n_freq, t // GRID_W, t % GRID_W).astype(F32)
    inv = ROPE_THETA ** (-(pair % n_freq).astype(F32) / n_freq)
    ang = pos * inv
    sign = jnp.where(lane % 2 == 0, -1.0, 1.0).astype(F32)
    return jnp.cos(ang), jnp.sin(ang) * sign


def _mixer(x, lw, cos, sin_signed, bsz, seq, layer):
    a0, a1, a2, bqkv, cqkv = _qkv(x, lw["norm_mix"], lw["w_in"], cos, sin_signed, lw["qk_norm_q"],
                                  lw["qk_norm_k"], bsz, seq, layer)
    a_slopes = _alibi_slopes(A_HEADS)
    outs, lses = [], []
    for gi, (arr, (window, r)) in enumerate(zip((a0, a1, a2), A_GROUPS)):
        hs = slice(gi * A_HEADS_PER_GROUP, (gi + 1) * A_HEADS_PER_GROUP)
        o, l = _band_attention(
            arr.reshape(bsz * r, seq // r, PROJ_W), q_col=0, k_col=GROUP_W, v_col=2 * GROUP_W,
            n_heads=A_HEADS_PER_GROUP, n_kv=A_HEADS_PER_GROUP, half_w=(window // 2) // r, slopes=a_slopes[hs],
            dist_scale=r, sink=None, emit_lse=True, out_dtype=F32)
        outs.append(o)
        lses.append(l)
    o_a = _merge_groups(outs, lses, bsz, seq)
    o_b = _dense_attention(bqkv.reshape(bsz, seq, PROJ_W))
    o_c, = _band_attention(
        cqkv.reshape(bsz, seq, PROJ_W), q_col=0, k_col=C_K0, v_col=C_V0, n_heads=C_Q_HEADS, n_kv=C_KV_HEADS,
        half_w=C_HALF_WINDOW, slopes=_alibi_slopes(C_Q_HEADS), dist_scale=1, sink=lw["sink_c"][layer],
        emit_lse=False, out_dtype=BF16)
    return _mixer_out(x, lw["norm_mix"], lw["w_in"], o_a, o_b, o_c.reshape(bsz * seq, C_OUT), lw["w_br_a"],
                      lw["w_br_b"], lw["w_br_c"], lw["w_out"], layer)


def _trunk(x, lw, norm_final, depth):
    bsz, seq, d = x.shape
    cos, sin_signed = _rope_tables(seq)
    xf = x.reshape(bsz * seq, d)
    for layer in range(depth):
        xf = _ffn(xf, lw["norm_ffn1"], lw["ffn1_w_in"], lw["ffn1_w_out"], layer)
        xf = _mixer(xf, lw, cos, sin_signed, bsz, seq, layer)
        last = layer == depth - 1
        xf = _ffn(xf, lw["norm_ffn2"], lw["ffn2_w_in"], lw["ffn2_w_out"], layer,
                  final_gain=norm_final.reshape(1, d) if last else None)
    return xf.reshape(bsz, seq, d)


def kernel(x_prompt, x_sample, norm_ffn1, ffn1_w_in, ffn1_w_out, norm_mix, w_in, qk_norm_q, qk_norm_k, sink_c,
           w_br_a, w_br_b, w_br_c, w_out, norm_ffn2, ffn2_w_in, ffn2_w_out, norm_final):
    depth = norm_ffn1.shape[0]
    row = lambda v: v.reshape(depth, 1, v.shape[-1])
    layers = {
        "norm_ffn1": row(norm_ffn1), "ffn1_w_in": ffn1_w_in.astype(BF16), "ffn1_w_out": ffn1_w_out.astype(BF16),
        "norm_mix": row(norm_mix), "w_in": w_in.astype(BF16),
        "qk_norm_q": row(qk_norm_q), "qk_norm_k": row(qk_norm_k), "sink_c": row(sink_c),
        "w_br_a": w_br_a.astype(BF16), "w_br_b": w_br_b.astype(BF16), "w_br_c": w_br_c.astype(BF16),
        "w_out": w_out.astype(BF16),
        "norm_ffn2": row(norm_ffn2), "ffn2_w_in": ffn2_w_in.astype(BF16), "ffn2_w_out": ffn2_w_out.astype(BF16),
    }
    assert x_prompt.shape[1:] == x_sample.shape[1:]
    n_prompt = x_prompt.shape[0]
    y = _trunk(jnp.concatenate([x_prompt, x_sample], axis=0), layers, norm_final, depth)
    return (y[:n_prompt], y[n_prompt:])
```

```python
import functools

import jax
import jax.numpy as jnp
from jax import lax
from jax.experimental import pallas as pl
from jax.experimental.pallas import tpu as pltpu

F32 = jnp.float32
BF16 = jnp.bfloat16

HEAD_DIM = 128
SCALE = HEAD_DIM ** -0.5
LOG2_E = 1.4426950408889634
Q_SCALE = SCALE * LOG2_E
NORM_EPS = 1e-6
NEG_INF = -1e30
F32_HUGE = 1e38
LANE_CHUNK_W = 256
ROPE_THETA = 10000.0
GRID_W = 64
A_GROUPS = ((128, 1), (512, 4), (2048, 16))
A_HEADS_PER_GROUP = 4
A_HEADS = 12
B_Q_HEADS = 8
B_KV_HEADS = 2
C_Q_HEADS = 8
C_KV_HEADS = 2
C_HALF_WINDOW = 128
N_BRANCHES = 3

QKV_COLS = 7680
PROJ_W = 1536
GROUP_W = 4 * HEAD_DIM
A_OUT = A_HEADS_PER_GROUP * HEAD_DIM
B_OUT = B_Q_HEADS * HEAD_DIM
C_OUT = C_Q_HEADS * HEAD_DIM
B_K0, B_V0 = B_OUT, B_OUT + B_KV_HEADS * HEAD_DIM
C_K0, C_V0 = C_OUT, C_OUT + C_KV_HEADS * HEAD_DIM

V7X_VMEM_LIMIT_BYTES = 56 * 1024 * 1024

FFN_ROW_TILE, FFN_HIDDEN_TILE = 1024, 512
PROJ_ROW_TILE = 512
OUT_ROW_TILE, OUT_COL_TILE = 512, 512
DENSE_Q_TILE, DENSE_K_TILE = 1024, 512
BAND_Q_TILE = 1024
MERGE_ROW_TILE = 512


def _tile(n, pref):
    t = min(pref, n)
    while n % t:
        t //= 2
    return t


def _params(*sem):
    return pltpu.CompilerParams(dimension_semantics=sem, vmem_limit_bytes=V7X_VMEM_LIMIT_BYTES)


def _rms(x, g):
    ms = jnp.mean(x * x, axis=-1, keepdims=True)
    return x * lax.rsqrt(ms + NORM_EPS) * g


def _alibi_slopes(n):
    return tuple(2.0 ** (-8.0 * i / n) for i in range(1, n + 1))


def _ffn_kernel(x_ref, g_ref, wa_ref, wb_ref, wo_ref, *rest, final_norm):
    o_ref, xn_ref = rest[-2:]
    j = pl.program_id(1)

    @pl.when(j == 0)
    def _():
        x = x_ref[...]
        xn_ref[...] = _rms(x, g_ref[...]).astype(BF16)
        o_ref[...] = x

    xn = xn_ref[...]
    a = jnp.dot(xn, wa_ref[...], preferred_element_type=F32)
    b = jnp.dot(xn, wb_ref[...], preferred_element_type=F32)
    h = (0.5 * a * jax.nn.sigmoid(a) * b).astype(BF16)
    o_ref[...] += jnp.dot(h, wo_ref[...], preferred_element_type=F32)

    if final_norm:
        @pl.when(j == pl.num_programs(1) - 1)
        def _():
            o_ref[...] = _rms(o_ref[...], rest[0][...])


def _ffn(x, g, w_in, w_out, layer, final_gain=None):
    n, d = x.shape
    d_ff = w_out.shape[1]
    tm = _tile(n, FFN_ROW_TILE)
    tf = _tile(d_ff, FFN_HIDDEN_TILE)
    nf = d_ff // tf
    in_specs = [
        pl.BlockSpec((tm, d), lambda i, j: (i, 0)),
        pl.BlockSpec((None, 1, d), lambda i, j: (layer, 0, 0)),
        pl.BlockSpec((None, d, tf), lambda i, j: (layer, 0, j)),
        pl.BlockSpec((None, d, tf), lambda i, j: (layer, 0, j + nf)),
        pl.BlockSpec((None, tf, d), lambda i, j: (layer, j, 0)),
    ]
    args = [x, g, w_in, w_in, w_out]
    if final_gain is not None:
        in_specs.append(pl.BlockSpec((1, d), lambda i, j: (0, 0)))
        args.append(final_gain)
    return pl.pallas_call(
        functools.partial(_ffn_kernel, final_norm=final_gain is not None),
        grid=(n // tm, nf),
        in_specs=in_specs,
        out_specs=pl.BlockSpec((tm, d), lambda i, j: (i, 0)),
        out_shape=jax.ShapeDtypeStruct((n, d), F32),
        scratch_shapes=[pltpu.VMEM((tm, d), BF16)],
        compiler_params=_params("parallel", "arbitrary"),
        name="ffn",
    )(*args)


def _norm_rope(y, gain, cos, sin_signed):
    yn = _rms(y, gain)
    lane = lax.broadcasted_iota(jnp.int32, yn.shape, 1)
    partner = jnp.where((lane & 1) == 0, pltpu.roll(yn, HEAD_DIM - 1, 1), pltpu.roll(yn, 1, 1))
    return yn * cos + partner * sin_signed


def _qkv_kernel(x_ref, g_ref, w_ref, cos_ref, sin_ref, qg_ref, kg_ref,
                a0_ref, a1_ref, a2_ref, b_ref, c_ref, u_ref, split_ref):
    j = pl.program_id(1)
    tm = x_ref.shape[0]

    @pl.when(j == 0)
    def _():
        u_ref[...] = _rms(x_ref[...], g_ref[...]).astype(BF16)

    def proj(c0, width):
        return jnp.dot(u_ref[...], w_ref[:, c0:c0 + width], preferred_element_type=F32)

    @pl.when(j < 3)
    def _():
        sc = jnp.where(j == 0, Q_SCALE, 1.0).astype(F32)
        for gi, dst_ref in ((2, a2_ref), (1, a1_ref)):
            r = A_GROUPS[gi][1]
            y = proj(gi * GROUP_W, GROUP_W) * sc
            for h in range(A_HEADS_PER_GROUP):
                split_ref[(gi - 1) * A_HEADS_PER_GROUP + h] = y[:, h * HEAD_DIM:(h + 1) * HEAD_DIM]
            for p in range(r):
                for h in range(A_HEADS_PER_GROUP):
                    rows = split_ref[(gi - 1) * A_HEADS_PER_GROUP + h, pl.ds(p, tm // r, stride=r), :]
                    dst_ref[p, :, h * HEAD_DIM:(h + 1) * HEAD_DIM] = rows.astype(BF16)
        a0_ref[...] = (proj(0, GROUP_W) * sc).astype(BF16)

    @pl.when(j == 3)
    def _():
        pair = 2 * HEAD_DIM
        for c0 in range(0, B_V0, pair):
            acc = proj(c0, pair)
            for h0 in range(0, pair, HEAD_DIM):
                is_q = c0 + h0 < B_K0
                y = _norm_rope(acc[:, h0:h0 + HEAD_DIM], (qg_ref if is_q else kg_ref)[...], cos_ref[...],
                               sin_ref[...])
                b_ref[:, c0 + h0:c0 + h0 + HEAD_DIM] = ((y * Q_SCALE) if is_q else y).astype(BF16)
        b_ref[:, B_V0:] = proj(B_V0, PROJ_W - B_V0).astype(BF16)

    @pl.when(j == 4)
    def _():
        for c0 in range(0, C_OUT, GROUP_W):
            c_ref[:, c0:c0 + GROUP_W] = (proj(c0, GROUP_W) * Q_SCALE).astype(BF16)
        c_ref[:, C_OUT:] = proj(C_OUT, PROJ_W - C_OUT).astype(BF16)


def _qkv(x, g, w_in, cos, sin_signed, qg, kg, bsz, seq, layer):
    n, d = x.shape
    tm = _tile(seq, PROJ_ROW_TILE)
    nseq = seq // tm
    r1, r2 = A_GROUPS[1][1], A_GROUPS[2][1]
    assert tm % (16 * r2) == 0
    a_part = lambda i, j: jnp.minimum(j, 2)

    def split_spec(r):
        return pl.BlockSpec((None, r, tm // r, GROUP_W), lambda i, j: (i // nseq, 0, i % nseq, a_part(i, j)))

    return pl.pallas_call(
        _qkv_kernel,
        grid=(n // tm, QKV_COLS // PROJ_W),
        in_specs=[
            pl.BlockSpec((tm, d), lambda i, j: (i, 0)),
            pl.BlockSpec((None, 1, d), lambda i, j: (layer, 0, 0)),
            pl.BlockSpec((None, d, PROJ_W), lambda i, j: (layer, 0, j)),
            pl.BlockSpec((tm, HEAD_DIM), lambda i, j: (i % nseq, 0)),
            pl.BlockSpec((tm, HEAD_DIM), lambda i, j: (i % nseq, 0)),
            pl.BlockSpec((None, 1, HEAD_DIM), lambda i, j: (layer, 0, 0)),
            pl.BlockSpec((None, 1, HEAD_DIM), lambda i, j: (layer, 0, 0)),
        ],
        out_specs=[
            pl.BlockSpec((tm, GROUP_W), lambda i, j: (i, a_part(i, j))),
            split_spec(r1),
            split_spec(r2),
            pl.BlockSpec((tm, PROJ_W), lambda i, j: (i, 0)),
            pl.BlockSpec((tm, PROJ_W), lambda i, j: (i, 0)),
        ],
        out_shape=[
            jax.ShapeDtypeStruct((n, PROJ_W), BF16),
            jax.ShapeDtypeStruct((bsz, r1, seq // r1, PROJ_W), BF16),
            jax.ShapeDtypeStruct((bsz, r2, seq // r2, PROJ_W), BF16),
            jax.ShapeDtypeStruct((n, PROJ_W), BF16),
            jax.ShapeDtypeStruct((n, PROJ_W), BF16),
        ],
        scratch_shapes=[pltpu.VMEM((tm, d), BF16),
                        pltpu.VMEM(((PROJ_W - GROUP_W) // HEAD_DIM, tm, HEAD_DIM), F32)],
        compiler_params=_params("parallel", "arbitrary"),
        name="qkv_proj",
    )(x, g, w_in, cos, sin_signed, qg, kg)


def _band_kernel(*refs, tq, sub, hb, half_w, n_sub, slopes, dist_scale, n_kv, has_sink, emit_lse):
    if has_sink:
        sink_ref, refs = refs[0], refs[1:]
    q_ref, kp_ref, km_ref, kn_ref, vp_ref, vm_ref, vn_ref, o_ref = refs[:8]
    n_heads = len(slopes)
    group = n_heads // n_kv
    qi = pl.program_id(1)
    span = sub + 2 * hb
    row = lax.broadcasted_iota(jnp.int32, (sub, span), 0)
    col = lax.broadcasted_iota(jnp.int32, (sub, span), 1)
    arel = jnp.abs(col - hb - row)
    in_band = arel <= half_w
    dist = (arel * dist_scale).astype(F32)
    biases = [jnp.where(in_band, (-slopes[h] * LOG2_E) * dist, NEG_INF) for h in range(n_heads)]
    kcat, vcat = [], []
    for kv in range(n_kv):
        ksl = slice(kv * HEAD_DIM, (kv + 1) * HEAD_DIM)
        kcat.append(jnp.concatenate([kp_ref[:, ksl], km_ref[:, ksl], kn_ref[:, ksl]], axis=0))
        vcat.append(jnp.concatenate([vp_ref[:, ksl], vm_ref[:, ksl], vn_ref[:, ksl]], axis=0))
    if emit_lse:
        lane = lax.broadcasted_iota(jnp.int32, (sub, HEAD_DIM), 1)
    for s in range(tq // sub):
        rows = slice(s * sub, (s + 1) * sub)
        kpos = qi * tq + (s * sub - hb) + col
        in_seq = (kpos >= 0) & (kpos < n_sub)
        lse_tile = jnp.zeros((sub, HEAD_DIM), F32) if emit_lse else None
        for h in range(n_heads):
            sl = slice(h * HEAD_DIM, (h + 1) * HEAD_DIM)
            ks = kcat[h // group][s * sub:s * sub + span]
            vs = vcat[h // group][s * sub:s * sub + span]
            sc = lax.dot_general(q_ref[rows, sl], ks, (((1,), (1,)), ((), ())), preferred_element_type=F32)
            sc = jnp.where(in_seq, sc + biases[h], NEG_INF)
            m = jnp.max(sc, axis=-1, keepdims=True)
            if has_sink:
                sk = sink_ref[0, h] * LOG2_E
                m = jnp.maximum(m, sk)
            p = jnp.exp2(sc - m)
            den = jnp.sum(p, axis=-1, keepdims=True)
            if has_sink:
                den = den + jnp.exp2(sk - m)
            o = jnp.dot(p.astype(BF16), vs, preferred_element_type=F32) * (1.0 / den)
            o_ref[rows, sl] = o.astype(o_ref.dtype)
            if emit_lse:
                lse_tile = jnp.where(lane == h, m + jnp.log2(den), lse_tile)
        if emit_lse:
            refs[8][rows, :] = lse_tile


def _band_gqa_kernel(*refs, tq, sub, hb, half_w, n_sub, slopes, dist_scale, n_kv, has_sink, emit_lse):
    assert not emit_lse
    if has_sink:
        sink_ref, refs = refs[0], refs[1:]
    q_ref, kp_ref, km_ref, kn_ref, vp_ref, vm_ref, vn_ref, o_ref = refs
    n_heads = len(slopes)
    group = n_heads // n_kv
    qi = pl.program_id(1)
    span = sub + 2 * hb
    head_sl = lambda h: slice(h * HEAD_DIM, (h + 1) * HEAD_DIM)
    key = lax.broadcasted_iota(jnp.int32, (span, sub), 0)
    qry = lax.broadcasted_iota(jnp.int32, (span, sub), 1)
    arel = jnp.abs(key - hb - qry)
    in_band = arel <= half_w
    dist = (arel * dist_scale).astype(F32)
    key_wide = lax.broadcasted_iota(jnp.int32, (span, group * sub), 0)
    biases, sinks, kcat, vcat = [], [], [], []
    for kv in range(n_kv):
        heads = range(kv * group, (kv + 1) * group)
        biases.append(jnp.concatenate(
            [jnp.where(in_band, (-slopes[h] * LOG2_E) * dist, NEG_INF) for h in heads], axis=1))
        if has_sink:
            sinks.append(jnp.concatenate(
                [jnp.full((1, sub), sink_ref[0, h] * LOG2_E, F32) for h in heads], axis=1))
        ksl = head_sl(kv)
        kcat.append(jnp.concatenate([kp_ref[:, ksl], km_ref[:, ksl], kn_ref[:, ksl]], axis=0))
        vcat.append(jnp.concatenate([vp_ref[:, ksl], vm_ref[:, ksl], vn_ref[:, ksl]], axis=0))
    for s in range(tq // sub):
        rows = slice(s * sub, (s + 1) * sub)
        kpos = qi * tq + (s * sub - hb) + key_wide
        in_seq = (kpos >= 0) & (kpos < n_sub)
        for kv in range(n_kv):
            heads = range(kv * group, (kv + 1) * group)
            ks = kcat[kv][s * sub:s * sub + span]
            vs = vcat[kv][s * sub:s * sub + span]
            q = jnp.concatenate([q_ref[rows, head_sl(h)] for h in heads], axis=0)
            st = lax.dot_general(ks, q, (((1,), (1,)), ((), ())), preferred_element_type=F32)
            st = jnp.where(in_seq, st + biases[kv], NEG_INF)
            m = jnp.max(st, axis=0, keepdims=True)
            if has_sink:
                m = jnp.maximum(m, sinks[kv])
            p = jnp.exp2(st - m)
            den = jnp.sum(p, axis=0, keepdims=True)
            if has_sink:
                den = den + jnp.exp2(sinks[kv] - m)
            ot = lax.dot_general(vs, p.astype(BF16), (((0,), (0,)), ((), ())), preferred_element_type=F32)
            ot = ot * (1.0 / den)
            for gi, h in enumerate(heads):
                o_ref[rows, head_sl(h)] = ot[:, gi * sub:(gi + 1) * sub].T.astype(o_ref.dtype)


def _band_attention(arr, *, q_col, k_col, v_col, n_heads, n_kv, half_w, slopes, dist_scale, sink, emit_lse,
                    out_dtype):
    nb, n_sub, _ = arr.shape
    hb = half_w
    sub = min(HEAD_DIM, n_sub)
    tq = _tile(n_sub, BAND_Q_TILE)
    assert sub % hb == 0 and tq % sub == 0 and n_sub % tq == 0
    per = tq // hb
    n_hb = n_sub // hb
    qw, kw = n_heads * HEAD_DIM, n_kv * HEAD_DIM
    assert q_col % qw == 0 and k_col % kw == 0 and v_col % kw == 0

    def halo_specs(c):
        return [
            pl.BlockSpec((None, hb, kw), lambda b, i: (b, jnp.maximum(i * per - 1, 0), c // kw)),
            pl.BlockSpec((None, tq, kw), lambda b, i: (b, i, c // kw)),
            pl.BlockSpec((None, hb, kw), lambda b, i: (b, jnp.minimum((i + 1) * per, n_hb - 1), c // kw)),
        ]

    in_specs = [pl.BlockSpec((None, tq, qw), lambda b, i: (b, i, q_col // qw))] + halo_specs(k_col) + halo_specs(v_col)
    args = [arr] * 7
    if sink is not None:
        in_specs = [pl.BlockSpec(memory_space=pltpu.SMEM)] + in_specs
        args = [sink] + args
    out_specs = [pl.BlockSpec((None, tq, qw), lambda b, i: (b, i, 0))]
    out_shape = [jax.ShapeDtypeStruct((nb, n_sub, qw), out_dtype)]
    if emit_lse:
        out_specs.append(pl.BlockSpec((None, tq, HEAD_DIM), lambda b, i: (b, i, 0)))
        out_shape.append(jax.ShapeDtypeStruct((nb, n_sub, HEAD_DIM), F32))
    body = _band_gqa_kernel if n_kv < n_heads else _band_kernel
    return pl.pallas_call(
        functools.partial(body, tq=tq, sub=sub, hb=hb, half_w=half_w, n_sub=n_sub, slopes=slopes,
                          dist_scale=dist_scale, n_kv=n_kv, has_sink=sink is not None, emit_lse=emit_lse),
        grid=(nb, n_sub // tq),
        in_specs=in_specs,
        out_specs=out_specs,
        out_shape=out_shape,
        compiler_params=_params("parallel", "arbitrary"),
        name=f"band_attn_n{n_sub}_w{half_w}",
    )(*args)


def _merge_kernel(o0_ref, o1_ref, o2_ref, l0_ref, l1_ref, l2_ref, out_ref, ob1_ref, ob2_ref, lb1_ref, lb2_ref):
    tm = out_ref.shape[0]
    for src_ref, dst_ref in ((o1_ref, ob1_ref), (o2_ref, ob2_ref), (l1_ref, lb1_ref), (l2_ref, lb2_ref)):
        r = src_ref.shape[0]
        for p in range(r):
            for h in range(dst_ref.shape[0]):
                dst_ref[h, pl.ds(p, tm // r, stride=r), :] = src_ref[p, :, h * HEAD_DIM:(h + 1) * HEAD_DIM]
    l0, l1, l2 = l0_ref[...], lb1_ref[0], lb2_ref[0]
    mx = jnp.maximum(jnp.maximum(l0, l1), l2)
    w0, w1, w2 = jnp.exp2(l0 - mx), jnp.exp2(l1 - mx), jnp.exp2(l2 - mx)
    den = w0 + w1 + w2
    a0, a1, a2 = w0 / den, w1 / den, w2 / den
    for h in range(A_HEADS_PER_GROUP):
        sl = slice(h * HEAD_DIM, (h + 1) * HEAD_DIM)
        out = a0[:, h:h + 1] * o0_ref[:, sl] + a1[:, h:h + 1] * ob1_ref[h] + a2[:, h:h + 1] * ob2_ref[h]
        out_ref[:, sl] = out.astype(BF16)


def _merge_groups(outs, lses, bsz, seq):
    n = bsz * seq
    tm = _tile(seq, MERGE_ROW_TILE)
    nseq = seq // tm
    r1, r2 = A_GROUPS[1][1], A_GROUPS[2][1]

    def specs(w):
        return [
            pl.BlockSpec((tm, w), lambda i: (i, 0)),
            pl.BlockSpec((None, r1, tm // r1, w), lambda i: (i // nseq, 0, i % nseq, 0)),
            pl.BlockSpec((None, r2, tm // r2, w), lambda i: (i // nseq, 0, i % nseq, 0)),
        ]

    def views(xs, w):
        return [xs[0].reshape(n, w), xs[1].reshape(bsz, r1, seq // r1, w), xs[2].reshape(bsz, r2, seq // r2, w)]

    return pl.pallas_call(
        _merge_kernel,
        grid=(n // tm,),
        in_specs=specs(A_OUT) + specs(HEAD_DIM),
        out_specs=pl.BlockSpec((tm, A_OUT), lambda i: (i, 0)),
        out_shape=jax.ShapeDtypeStruct((n, A_OUT), BF16),
        scratch_shapes=[pltpu.VMEM((A_HEADS_PER_GROUP, tm, HEAD_DIM), F32),
                        pltpu.VMEM((A_HEADS_PER_GROUP, tm, HEAD_DIM), F32),
                        pltpu.VMEM((1, tm, HEAD_DIM), F32), pltpu.VMEM((1, tm, HEAD_DIM), F32)],
        compiler_params=_params("parallel"),
        name="merge_dilated",
    )(*views(outs, A_OUT), *views(lses, HEAD_DIM))


def _dense_kernel(q_ref, k_ref, v_ref, o_ref, qt_ref, st_ref, m_ref, l_ref, acc_ref, *, tq, tk, seq):
    group = B_Q_HEADS // B_KV_HEADS
    n_blk = seq // tk
    for g in range(group):
        qg = q_ref[:, g * HEAD_DIM:(g + 1) * HEAD_DIM].astype(F32)
        qt_ref[:, g * tq:(g + 1) * tq] = qg.T.astype(BF16)

    def scores(blk):
        k0 = pl.multiple_of(blk * tk, tk)
        return jnp.dot(k_ref[pl.ds(k0, tk), :], qt_ref[...], preferred_element_type=F32)

    def values(blk):
        v0 = pl.multiple_of(blk * tk, tk)
        return v_ref[pl.ds(v0, tk), :]

    def pv(v, p):
        return lax.dot_general(v, p.astype(BF16), (((0,), (0,)), ((), ())), preferred_element_type=F32)

    def write_out():
        out = acc_ref[...] / l_ref[...]
        for g in range(group):
            o_ref[:, g * HEAD_DIM:(g + 1) * HEAD_DIM] = out[:, g * tq:(g + 1) * tq].T.astype(BF16)

    first = jnp.dot(k_ref[pl.ds(0, min(tk, HEAD_DIM)), :], qt_ref[...], preferred_element_type=F32)
    m_ref[...] = jnp.max(first, axis=0, keepdims=True)
    l_ref[...] = jnp.zeros(l_ref.shape, F32)
    acc_ref[...] = jnp.zeros(acc_ref.shape, F32)

    def lazy_step(blk):
        k0 = pl.multiple_of(blk * tk, tk)
        k = k_ref[pl.ds(k0, tk), :]
        v = values(blk)
        width = LANE_CHUNK_W
        chunks = [slice(c * width, (c + 1) * width) for c in range(group * tq // width)]
        sts = [jnp.dot(k, qt_ref[:, sl], preferred_element_type=F32) for sl in chunks]
        for sl, st in zip(chunks, sts):
            p = jnp.exp2(st - m_ref[:, sl])
            l_ref[:, sl] += jnp.sum(p, axis=0, keepdims=True)
            acc_ref[:, sl] += pv(v, p)

    def lazy_body(c, carry):
        lazy_step(2 * c)
        lazy_step(2 * c + 1)
        return carry

    lax.fori_loop(0, n_blk // 2, lazy_body, 0)
    l = l_ref[...]
    sums_ok = jnp.min(jnp.where(jnp.abs(acc_ref[...]) < F32_HUGE, 1.0, 0.0))
    sums_ok = sums_ok * jnp.min(jnp.where((l > 0.0) & (l < F32_HUGE), 1.0, 0.0))
    exceeded = sums_ok < 0.5

    @pl.when(jnp.logical_not(exceeded))
    def _():
        write_out()

    @pl.when(exceeded)
    def _():
        _dense_online(scores, values, pv, write_out, st_ref, m_ref, l_ref, acc_ref, n_blk)


def _dense_online(scores, values, pv, write_out, st_ref, m_ref, l_ref, acc_ref, n_blk):
    m_ref[...] = jnp.full(m_ref.shape, -jnp.inf, F32)
    l_ref[...] = jnp.zeros(l_ref.shape, F32)
    acc_ref[...] = jnp.zeros(acc_ref.shape, F32)
    st_ref[0] = scores(0)

    def step(blk, cur, nxt):
        st_ref[nxt] = scores(jnp.minimum(blk + 1, n_blk - 1))
        st = st_ref[cur]
        m_prev = m_ref[...]
        m_new = jnp.maximum(m_prev, jnp.max(st, axis=0, keepdims=True))
        alpha = jnp.exp2(m_prev - m_new)
        p = jnp.exp2(st - m_new)
        l_ref[...] = alpha * l_ref[...] + jnp.sum(p, axis=0, keepdims=True)
        acc_ref[...] = alpha * acc_ref[...] + pv(values(blk), p)
        m_ref[...] = m_new

    def body(c, carry):
        step(2 * c, 0, 1)
        step(2 * c + 1, 1, 0)
        return carry

    lax.fori_loop(0, n_blk // 2, body, 0)
    write_out()


def _dense_attention(arr):
    bsz, seq, _ = arr.shape
    tq = _tile(seq, DENSE_Q_TILE)
    tk = _tile(seq // 2, DENSE_K_TILE)
    assert (seq // tk) % 2 == 0
    group = B_Q_HEADS // B_KV_HEADS
    qw = group * HEAD_DIM
    out = pl.pallas_call(
        functools.partial(_dense_kernel, tq=tq, tk=tk, seq=seq),
        grid=(bsz, B_KV_HEADS, seq // tq),
        in_specs=[
            pl.BlockSpec((None, tq, qw), lambda b, h, i: (b, i, h)),
            pl.BlockSpec((None, seq, HEAD_DIM), lambda b, h, i: (b, 0, B_K0 // HEAD_DIM + h)),
            pl.BlockSpec((None, seq, HEAD_DIM), lambda b, h, i: (b, 0, B_V0 // HEAD_DIM + h)),
        ],
        out_specs=pl.BlockSpec((None, tq, qw), lambda b, h, i: (b, i, h)),
        out_shape=jax.ShapeDtypeStruct((bsz, seq, B_OUT), BF16),
        scratch_shapes=[
            pltpu.VMEM((HEAD_DIM, group * tq), BF16),
            pltpu.VMEM((2, tk, group * tq), F32),
            pltpu.VMEM((1, group * tq), F32),
            pltpu.VMEM((1, group * tq), F32),
            pltpu.VMEM((HEAD_DIM, group * tq), F32),
        ],
        compiler_params=_params("parallel", "parallel", "arbitrary"),
        name="dense_attn",
    )(arr, arr, arr)
    return out.reshape(bsz * seq, B_OUT)


def _out_kernel(x_ref, g_ref, wg0_ref, wg1_ref, wg2_ref, oa_ref, ob_ref, oc_ref,
                wa_ref, wb_ref, wc_ref, wo_ref, o_ref, u_ref):
    j = pl.program_id(1)

    @pl.when(j == 0)
    def _():
        x = x_ref[...]
        u_ref[...] = _rms(x, g_ref[...]).astype(BF16)
        o_ref[...] = x

    u = u_ref[...]

    def branch(wg_ref, br_ref, w_ref):
        gate = jax.nn.sigmoid(jnp.dot(u, wg_ref[...], preferred_element_type=F32))
        return gate * jnp.dot(br_ref[...], w_ref[...], preferred_element_type=F32)

    merged = branch(wg0_ref, oa_ref, wa_ref) + branch(wg1_ref, ob_ref, wb_ref) + branch(wg2_ref, oc_ref, wc_ref)
    o_ref[...] += jnp.dot(merged.astype(BF16), wo_ref[...], preferred_element_type=F32)


def _mixer_out(x, g, w_in, o_a, o_b, o_c, w_a, w_b, w_c, w_o, layer):
    n, d = x.shape
    tm = _tile(n, OUT_ROW_TILE)
    tc = _tile(d, OUT_COL_TILE)
    assert QKV_COLS % tc == 0
    g0 = QKV_COLS // tc
    gstep = d // tc

    def gate_spec(br):
        return pl.BlockSpec((None, d, tc), lambda i, j: (layer, 0, g0 + br * gstep + j))

    return pl.pallas_call(
        _out_kernel,
        grid=(n // tm, d // tc),
        in_specs=[
            pl.BlockSpec((tm, d), lambda i, j: (i, 0)),
            pl.BlockSpec((None, 1, d), lambda i, j: (layer, 0, 0)),
            gate_spec(0), gate_spec(1), gate_spec(2),
            pl.BlockSpec((tm, A_OUT), lambda i, j: (i, 0)),
            pl.BlockSpec((tm, B_OUT), lambda i, j: (i, 0)),
            pl.BlockSpec((tm, C_OUT), lambda i, j: (i, 0)),
            pl.BlockSpec((None, A_OUT, tc), lambda i, j: (layer, 0, j)),
            pl.BlockSpec((None, B_OUT, tc), lambda i, j: (layer, 0, j)),
            pl.BlockSpec((None, C_OUT, tc), lambda i, j: (layer, 0, j)),
            pl.BlockSpec((None, tc, d), lambda i, j: (layer, j, 0)),
        ],
        out_specs=pl.BlockSpec((tm, d), lambda i, j: (i, 0)),
        out_shape=jax.ShapeDtypeStruct((n, d), F32),
        scratch_shapes=[pltpu.VMEM((tm, d), BF16)],
        compiler_params=_params("parallel", "arbitrary"),
        name="mixer_out",
    )(x, g, w_in, w_in, w_in, o_a, o_b, o_c, w_a, w_b, w_c, w_o)


def _rope_tables(seq):
    n_freq = HEAD_DIM // 4
    t = lax.broadcasted_iota(jnp.int32, (seq, HEAD_DIM), 0)
    lane = lax.broadcasted_iota(jnp.int32, (seq, HEAD_DIM), 1)
    pair = lane // 2
    pos = jnp.where(pair < n_freq, t // GRID_W, t % GRID_W).astype(F32)
    inv = ROPE_THETA ** (-(pair % n_freq).astype(F32) / n_freq)
    ang = pos * inv
    sign = jnp.where(lane % 2 == 0, -1.0, 1.0).astype(F32)
    return jnp.cos(ang), jnp.sin(ang) * sign


def _mixer(x, lw, cos, sin_signed, bsz, seq, layer):
    a0, a1, a2, bqkv, cqkv = _qkv(x, lw["norm_mix"], lw["w_in"], cos, sin_signed, lw["qk_norm_q"],
                                  lw["qk_norm_k"], bsz, seq, layer)
    a_slopes = _alibi_slopes(A_HEADS)
    outs, lses = [], []
    for gi, (arr, (window, r)) in enumerate(zip((a0, a1, a2), A_GROUPS)):
        hs = slice(gi * A_HEADS_PER_GROUP, (gi + 1) * A_HEADS_PER_GROUP)
        o, l = _band_attention(
            arr.reshape(bsz * r, seq // r, PROJ_W), q_col=0, k_col=GROUP_W, v_col=2 * GROUP_W,
            n_heads=A_HEADS_PER_GROUP, n_kv=A_HEADS_PER_GROUP, half_w=(window // 2) // r, slopes=a_slopes[hs],
            dist_scale=r, sink=None, emit_lse=True, out_dtype=F32)
        outs.append(o)
        lses.append(l)
    o_a = _merge_groups(outs, lses, bsz, seq)
    o_b = _dense_attention(bqkv.reshape(bsz, seq, PROJ_W))
    o_c, = _band_attention(
        cqkv.reshape(bsz, seq, PROJ_W), q_col=0, k_col=C_K0, v_col=C_V0, n_heads=C_Q_HEADS, n_kv=C_KV_HEADS,
        half_w=C_HALF_WINDOW, slopes=_alibi_slopes(C_Q_HEADS), dist_scale=1, sink=lw["sink_c"][layer],
        emit_lse=False, out_dtype=BF16)
    return _mixer_out(x, lw["norm_mix"], lw["w_in"], o_a, o_b, o_c.reshape(bsz * seq, C_OUT), lw["w_br_a"],
                      lw["w_br_b"], lw["w_br_c"], lw["w_out"], layer)


def _trunk(x, lw, norm_final, depth):
    bsz, seq, d = x.shape
    cos, sin_signed = _rope_tables(seq)
    xf = x.reshape(bsz * seq, d)
    for layer in range(depth):
        xf = _ffn(xf, lw["norm_ffn1"], lw["ffn1_w_in"], lw["ffn1_w_out"], layer)
        xf = _mixer(xf, lw, cos, sin_signed, bsz, seq, layer)
        last = layer == depth - 1
        xf = _ffn(xf, lw["norm_ffn2"], lw["ffn2_w_in"], lw["ffn2_w_out"], layer,
                  final_gain=norm_final.reshape(1, d) if last else None)
    return xf.reshape(bsz, seq, d)


def kernel(x_prompt, x_sample, norm_ffn1, ffn1_w_in, ffn1_w_out, norm_mix, w_in, qk_norm_q, qk_norm_k, sink_c,
           w_br_a, w_br_b, w_br_c, w_out, norm_ffn2, ffn2_w_in, ffn2_w_out, norm_final):
    depth = norm_ffn1.shape[0]
    row = lambda v: v.reshape(depth, 1, v.shape[-1])
    layers = {
        "norm_ffn1": row(norm_ffn1), "ffn1_w_in": ffn1_w_in.astype(BF16), "ffn1_w_out": ffn1_w_out.astype(BF16),
        "norm_mix": row(norm_mix), "w_in": w_in.astype(BF16),
        "qk_norm_q": row(qk_norm_q), "qk_norm_k": row(qk_norm_k), "sink_c": row(sink_c),
        "w_br_a": w_br_a.astype(BF16), "w_br_b": w_br_b.astype(BF16), "w_br_c": w_br_c.astype(BF16),
        "w_out": w_out.astype(BF16),
        "norm_ffn2": row(norm_ffn2), "ffn2_w_in": ffn2_w_in.astype(BF16), "ffn2_w_out": ffn2_w_out.astype(BF16),
    }
    assert x_prompt.shape[1:] == x_sample.shape[1:]
    n_prompt = x_prompt.shape[0]
    y = _trunk(jnp.concatenate([x_prompt, x_sample], axis=0), layers, norm_final, depth)
    return (y[:n_prompt], y[n_prompt:])
```

```python
import functools

import jax
import jax.numpy as jnp
from jax import lax
from jax.experimental import pallas as pl
from jax.experimental.pallas import tpu as pltpu

F32 = jnp.float32
BF16 = jnp.bfloat16

HEAD_DIM = 128
SCALE = HEAD_DIM ** -0.5
LOG2_E = 1.4426950408889634
Q_SCALE = SCALE * LOG2_E
NORM_EPS = 1e-6
NEG_INF = -1e30
F32_HUGE = 1e38
LANE_CHUNK_W = 256
ROPE_THETA = 10000.0
GRID_W = 64
A_GROUPS = ((128, 1), (512, 4), (2048, 16))
A_HEADS_PER_GROUP = 4
A_HEADS = 12
B_Q_HEADS = 8
B_KV_HEADS = 2
C_Q_HEADS = 8
C_KV_HEADS = 2
C_HALF_WINDOW = 128
N_BRANCHES = 3

QKV_COLS = 7680
PROJ_W = 1536
GROUP_W = 4 * HEAD_DIM
A_OUT = A_HEADS_PER_GROUP * HEAD_DIM
B_OUT = B_Q_HEADS * HEAD_DIM
C_OUT = C_Q_HEADS * HEAD_DIM
B_K0, B_V0 = B_OUT, B_OUT + B_KV_HEADS * HEAD_DIM
C_K0, C_V0 = C_OUT, C_OUT + C_KV_HEADS * HEAD_DIM

V7X_VMEM_LIMIT_BYTES = 56 * 1024 * 1024

FFN_ROW_TILE, FFN_HIDDEN_TILE = 1024, 512
PROJ_ROW_TILE = 512
OUT_ROW_TILE, OUT_COL_TILE = 512, 512
DENSE_Q_TILE, DENSE_K_TILE = 1024, 512
BAND_Q_TILE = 1024
MERGE_ROW_TILE = 1024


def _tile(n, pref):
    t = min(pref, n)
    while n % t:
        t //= 2
    return t


def _params(*sem):
    return pltpu.CompilerParams(dimension_semantics=sem, vmem_limit_bytes=V7X_VMEM_LIMIT_BYTES)


def _rms(x, g):
    ms = jnp.mean(x * x, axis=-1, keepdims=True)
    return x * lax.rsqrt(ms + NORM_EPS) * g


def _alibi_slopes(n):
    return tuple(2.0 ** (-8.0 * i / n) for i in range(1, n + 1))


def _ffn_kernel(x_ref, g_ref, wa_ref, wb_ref, wo_ref, *rest, final_norm):
    o_ref, xn_ref = rest[-2:]
    j = pl.program_id(1)

    @pl.when(j == 0)
    def _():
        x = x_ref[...]
        xn_ref[...] = _rms(x, g_ref[...]).astype(BF16)
        o_ref[...] = x

    xn = xn_ref[...]
    a = jnp.dot(xn, wa_ref[...], preferred_element_type=F32)
    b = jnp.dot(xn, wb_ref[...], preferred_element_type=F32)
    h = (0.5 * a * jax.nn.sigmoid(a) * b).astype(BF16)
    o_ref[...] += jnp.dot(h, wo_ref[...], preferred_element_type=F32)

    if final_norm:
        @pl.when(j == pl.num_programs(1) - 1)
        def _():
            o_ref[...] = _rms(o_ref[...], rest[0][...])


def _ffn(x, g, w_in, w_out, layer, final_gain=None):
    n, d = x.shape
    d_ff = w_out.shape[1]
    tm = _tile(n, FFN_ROW_TILE)
    tf = _tile(d_ff, FFN_HIDDEN_TILE)
    nf = d_ff // tf
    in_specs = [
        pl.BlockSpec((tm, d), lambda i, j: (i, 0)),
        pl.BlockSpec((None, 1, d), lambda i, j: (layer, 0, 0)),
        pl.BlockSpec((None, d, tf), lambda i, j: (layer, 0, j)),
        pl.BlockSpec((None, d, tf), lambda i, j: (layer, 0, j + nf)),
        pl.BlockSpec((None, tf, d), lambda i, j: (layer, j, 0)),
    ]
    args = [x, g, w_in, w_in, w_out]
    if final_gain is not None:
        in_specs.append(pl.BlockSpec((1, d), lambda i, j: (0, 0)))
        args.append(final_gain)
    return pl.pallas_call(
        functools.partial(_ffn_kernel, final_norm=final_gain is not None),
        grid=(n // tm, nf),
        in_specs=in_specs,
        out_specs=pl.BlockSpec((tm, d), lambda i, j: (i, 0)),
        out_shape=jax.ShapeDtypeStruct((n, d), F32),
        scratch_shapes=[pltpu.VMEM((tm, d), BF16)],
        compiler_params=_params("parallel", "arbitrary"),
        name="ffn",
    )(*args)


def _norm_rope(y, gain, cos, sin_signed):
    yn = _rms(y, gain)
    lane = lax.broadcasted_iota(jnp.int32, yn.shape, 1)
    partner = jnp.where((lane & 1) == 0, pltpu.roll(yn, HEAD_DIM - 1, 1), pltpu.roll(yn, 1, 1))
    return yn * cos + partner * sin_signed


def _qkv_kernel(x_ref, g_ref, w_ref, cos_ref, sin_ref, qg_ref, kg_ref,
                a0_ref, a1_ref, a2_ref, b_ref, c_ref, u_ref, split_ref):
    j = pl.program_id(1)
    tm = x_ref.shape[0]

    @pl.when(j == 0)
    def _():
        u_ref[...] = _rms(x_ref[...], g_ref[...]).astype(BF16)

    def proj(c0, width):
        return jnp.dot(u_ref[...], w_ref[:, c0:c0 + width], preferred_element_type=F32)

    @pl.when(j < 3)
    def _():
        sc = jnp.where(j == 0, Q_SCALE, 1.0).astype(F32)
        a0_ref[...] = (proj(0, GROUP_W) * sc).astype(BF16)
        for gi, dst_ref in ((1, a1_ref), (2, a2_ref)):
            r = A_GROUPS[gi][1]
            y = proj(gi * GROUP_W, GROUP_W) * sc
            for h in range(A_HEADS_PER_GROUP):
                split_ref[(gi - 1) * A_HEADS_PER_GROUP + h] = y[:, h * HEAD_DIM:(h + 1) * HEAD_DIM]
            for p in range(r):
                for h in range(A_HEADS_PER_GROUP):
                    rows = split_ref[(gi - 1) * A_HEADS_PER_GROUP + h, pl.ds(p, tm // r, stride=r), :]
                    dst_ref[p, :, h * HEAD_DIM:(h + 1) * HEAD_DIM] = rows.astype(BF16)

    @pl.when(j == 3)
    def _():
        pair = 2 * HEAD_DIM
        for c0 in range(0, B_V0, pair):
            acc = proj(c0, pair)
            for h0 in range(0, pair, HEAD_DIM):
                is_q = c0 + h0 < B_K0
                y = _norm_rope(acc[:, h0:h0 + HEAD_DIM], (qg_ref if is_q else kg_ref)[...], cos_ref[...],
                               sin_ref[...])
                b_ref[:, c0 + h0:c0 + h0 + HEAD_DIM] = ((y * Q_SCALE) if is_q else y).astype(BF16)
        b_ref[:, B_V0:] = proj(B_V0, PROJ_W - B_V0).astype(BF16)

    @pl.when(j == 4)
    def _():
        for c0 in range(0, C_OUT, GROUP_W):
            c_ref[:, c0:c0 + GROUP_W] = (proj(c0, GROUP_W) * Q_SCALE).astype(BF16)
        c_ref[:, C_OUT:] = proj(C_OUT, PROJ_W - C_OUT).astype(BF16)


def _qkv(x, g, w_in, cos, sin_signed, qg, kg, bsz, seq, layer):
    n, d = x.shape
    tm = _tile(seq, PROJ_ROW_TILE)
    nseq = seq // tm
    r1, r2 = A_GROUPS[1][1], A_GROUPS[2][1]
    assert tm % (16 * r2) == 0
    a_part = lambda i, j: jnp.minimum(j, 2)

    def split_spec(r):
        return pl.BlockSpec((None, r, tm // r, GROUP_W), lambda i, j: (i // nseq, 0, i % nseq, a_part(i, j)))

    return pl.pallas_call(
        _qkv_kernel,
        grid=(n // tm, QKV_COLS // PROJ_W),
        in_specs=[
            pl.BlockSpec((tm, d), lambda i, j: (i, 0)),
            pl.BlockSpec((None, 1, d), lambda i, j: (layer, 0, 0)),
            pl.BlockSpec((None, d, PROJ_W), lambda i, j: (layer, 0, j)),
            pl.BlockSpec((tm, HEAD_DIM), lambda i, j: (i % nseq, 0)),
            pl.BlockSpec((tm, HEAD_DIM), lambda i, j: (i % nseq, 0)),
            pl.BlockSpec((None, 1, HEAD_DIM), lambda i, j: (layer, 0, 0)),
            pl.BlockSpec((None, 1, HEAD_DIM), lambda i, j: (layer, 0, 0)),
        ],
        out_specs=[
            pl.BlockSpec((tm, GROUP_W), lambda i, j: (i, a_part(i, j))),
            split_spec(r1),
            split_spec(r2),
            pl.BlockSpec((tm, PROJ_W), lambda i, j: (i, 0)),
            pl.BlockSpec((tm, PROJ_W), lambda i, j: (i, 0)),
            pl.BlockSpec((tm, d), lambda i, j: (i, 0)),
        ],
        out_shape=[
            jax.ShapeDtypeStruct((n, PROJ_W), BF16),
            jax.ShapeDtypeStruct((bsz, r1, seq // r1, PROJ_W), BF16),
            jax.ShapeDtypeStruct((bsz, r2, seq // r2, PROJ_W), BF16),
            jax.ShapeDtypeStruct((n, PROJ_W), BF16),
            jax.ShapeDtypeStruct((n, PROJ_W), BF16),
            jax.ShapeDtypeStruct((n, d), BF16),
        ],
        scratch_shapes=[pltpu.VMEM(((PROJ_W - GROUP_W) // HEAD_DIM, tm, HEAD_DIM), F32)],
        compiler_params=_params("parallel", "arbitrary"),
        name="qkv_proj",
    )(x, g, w_in, cos, sin_signed, qg, kg)


def _band_kernel(*refs, tq, sub, hb, half_w, n_sub, slopes, dist_scale, n_kv, has_sink, emit_lse):
    if has_sink:
        sink_ref, refs = refs[0], refs[1:]
    q_ref, kp_ref, km_ref, kn_ref, vp_ref, vm_ref, vn_ref, o_ref = refs[:8]
    n_heads = len(slopes)
    group = n_heads // n_kv
    qi = pl.program_id(1)
    span = sub + 2 * hb
    row = lax.broadcasted_iota(jnp.int32, (sub, span), 0)
    col = lax.broadcasted_iota(jnp.int32, (sub, span), 1)
    arel = jnp.abs(col - hb - row)
    in_band = arel <= half_w
    dist = (arel * dist_scale).astype(F32)
    biases = [jnp.where(in_band, (-slopes[h] * LOG2_E) * dist, NEG_INF) for h in range(n_heads)]
    kcat, vcat = [], []
    for kv in range(n_kv):
        ksl = slice(kv * HEAD_DIM, (kv + 1) * HEAD_DIM)
        kcat.append(jnp.concatenate([kp_ref[:, ksl], km_ref[:, ksl], kn_ref[:, ksl]], axis=0))
        vcat.append(jnp.concatenate([vp_ref[:, ksl], vm_ref[:, ksl], vn_ref[:, ksl]], axis=0))
    if emit_lse:
        lane = lax.broadcasted_iota(jnp.int32, (sub, HEAD_DIM), 1)
    for s in range(tq // sub):
        rows = slice(s * sub, (s + 1) * sub)
        kpos = qi * tq + (s * sub - hb) + col
        in_seq = (kpos >= 0) & (kpos < n_sub)
        lse_tile = jnp.zeros((sub, HEAD_DIM), F32) if emit_lse else None
        for h in range(n_heads):
            sl = slice(h * HEAD_DIM, (h + 1) * HEAD_DIM)
            ks = kcat[h // group][s * sub:s * sub + span]
            vs = vcat[h // group][s * sub:s * sub + span]
            sc = lax.dot_general(q_ref[rows, sl], ks, (((1,), (1,)), ((), ())), preferred_element_type=F32)
            sc = jnp.where(in_seq, sc + biases[h], NEG_INF)
            m = jnp.max(sc, axis=-1, keepdims=True)
            if has_sink:
                sk = sink_ref[0, h] * LOG2_E
                m = jnp.maximum(m, sk)
            p = jnp.exp2(sc - m)
            den = jnp.sum(p, axis=-1, keepdims=True)
            if has_sink:
                den = den + jnp.exp2(sk - m)
            o = jnp.dot(p.astype(BF16), vs, preferred_element_type=F32) * (1.0 / den)
            o_ref[rows, sl] = o.astype(o_ref.dtype)
            if emit_lse:
                lse_tile = jnp.where(lane == h, m + jnp.log2(den), lse_tile)
        if emit_lse:
            refs[8][rows, :] = lse_tile


def _band_gqa_kernel(*refs, tq, sub, hb, half_w, n_sub, slopes, dist_scale, n_kv, has_sink, emit_lse):
    assert not emit_lse
    if has_sink:
        sink_ref, refs = refs[0], refs[1:]
    q_ref, kp_ref, km_ref, kn_ref, vp_ref, vm_ref, vn_ref, o_ref = refs
    n_heads = len(slopes)
    group = n_heads // n_kv
    qi = pl.program_id(1)
    span = sub + 2 * hb
    head_sl = lambda h: slice(h * HEAD_DIM, (h + 1) * HEAD_DIM)
    key = lax.broadcasted_iota(jnp.int32, (span, sub), 0)
    qry = lax.broadcasted_iota(jnp.int32, (span, sub), 1)
    arel = jnp.abs(key - hb - qry)
    in_band = arel <= half_w
    dist = (arel * dist_scale).astype(F32)
    key_wide = lax.broadcasted_iota(jnp.int32, (span, group * sub), 0)
    biases, sinks, kcat, vcat = [], [], [], []
    for kv in range(n_kv):
        heads = range(kv * group, (kv + 1) * group)
        biases.append(jnp.concatenate(
            [jnp.where(in_band, (-slopes[h] * LOG2_E) * dist, NEG_INF) for h in heads], axis=1))
        if has_sink:
            sinks.append(jnp.concatenate(
                [jnp.full((1, sub), sink_ref[0, h] * LOG2_E, F32) for h in heads], axis=1))
        ksl = head_sl(kv)
        kcat.append(jnp.concatenate([kp_ref[:, ksl], km_ref[:, ksl], kn_ref[:, ksl]], axis=0))
        vcat.append(jnp.concatenate([vp_ref[:, ksl], vm_ref[:, ksl], vn_ref[:, ksl]], axis=0))
    for s in range(tq // sub):
        rows = slice(s * sub, (s + 1) * sub)
        kpos = qi * tq + (s * sub - hb) + key_wide
        in_seq = (kpos >= 0) & (kpos < n_sub)
        for kv in range(n_kv):
            heads = range(kv * group, (kv + 1) * group)
            ks = kcat[kv][s * sub:s * sub + span]
            vs = vcat[kv][s * sub:s * sub + span]
            q = jnp.concatenate([q_ref[rows, head_sl(h)] for h in heads], axis=0)
            st = lax.dot_general(ks, q, (((1,), (1,)), ((), ())), preferred_element_type=F32)
            st = jnp.where(in_seq, st + biases[kv], NEG_INF)
            m = jnp.max(st, axis=0, keepdims=True)
            if has_sink:
                m = jnp.maximum(m, sinks[kv])
            p = jnp.exp2(st - m)
            den = jnp.sum(p, axis=0, keepdims=True)
            if has_sink:
                den = den + jnp.exp2(sinks[kv] - m)
            ot = lax.dot_general(vs, p.astype(BF16), (((0,), (0,)), ((), ())), preferred_element_type=F32)
            ot = ot * (1.0 / den)
            for gi, h in enumerate(heads):
                o_ref[rows, head_sl(h)] = ot[:, gi * sub:(gi + 1) * sub].T.astype(o_ref.dtype)


def _band_attention(arr, *, q_col, k_col, v_col, n_heads, n_kv, half_w, slopes, dist_scale, sink, emit_lse,
                    out_dtype):
    nb, n_sub, _ = arr.shape
    hb = half_w
    sub = min(HEAD_DIM, n_sub)
    tq = _tile(n_sub, BAND_Q_TILE)
    assert sub % hb == 0 and tq % sub == 0 and n_sub % tq == 0
    per = tq // hb
    n_hb = n_sub // hb
    qw, kw = n_heads * HEAD_DIM, n_kv * HEAD_DIM
    assert q_col % qw == 0 and k_col % kw == 0 and v_col % kw == 0

    def halo_specs(c):
        return [
            pl.BlockSpec((None, hb, kw), lambda b, i: (b, jnp.maximum(i * per - 1, 0), c // kw)),
            pl.BlockSpec((None, tq, kw), lambda b, i: (b, i, c // kw)),
            pl.BlockSpec((None, hb, kw), lambda b, i: (b, jnp.minimum((i + 1) * per, n_hb - 1), c // kw)),
        ]

    in_specs = [pl.BlockSpec((None, tq, qw), lambda b, i: (b, i, q_col // qw))] + halo_specs(k_col) + halo_specs(v_col)
    args = [arr] * 7
    if sink is not None:
        in_specs = [pl.BlockSpec(memory_space=pltpu.SMEM)] + in_specs
        args = [sink] + args
    out_specs = [pl.BlockSpec((None, tq, qw), lambda b, i: (b, i, 0))]
    out_shape = [jax.ShapeDtypeStruct((nb, n_sub, qw), out_dtype)]
    if emit_lse:
        out_specs.append(pl.BlockSpec((None, tq, HEAD_DIM), lambda b, i: (b, i, 0)))
        out_shape.append(jax.ShapeDtypeStruct((nb, n_sub, HEAD_DIM), F32))
    body = _band_gqa_kernel if n_kv < n_heads else _band_kernel
    return pl.pallas_call(
        functools.partial(body, tq=tq, sub=sub, hb=hb, half_w=half_w, n_sub=n_sub, slopes=slopes,
                          dist_scale=dist_scale, n_kv=n_kv, has_sink=sink is not None, emit_lse=emit_lse),
        grid=(nb, n_sub // tq),
        in_specs=in_specs,
        out_specs=out_specs,
        out_shape=out_shape,
        compiler_params=_params("parallel", "arbitrary"),
        name=f"band_attn_n{n_sub}_w{half_w}",
    )(*args)


def _merge_kernel(o0_ref, o1_ref, o2_ref, l0_ref, l1_ref, l2_ref, out_ref, ob1_ref, ob2_ref, lb1_ref, lb2_ref):
    tm = out_ref.shape[0]
    for src_ref, dst_ref in ((o1_ref, ob1_ref), (o2_ref, ob2_ref), (l1_ref, lb1_ref), (l2_ref, lb2_ref)):
        r = src_ref.shape[0]
        for p in range(r):
            for h in range(dst_ref.shape[0]):
                dst_ref[h, pl.ds(p, tm // r, stride=r), :] = src_ref[p, :, h * HEAD_DIM:(h + 1) * HEAD_DIM]
    l0, l1, l2 = l0_ref[...], lb1_ref[0], lb2_ref[0]
    mx = jnp.maximum(jnp.maximum(l0, l1), l2)
    w0, w1, w2 = jnp.exp2(l0 - mx), jnp.exp2(l1 - mx), jnp.exp2(l2 - mx)
    den = w0 + w1 + w2
    a0, a1, a2 = w0 / den, w1 / den, w2 / den
    for h in range(A_HEADS_PER_GROUP):
        sl = slice(h * HEAD_DIM, (h + 1) * HEAD_DIM)
        out = a0[:, h:h + 1] * o0_ref[:, sl] + a1[:, h:h + 1] * ob1_ref[h] + a2[:, h:h + 1] * ob2_ref[h]
        out_ref[:, sl] = out.astype(BF16)


def _merge_groups(outs, lses, bsz, seq):
    n = bsz * seq
    tm = _tile(seq, MERGE_ROW_TILE)
    nseq = seq // tm
    r1, r2 = A_GROUPS[1][1], A_GROUPS[2][1]

    def specs(w):
        return [
            pl.BlockSpec((tm, w), lambda i: (i, 0)),
            pl.BlockSpec((None, r1, tm // r1, w), lambda i: (i // nseq, 0, i % nseq, 0)),
            pl.BlockSpec((None, r2, tm // r2, w), lambda i: (i // nseq, 0, i % nseq, 0)),
        ]

    def views(xs, w):
        return [xs[0].reshape(n, w), xs[1].reshape(bsz, r1, seq // r1, w), xs[2].reshape(bsz, r2, seq // r2, w)]

    return pl.pallas_call(
        _merge_kernel,
        grid=(n // tm,),
        in_specs=specs(A_OUT) + specs(HEAD_DIM),
        out_specs=pl.BlockSpec((tm, A_OUT), lambda i: (i, 0)),
        out_shape=jax.ShapeDtypeStruct((n, A_OUT), BF16),
        scratch_shapes=[pltpu.VMEM((A_HEADS_PER_GROUP, tm, HEAD_DIM), F32),
                        pltpu.VMEM((A_HEADS_PER_GROUP, tm, HEAD_DIM), F32),
                        pltpu.VMEM((1, tm, HEAD_DIM), F32), pltpu.VMEM((1, tm, HEAD_DIM), F32)],
        compiler_params=_params("parallel"),
        name="merge_dilated",
    )(*views(outs, A_OUT), *views(lses, HEAD_DIM))


def _dense_kernel(q_ref, k_ref, v_ref, o_ref, qt_ref, st_ref, m_ref, l_ref, acc_ref, *, tq, tk, seq):
    group = B_Q_HEADS // B_KV_HEADS
    n_blk = seq // tk
    for g in range(group):
        qg = q_ref[:, g * HEAD_DIM:(g + 1) * HEAD_DIM].astype(F32)
        qt_ref[:, g * tq:(g + 1) * tq] = qg.T.astype(BF16)

    def scores(blk):
        k0 = pl.multiple_of(blk * tk, tk)
        return jnp.dot(k_ref[pl.ds(k0, tk), :], qt_ref[...], preferred_element_type=F32)

    def values(blk):
        v0 = pl.multiple_of(blk * tk, tk)
        return v_ref[pl.ds(v0, tk), :]

    def pv(v, p):
        return lax.dot_general(v, p.astype(BF16), (((0,), (0,)), ((), ())), preferred_element_type=F32)

    def write_out():
        out = acc_ref[...] / l_ref[...]
        for g in range(group):
            o_ref[:, g * HEAD_DIM:(g + 1) * HEAD_DIM] = out[:, g * tq:(g + 1) * tq].T.astype(BF16)

    first = jnp.dot(k_ref[pl.ds(0, min(tk, HEAD_DIM)), :], qt_ref[...], preferred_element_type=F32)
    m_ref[...] = jnp.max(first, axis=0, keepdims=True)
    l_ref[...] = jnp.zeros(l_ref.shape, F32)
    acc_ref[...] = jnp.zeros(acc_ref.shape, F32)

    def lazy_step(blk):
        k0 = pl.multiple_of(blk * tk, tk)
        k = k_ref[pl.ds(k0, tk), :]
        v = values(blk)
        width = LANE_CHUNK_W
        chunks = [slice(c * width, (c + 1) * width) for c in range(group * tq // width)]
        sts = [jnp.dot(k, qt_ref[:, sl], preferred_element_type=F32) for sl in chunks]
        for sl, st in zip(chunks, sts):
            p = jnp.exp2(st - m_ref[:, sl])
            l_ref[:, sl] += jnp.sum(p, axis=0, keepdims=True)
            acc_ref[:, sl] += pv(v, p)

    def lazy_body(c, carry):
        lazy_step(2 * c)
        lazy_step(2 * c + 1)
        return carry

    lax.fori_loop(0, n_blk // 2, lazy_body, 0)
    l = l_ref[...]
    sums_ok = jnp.min(jnp.where(jnp.abs(acc_ref[...]) < F32_HUGE, 1.0, 0.0))
    sums_ok = sums_ok * jnp.min(jnp.where((l > 0.0) & (l < F32_HUGE), 1.0, 0.0))
    exceeded = sums_ok < 0.5

    @pl.when(jnp.logical_not(exceeded))
    def _():
        write_out()

    @pl.when(exceeded)
    def _():
        _dense_online(scores, values, pv, write_out, st_ref, m_ref, l_ref, acc_ref, n_blk)


def _dense_online(scores, values, pv, write_out, st_ref, m_ref, l_ref, acc_ref, n_blk):
    m_ref[...] = jnp.full(m_ref.shape, -jnp.inf, F32)
    l_ref[...] = jnp.zeros(l_ref.shape, F32)
    acc_ref[...] = jnp.zeros(acc_ref.shape, F32)
    st_ref[0] = scores(0)

    def step(blk, cur, nxt):
        st_ref[nxt] = scores(jnp.minimum(blk + 1, n_blk - 1))
        st = st_ref[cur]
        m_prev = m_ref[...]
        m_new = jnp.maximum(m_prev, jnp.max(st, axis=0, keepdims=True))
        alpha = jnp.exp2(m_prev - m_new)
        p = jnp.exp2(st - m_new)
        l_ref[...] = alpha * l_ref[...] + jnp.sum(p, axis=0, keepdims=True)
        acc_ref[...] = alpha * acc_ref[...] + pv(values(blk), p)
        m_ref[...] = m_new

    def body(c, carry):
        step(2 * c, 0, 1)
        step(2 * c + 1, 1, 0)
        return carry

    lax.fori_loop(0, n_blk // 2, body, 0)
    write_out()


def _dense_attention(arr):
    bsz, seq, _ = arr.shape
    tq = _tile(seq, DENSE_Q_TILE)
    tk = _tile(seq // 2, DENSE_K_TILE)
    assert (seq // tk) % 2 == 0
    group = B_Q_HEADS // B_KV_HEADS
    qw = group * HEAD_DIM
    out = pl.pallas_call(
        functools.partial(_dense_kernel, tq=tq, tk=tk, seq=seq),
        grid=(bsz, B_KV_HEADS, seq // tq),
        in_specs=[
            pl.BlockSpec((None, tq, qw), lambda b, h, i: (b, i, h)),
            pl.BlockSpec((None, seq, HEAD_DIM), lambda b, h, i: (b, 0, B_K0 // HEAD_DIM + h)),
            pl.BlockSpec((None, seq, HEAD_DIM), lambda b, h, i: (b, 0, B_V0 // HEAD_DIM + h)),
        ],
        out_specs=pl.BlockSpec((None, tq, qw), lambda b, h, i: (b, i, h)),
        out_shape=jax.ShapeDtypeStruct((bsz, seq, B_OUT), BF16),
        scratch_shapes=[
            pltpu.VMEM((HEAD_DIM, group * tq), BF16),
            pltpu.VMEM((2, tk, group * tq), F32),
            pltpu.VMEM((1, group * tq), F32),
            pltpu.VMEM((1, group * tq), F32),
            pltpu.VMEM((HEAD_DIM, group * tq), F32),
        ],
        compiler_params=_params("parallel", "parallel", "arbitrary"),
        name="dense_attn",
    )(arr, arr, arr)
    return out.reshape(bsz * seq, B_OUT)


def _out_kernel(x_ref, u_ref, wg0_ref, wg1_ref, wg2_ref, oa_ref, ob_ref, oc_ref,
                wa_ref, wb_ref, wc_ref, wo_ref, o_ref):
    j = pl.program_id(1)

    @pl.when(j == 0)
    def _():
        o_ref[...] = x_ref[...]

    u = u_ref[...]

    def branch(wg_ref, br_ref, w_ref):
        gate = jax.nn.sigmoid(jnp.dot(u, wg_ref[...], preferred_element_type=F32))
        return gate * jnp.dot(br_ref[...], w_ref[...], preferred_element_type=F32)

    merged = branch(wg0_ref, oa_ref, wa_ref) + branch(wg1_ref, ob_ref, wb_ref) + branch(wg2_ref, oc_ref, wc_ref)
    o_ref[...] += jnp.dot(merged.astype(BF16), wo_ref[...], preferred_element_type=F32)


def _mixer_out(x, u, w_in, o_a, o_b, o_c, w_a, w_b, w_c, w_o, layer):
    n, d = x.shape
    tm = _tile(n, OUT_ROW_TILE)
    tc = _tile(d, OUT_COL_TILE)
    assert QKV_COLS % tc == 0
    g0 = QKV_COLS // tc
    gstep = d // tc

    def gate_spec(br):
        return pl.BlockSpec((None, d, tc), lambda i, j: (layer, 0, g0 + br * gstep + j))

    return pl.pallas_call(
        _out_kernel,
        grid=(n // tm, d // tc),
        in_specs=[
            pl.BlockSpec((tm, d), lambda i, j: (i, 0)),
            pl.BlockSpec((tm, d), lambda i, j: (i, 0)),
            gate_spec(0), gate_spec(1), gate_spec(2),
            pl.BlockSpec((tm, A_OUT), lambda i, j: (i, 0)),
            pl.BlockSpec((tm, B_OUT), lambda i, j: (i, 0)),
            pl.BlockSpec((tm, C_OUT), lambda i, j: (i, 0)),
            pl.BlockSpec((None, A_OUT, tc), lambda i, j: (layer, 0, j)),
            pl.BlockSpec((None, B_OUT, tc), lambda i, j: (layer, 0, j)),
            pl.BlockSpec((None, C_OUT, tc), lambda i, j: (layer, 0, j)),
            pl.BlockSpec((None, tc, d), lambda i, j: (layer, j, 0)),
        ],
        out_specs=pl.BlockSpec((tm, d), lambda i, j: (i, 0)),
        out_shape=jax.ShapeDtypeStruct((n, d), F32),
        compiler_params=_params("parallel", "arbitrary"),
        name="mixer_out",
    )(x, u, w_in, w_in, w_in, o_a, o_b, o_c, w_a, w_b, w_c, w_o)


def _rope_tables(seq):
    n_freq = HEAD_DIM // 4
    t = lax.broadcasted_iota(jnp.int32, (seq, HEAD_DIM), 0)
    lane = lax.broadcasted_iota(jnp.int32, (seq, HEAD_DIM), 1)
    pair = lane // 2
    pos = jnp.where(pair < n_freq, t // GRID_W, t % GRID_W).astype(F32)
    inv = ROPE_THETA ** (-(pair % n_freq).astype(F32) / n_freq)
    ang = pos * inv
    sign = jnp.where(lane % 2 == 0, -1.0, 1.0).astype(F32)
    return jnp.cos(ang), jnp.sin(ang) * sign


def _mixer(x, lw, cos, sin_signed, bsz, seq, layer):
    a0, a1, a2, bqkv, cqkv, u = _qkv(x, lw["norm_mix"], lw["w_in"], cos, sin_signed, lw["qk_norm_q"],
                                     lw["qk_norm_k"], bsz, seq, layer)
    a_slopes = _alibi_slopes(A_HEADS)
    outs, lses = [], []
    for gi, (arr, (window, r)) in enumerate(zip((a0, a1, a2), A_GROUPS)):
        hs = slice(gi * A_HEADS_PER_GROUP, (gi + 1) * A_HEADS_PER_GROUP)
        o, l = _band_attention(
            arr.reshape(bsz * r, seq // r, PROJ_W), q_col=0, k_col=GROUP_W, v_col=2 * GROUP_W,
            n_heads=A_HEADS_PER_GROUP, n_kv=A_HEADS_PER_GROUP, half_w=(window // 2) // r, slopes=a_slopes[hs],
            dist_scale=r, sink=None, emit_lse=True, out_dtype=F32)
        outs.append(o)
        lses.append(l)
    o_a = _merge_groups(outs, lses, bsz, seq)
    o_b = _dense_attention(bqkv.reshape(bsz, seq, PROJ_W))
    o_c, = _band_attention(
        cqkv.reshape(bsz, seq, PROJ_W), q_col=0, k_col=C_K0, v_col=C_V0, n_heads=C_Q_HEADS, n_kv=C_KV_HEADS,
        half_w=C_HALF_WINDOW, slopes=_alibi_slopes(C_Q_HEADS), dist_scale=1, sink=lw["sink_c"][layer],
        emit_lse=False, out_dtype=BF16)
    return _mixer_out(x, u, lw["w_in"], o_a, o_b, o_c.reshape(bsz * seq, C_OUT), lw["w_br_a"],
                      lw["w_br_b"], lw["w_br_c"], lw["w_out"], layer)


def _trunk(x, lw, norm_final, depth):
    bsz, seq, d = x.shape
    cos, sin_signed = _rope_tables(seq)
    xf = x.reshape(bsz * seq, d)
    for layer in range(depth):
        xf = _ffn(xf, lw["norm_ffn1"], lw["ffn1_w_in"], lw["ffn1_w_out"], layer)
        xf = _mixer(xf, lw, cos, sin_signed, bsz, seq, layer)
        last = layer == depth - 1
        xf = _ffn(xf, lw["norm_ffn2"], lw["ffn2_w_in"], lw["ffn2_w_out"], layer,
                  final_gain=norm_final.reshape(1, d) if last else None)
    return xf.reshape(bsz, seq, d)


def kernel(x_prompt, x_sample, norm_ffn1, ffn1_w_in, ffn1_w_out, norm_mix, w_in, qk_norm_q, qk_norm_k, sink_c,
           w_br_a, w_br_b, w_br_c, w_out, norm_ffn2, ffn2_w_in, ffn2_w_out, norm_final):
    depth = norm_ffn1.shape[0]
    row = lambda v: v.reshape(depth, 1, v.shape[-1])
    layers = {
        "norm_ffn1": row(norm_ffn1), "ffn1_w_in": ffn1_w_in.astype(BF16), "ffn1_w_out": ffn1_w_out.astype(BF16),
        "norm_mix": row(norm_mix), "w_in": w_in.astype(BF16),
        "qk_norm_q": row(qk_norm_q), "qk_norm_k": row(qk_norm_k), "sink_c": row(sink_c),
        "w_br_a": w_br_a.astype(BF16), "w_br_b": w_br_b.astype(BF16), "w_br_c": w_br_c.astype(BF16),
        "w_out": w_out.astype(BF16),
        "norm_ffn2": row(norm_ffn2), "ffn2_w_in": ffn2_w_in.astype(BF16), "ffn2_w_out": ffn2_w_out.astype(BF16),
    }
    assert x_prompt.shape[1:] == x_sample.shape[1:]
    n_prompt = x_prompt.shape[0]
    y = _trunk(jnp.concatenate([x_prompt, x_sample], axis=0), layers, norm_final, depth)
    return (y[:n_prompt], y[n_prompt:])
```

```python
import functools

import jax
import jax.numpy as jnp
from jax import lax
from jax.experimental import pallas as pl
from jax.experimental.pallas import tpu as pltpu

F32 = jnp.float32
BF16 = jnp.bfloat16

HEAD_DIM = 128
SCALE = HEAD_DIM ** -0.5
LOG2_E = 1.4426950408889634
Q_SCALE = SCALE * LOG2_E
NORM_EPS = 1e-6
NEG_INF = -1e30
F32_HUGE = 1e38
LANE_CHUNK_W = 256
ROPE_THETA = 10000.0
GRID_W = 64
A_GROUPS = ((128, 1), (512, 4), (2048, 16))
A_HEADS_PER_GROUP = 4
A_HEADS = 12
B_Q_HEADS = 8
B_KV_HEADS = 2
C_Q_HEADS = 8
C_KV_HEADS = 2
C_HALF_WINDOW = 128
N_BRANCHES = 3

QKV_COLS = 7680
PROJ_W = 1536
GROUP_W = 4 * HEAD_DIM
A_OUT = A_HEADS_PER_GROUP * HEAD_DIM
B_OUT = B_Q_HEADS * HEAD_DIM
C_OUT = C_Q_HEADS * HEAD_DIM
B_K0, B_V0 = B_OUT, B_OUT + B_KV_HEADS * HEAD_DIM
C_K0, C_V0 = C_OUT, C_OUT + C_KV_HEADS * HEAD_DIM

V7X_VMEM_LIMIT_BYTES = 56 * 1024 * 1024

FFN_ROW_TILE, FFN_HIDDEN_TILE = 1024, 512
FFN_NORM_CHUNKS = 4
PROJ_ROW_TILE = 512
OUT_ROW_TILE, OUT_COL_TILE = 512, 512
DENSE_Q_TILE, DENSE_K_TILE = 1024, 512
BAND_Q_TILE = 1024
MERGE_ROW_TILE = 1024


def _tile(n, pref):
    t = min(pref, n)
    while n % t:
        t //= 2
    return t


def _params(*sem):
    return pltpu.CompilerParams(dimension_semantics=sem, vmem_limit_bytes=V7X_VMEM_LIMIT_BYTES)


def _rms(x, g):
    ms = jnp.mean(x * x, axis=-1, keepdims=True)
    return x * lax.rsqrt(ms + NORM_EPS) * g


def _alibi_slopes(n):
    return tuple(2.0 ** (-8.0 * i / n) for i in range(1, n + 1))


def _ffn_kernel(x_ref, g_ref, wa_ref, wb_ref, wo_ref, *rest, final_norm):
    o_ref, xn_ref = rest[-2:]
    j = pl.program_id(1)

    def half_swiglu(xn):
        a = jnp.dot(xn, wa_ref[...], preferred_element_type=F32)
        b = jnp.dot(xn, wb_ref[...], preferred_element_type=F32)
        h = (0.5 * a * jax.nn.sigmoid(a) * b).astype(BF16)
        return jnp.dot(h, wo_ref[...], preferred_element_type=F32)

    @pl.when(j == 0)
    def _():
        rows = x_ref.shape[0] // FFN_NORM_CHUNKS
        for c in range(FFN_NORM_CHUNKS):
            sl = slice(c * rows, (c + 1) * rows)
            x = x_ref[sl, :]
            xn = _rms(x, g_ref[...]).astype(BF16)
            xn_ref[sl, :] = xn
            o_ref[sl, :] = x + half_swiglu(xn)

    @pl.when(j > 0)
    def _():
        o_ref[...] += half_swiglu(xn_ref[...])

    if final_norm:
        @pl.when(j == pl.num_programs(1) - 1)
        def _():
            o_ref[...] = _rms(o_ref[...], rest[0][...])


def _ffn(x, g, w_in, w_out, layer, final_gain=None):
    n, d = x.shape
    d_ff = w_out.shape[1]
    tm = _tile(n, FFN_ROW_TILE)
    tf = _tile(d_ff, FFN_HIDDEN_TILE)
    nf = d_ff // tf
    in_specs = [
        pl.BlockSpec((tm, d), lambda i, j: (i, 0)),
        pl.BlockSpec((None, 1, d), lambda i, j: (layer, 0, 0)),
        pl.BlockSpec((None, d, tf), lambda i, j: (layer, 0, j)),
        pl.BlockSpec((None, d, tf), lambda i, j: (layer, 0, j + nf)),
        pl.BlockSpec((None, tf, d), lambda i, j: (layer, j, 0)),
    ]
    args = [x, g, w_in, w_in, w_out]
    if final_gain is not None:
        in_specs.append(pl.BlockSpec((1, d), lambda i, j: (0, 0)))
        args.append(final_gain)
    return pl.pallas_call(
        functools.partial(_ffn_kernel, final_norm=final_gain is not None),
        grid=(n // tm, nf),
        in_specs=in_specs,
        out_specs=pl.BlockSpec((tm, d), lambda i, j: (i, 0)),
        out_shape=jax.ShapeDtypeStruct((n, d), F32),
        scratch_shapes=[pltpu.VMEM((tm, d), BF16)],
        compiler_params=_params("parallel", "arbitrary"),
        name="ffn",
    )(*args)


def _norm_rope(y, gain, cos, sin_signed):
    yn = _rms(y, gain)
    lane = lax.broadcasted_iota(jnp.int32, yn.shape, 1)
    partner = jnp.where((lane & 1) == 0, pltpu.roll(yn, HEAD_DIM - 1, 1), pltpu.roll(yn, 1, 1))
    return yn * cos + partner * sin_signed


def _qkv_kernel(x_ref, g_ref, w_ref, cos_ref, sin_ref, qg_ref, kg_ref,
                a0_ref, a1_ref, a2_ref, b_ref, c_ref, u_ref, split_ref):
    j = pl.program_id(1)
    tm = x_ref.shape[0]

    @pl.when(j == 0)
    def _():
        u_ref[...] = _rms(x_ref[...], g_ref[...]).astype(BF16)

    def proj(c0, width):
        return jnp.dot(u_ref[...], w_ref[:, c0:c0 + width], preferred_element_type=F32)

    @pl.when(j < 3)
    def _():
        sc = jnp.where(j == 0, Q_SCALE, 1.0).astype(F32)
        a0_ref[...] = (proj(0, GROUP_W) * sc).astype(BF16)
        for gi, dst_ref in ((1, a1_ref), (2, a2_ref)):
            r = A_GROUPS[gi][1]
            y = proj(gi * GROUP_W, GROUP_W) * sc
            for h in range(A_HEADS_PER_GROUP):
                split_ref[(gi - 1) * A_HEADS_PER_GROUP + h] = y[:, h * HEAD_DIM:(h + 1) * HEAD_DIM]
            for p in range(r):
                for h in range(A_HEADS_PER_GROUP):
                    rows = split_ref[(gi - 1) * A_HEADS_PER_GROUP + h, pl.ds(p, tm // r, stride=r), :]
                    dst_ref[p, :, h * HEAD_DIM:(h + 1) * HEAD_DIM] = rows.astype(BF16)

    @pl.when(j == 3)
    def _():
        pair = 2 * HEAD_DIM
        for c0 in range(0, B_V0, pair):
            acc = proj(c0, pair)
            for h0 in range(0, pair, HEAD_DIM):
                is_q = c0 + h0 < B_K0
                y = _norm_rope(acc[:, h0:h0 + HEAD_DIM], (qg_ref if is_q else kg_ref)[...], cos_ref[...],
                               sin_ref[...])
                b_ref[:, c0 + h0:c0 + h0 + HEAD_DIM] = ((y * Q_SCALE) if is_q else y).astype(BF16)
        b_ref[:, B_V0:] = proj(B_V0, PROJ_W - B_V0).astype(BF16)

    @pl.when(j == 4)
    def _():
        for c0 in range(0, C_OUT, GROUP_W):
            c_ref[:, c0:c0 + GROUP_W] = (proj(c0, GROUP_W) * Q_SCALE).astype(BF16)
        c_ref[:, C_OUT:] = proj(C_OUT, PROJ_W - C_OUT).astype(BF16)


def _qkv(x, g, w_in, cos, sin_signed, qg, kg, bsz, seq, layer):
    n, d = x.shape
    tm = _tile(seq, PROJ_ROW_TILE)
    nseq = seq // tm
    r1, r2 = A_GROUPS[1][1], A_GROUPS[2][1]
    assert tm % (16 * r2) == 0
    a_part = lambda i, j: jnp.minimum(j, 2)

    def split_spec(r):
        return pl.BlockSpec((None, r, tm // r, GROUP_W), lambda i, j: (i // nseq, 0, i % nseq, a_part(i, j)))

    return pl.pallas_call(
        _qkv_kernel,
        grid=(n // tm, QKV_COLS // PROJ_W),
        in_specs=[
            pl.BlockSpec((tm, d), lambda i, j: (i, 0)),
            pl.BlockSpec((None, 1, d), lambda i, j: (layer, 0, 0)),
            pl.BlockSpec((None, d, PROJ_W), lambda i, j: (layer, 0, j)),
            pl.BlockSpec((tm, HEAD_DIM), lambda i, j: (i % nseq, 0)),
            pl.BlockSpec((tm, HEAD_DIM), lambda i, j: (i % nseq, 0)),
            pl.BlockSpec((None, 1, HEAD_DIM), lambda i, j: (layer, 0, 0)),
            pl.BlockSpec((None, 1, HEAD_DIM), lambda i, j: (layer, 0, 0)),
        ],
        out_specs=[
            pl.BlockSpec((tm, GROUP_W), lambda i, j: (i, a_part(i, j))),
            split_spec(r1),
            split_spec(r2),
            pl.BlockSpec((tm, PROJ_W), lambda i, j: (i, 0)),
            pl.BlockSpec((tm, PROJ_W), lambda i, j: (i, 0)),
            pl.BlockSpec((tm, d), lambda i, j: (i, 0)),
        ],
        out_shape=[
            jax.ShapeDtypeStruct((n, PROJ_W), BF16),
            jax.ShapeDtypeStruct((bsz, r1, seq // r1, PROJ_W), BF16),
            jax.ShapeDtypeStruct((bsz, r2, seq // r2, PROJ_W), BF16),
            jax.ShapeDtypeStruct((n, PROJ_W), BF16),
            jax.ShapeDtypeStruct((n, PROJ_W), BF16),
            jax.ShapeDtypeStruct((n, d), BF16),
        ],
        scratch_shapes=[pltpu.VMEM(((PROJ_W - GROUP_W) // HEAD_DIM, tm, HEAD_DIM), F32)],
        compiler_params=_params("parallel", "arbitrary"),
        name="qkv_proj",
    )(x, g, w_in, cos, sin_signed, qg, kg)


def _band_kernel(*refs, tq, sub, hb, half_w, n_sub, slopes, dist_scale, n_kv, has_sink, emit_lse):
    if has_sink:
        sink_ref, refs = refs[0], refs[1:]
    q_ref, kp_ref, km_ref, kn_ref, vp_ref, vm_ref, vn_ref, o_ref = refs[:8]
    n_heads = len(slopes)
    group = n_heads // n_kv
    qi = pl.program_id(1)
    span = sub + 2 * hb
    row = lax.broadcasted_iota(jnp.int32, (sub, span), 0)
    col = lax.broadcasted_iota(jnp.int32, (sub, span), 1)
    arel = jnp.abs(col - hb - row)
    in_band = arel <= half_w
    dist = (arel * dist_scale).astype(F32)
    biases = [jnp.where(in_band, (-slopes[h] * LOG2_E) * dist, NEG_INF) for h in range(n_heads)]
    kcat, vcat = [], []
    for kv in range(n_kv):
        ksl = slice(kv * HEAD_DIM, (kv + 1) * HEAD_DIM)
        kcat.append(jnp.concatenate([kp_ref[:, ksl], km_ref[:, ksl], kn_ref[:, ksl]], axis=0))
        vcat.append(jnp.concatenate([vp_ref[:, ksl], vm_ref[:, ksl], vn_ref[:, ksl]], axis=0))
    if emit_lse:
        lane = lax.broadcasted_iota(jnp.int32, (sub, HEAD_DIM), 1)
    for s in range(tq // sub):
        rows = slice(s * sub, (s + 1) * sub)
        kpos = qi * tq + (s * sub - hb) + col
        in_seq = (kpos >= 0) & (kpos < n_sub)
        lse_tile = jnp.zeros((sub, HEAD_DIM), F32) if emit_lse else None
        for h in range(n_heads):
            sl = slice(h * HEAD_DIM, (h + 1) * HEAD_DIM)
            ks = kcat[h // group][s * sub:s * sub + span]
            vs = vcat[h // group][s * sub:s * sub + span]
            sc = lax.dot_general(q_ref[rows, sl], ks, (((1,), (1,)), ((), ())), preferred_element_type=F32)
            sc = jnp.where(in_seq, sc + biases[h], NEG_INF)
            m = jnp.max(sc, axis=-1, keepdims=True)
            if has_sink:
                sk = sink_ref[0, h] * LOG2_E
                m = jnp.maximum(m, sk)
            p = jnp.exp2(sc - m)
            den = jnp.sum(p, axis=-1, keepdims=True)
            if has_sink:
                den = den + jnp.exp2(sk - m)
            o = jnp.dot(p.astype(BF16), vs, preferred_element_type=F32) * (1.0 / den)
            o_ref[rows, sl] = o.astype(o_ref.dtype)
            if emit_lse:
                lse_tile = jnp.where(lane == h, m + jnp.log2(den), lse_tile)
        if emit_lse:
            refs[8][rows, :] = lse_tile


def _band_gqa_kernel(*refs, tq, sub, hb, half_w, n_sub, slopes, dist_scale, n_kv, has_sink, emit_lse):
    assert not emit_lse
    if has_sink:
        sink_ref, refs = refs[0], refs[1:]
    q_ref, kp_ref, km_ref, kn_ref, vp_ref, vm_ref, vn_ref, o_ref = refs
    n_heads = len(slopes)
    group = n_heads // n_kv
    qi = pl.program_id(1)
    span = sub + 2 * hb
    head_sl = lambda h: slice(h * HEAD_DIM, (h + 1) * HEAD_DIM)
    key = lax.broadcasted_iota(jnp.int32, (span, sub), 0)
    qry = lax.broadcasted_iota(jnp.int32, (span, sub), 1)
    arel = jnp.abs(key - hb - qry)
    in_band = arel <= half_w
    dist = (arel * dist_scale).astype(F32)
    key_wide = lax.broadcasted_iota(jnp.int32, (span, group * sub), 0)
    biases, sinks, kcat, vcat = [], [], [], []
    for kv in range(n_kv):
        heads = range(kv * group, (kv + 1) * group)
        biases.append(jnp.concatenate(
            [jnp.where(in_band, (-slopes[h] * LOG2_E) * dist, NEG_INF) for h in heads], axis=1))
        if has_sink:
            sinks.append(jnp.concatenate(
                [jnp.full((1, sub), sink_ref[0, h] * LOG2_E, F32) for h in heads], axis=1))
        ksl = head_sl(kv)
        kcat.append(jnp.concatenate([kp_ref[:, ksl], km_ref[:, ksl], kn_ref[:, ksl]], axis=0))
        vcat.append(jnp.concatenate([vp_ref[:, ksl], vm_ref[:, ksl], vn_ref[:, ksl]], axis=0))
    for s in range(tq // sub):
        rows = slice(s * sub, (s + 1) * sub)
        kpos = qi * tq + (s * sub - hb) + key_wide
        in_seq = (kpos >= 0) & (kpos < n_sub)
        for kv in range(n_kv):
            heads = range(kv * group, (kv + 1) * group)
            ks = kcat[kv][s * sub:s * sub + span]
            vs = vcat[kv][s * sub:s * sub + span]
            q = jnp.concatenate([q_ref[rows, head_sl(h)] for h in heads], axis=0)
            st = lax.dot_general(ks, q, (((1,), (1,)), ((), ())), preferred_element_type=F32)
            st = jnp.where(in_seq, st + biases[kv], NEG_INF)
            m = jnp.max(st, axis=0, keepdims=True)
            if has_sink:
                m = jnp.maximum(m, sinks[kv])
            p = jnp.exp2(st - m)
            den = jnp.sum(p, axis=0, keepdims=True)
            if has_sink:
                den = den + jnp.exp2(sinks[kv] - m)
            ot = lax.dot_general(vs, p.astype(BF16), (((0,), (0,)), ((), ())), preferred_element_type=F32)
            ot = ot * (1.0 / den)
            for gi, h in enumerate(heads):
                o_ref[rows, head_sl(h)] = ot[:, gi * sub:(gi + 1) * sub].T.astype(o_ref.dtype)


def _band_attention(arr, *, q_col, k_col, v_col, n_heads, n_kv, half_w, slopes, dist_scale, sink, emit_lse,
                    out_dtype):
    nb, n_sub, _ = arr.shape
    hb = half_w
    sub = min(HEAD_DIM, n_sub)
    tq = _tile(n_sub, BAND_Q_TILE)
    assert sub % hb == 0 and tq % sub == 0 and n_sub % tq == 0
    per = tq // hb
    n_hb = n_sub // hb
    qw, kw = n_heads * HEAD_DIM, n_kv * HEAD_DIM
    assert q_col % qw == 0 and k_col % kw == 0 and v_col % kw == 0

    def halo_specs(c):
        return [
            pl.BlockSpec((None, hb, kw), lambda b, i: (b, jnp.maximum(i * per - 1, 0), c // kw)),
            pl.BlockSpec((None, tq, kw), lambda b, i: (b, i, c // kw)),
            pl.BlockSpec((None, hb, kw), lambda b, i: (b, jnp.minimum((i + 1) * per, n_hb - 1), c // kw)),
        ]

    in_specs = [pl.BlockSpec((None, tq, qw), lambda b, i: (b, i, q_col // qw))] + halo_specs(k_col) + halo_specs(v_col)
    args = [arr] * 7
    if sink is not None:
        in_specs = [pl.BlockSpec(memory_space=pltpu.SMEM)] + in_specs
        args = [sink] + args
    out_specs = [pl.BlockSpec((None, tq, qw), lambda b, i: (b, i, 0))]
    out_shape = [jax.ShapeDtypeStruct((nb, n_sub, qw), out_dtype)]
    if emit_lse:
        out_specs.append(pl.BlockSpec((None, tq, HEAD_DIM), lambda b, i: (b, i, 0)))
        out_shape.append(jax.ShapeDtypeStruct((nb, n_sub, HEAD_DIM), F32))
    body = _band_gqa_kernel if n_kv < n_heads else _band_kernel
    return pl.pallas_call(
        functools.partial(body, tq=tq, sub=sub, hb=hb, half_w=half_w, n_sub=n_sub, slopes=slopes,
                          dist_scale=dist_scale, n_kv=n_kv, has_sink=sink is not None, emit_lse=emit_lse),
        grid=(nb, n_sub // tq),
        in_specs=in_specs,
        out_specs=out_specs,
        out_shape=out_shape,
        compiler_params=_params("parallel", "arbitrary"),
        name=f"band_attn_n{n_sub}_w{half_w}",
    )(*args)


def _merge_kernel(o0_ref, o1_ref, o2_ref, l0_ref, l1_ref, l2_ref, out_ref, ob1_ref, ob2_ref, lb1_ref, lb2_ref):
    tm = out_ref.shape[0]
    for src_ref, dst_ref in ((o1_ref, ob1_ref), (o2_ref, ob2_ref), (l1_ref, lb1_ref), (l2_ref, lb2_ref)):
        r = src_ref.shape[0]
        for p in range(r):
            for h in range(dst_ref.shape[0]):
                dst_ref[h, pl.ds(p, tm // r, stride=r), :] = src_ref[p, :, h * HEAD_DIM:(h + 1) * HEAD_DIM]
    l0, l1, l2 = l0_ref[...], lb1_ref[0], lb2_ref[0]
    mx = jnp.maximum(jnp.maximum(l0, l1), l2)
    w0, w1, w2 = jnp.exp2(l0 - mx), jnp.exp2(l1 - mx), jnp.exp2(l2 - mx)
    den = w0 + w1 + w2
    a0, a1, a2 = w0 / den, w1 / den, w2 / den
    for h in range(A_HEADS_PER_GROUP):
        sl = slice(h * HEAD_DIM, (h + 1) * HEAD_DIM)
        out = a0[:, h:h + 1] * o0_ref[:, sl] + a1[:, h:h + 1] * ob1_ref[h] + a2[:, h:h + 1] * ob2_ref[h]
        out_ref[:, sl] = out.astype(BF16)


def _merge_groups(outs, lses, bsz, seq):
    n = bsz * seq
    tm = _tile(seq, MERGE_ROW_TILE)
    nseq = seq // tm
    r1, r2 = A_GROUPS[1][1], A_GROUPS[2][1]

    def specs(w):
        return [
            pl.BlockSpec((tm, w), lambda i: (i, 0)),
            pl.BlockSpec((None, r1, tm // r1, w), lambda i: (i // nseq, 0, i % nseq, 0)),
            pl.BlockSpec((None, r2, tm // r2, w), lambda i: (i // nseq, 0, i % nseq, 0)),
        ]

    def views(xs, w):
        return [xs[0].reshape(n, w), xs[1].reshape(bsz, r1, seq // r1, w), xs[2].reshape(bsz, r2, seq // r2, w)]

    return pl.pallas_call(
        _merge_kernel,
        grid=(n // tm,),
        in_specs=specs(A_OUT) + specs(HEAD_DIM),
        out_specs=pl.BlockSpec((tm, A_OUT), lambda i: (i, 0)),
        out_shape=jax.ShapeDtypeStruct((n, A_OUT), BF16),
        scratch_shapes=[pltpu.VMEM((A_HEADS_PER_GROUP, tm, HEAD_DIM), F32),
                        pltpu.VMEM((A_HEADS_PER_GROUP, tm, HEAD_DIM), F32),
                        pltpu.VMEM((1, tm, HEAD_DIM), F32), pltpu.VMEM((1, tm, HEAD_DIM), F32)],
        compiler_params=_params("parallel"),
        name="merge_dilated",
    )(*views(outs, A_OUT), *views(lses, HEAD_DIM))


def _dense_kernel(q_ref, k_ref, v_ref, o_ref, qt_ref, st_ref, m_ref, l_ref, acc_ref, *, tq, tk, seq):
    group = B_Q_HEADS // B_KV_HEADS
    n_blk = seq // tk
    for g in range(group):
        qg = q_ref[:, g * HEAD_DIM:(g + 1) * HEAD_DIM].astype(F32)
        qt_ref[:, g * tq:(g + 1) * tq] = qg.T.astype(BF16)

    def scores(blk):
        k0 = pl.multiple_of(blk * tk, tk)
        return jnp.dot(k_ref[pl.ds(k0, tk), :], qt_ref[...], preferred_element_type=F32)

    def values(blk):
        v0 = pl.multiple_of(blk * tk, tk)
        return v_ref[pl.ds(v0, tk), :]

    def pv(v, p):
        return lax.dot_general(v, p.astype(BF16), (((0,), (0,)), ((), ())), preferred_element_type=F32)

    def write_out():
        out = acc_ref[...] / l_ref[...]
        for g in range(group):
            o_ref[:, g * HEAD_DIM:(g + 1) * HEAD_DIM] = out[:, g * tq:(g + 1) * tq].T.astype(BF16)

    first = jnp.dot(k_ref[pl.ds(0, min(tk, HEAD_DIM)), :], qt_ref[...], preferred_element_type=F32)
    m_ref[...] = jnp.max(first, axis=0, keepdims=True)
    l_ref[...] = jnp.zeros(l_ref.shape, F32)
    acc_ref[...] = jnp.zeros(acc_ref.shape, F32)

    def lazy_step(blk):
        k0 = pl.multiple_of(blk * tk, tk)
        k = k_ref[pl.ds(k0, tk), :]
        v = values(blk)
        width = LANE_CHUNK_W
        chunks = [slice(c * width, (c + 1) * width) for c in range(group * tq // width)]
        sts = [jnp.dot(k, qt_ref[:, sl], preferred_element_type=F32) for sl in chunks]
        for sl, st in zip(chunks, sts):
            p = jnp.exp2(st - m_ref[:, sl])
            l_ref[:, sl] += jnp.sum(p, axis=0, keepdims=True)
            acc_ref[:, sl] += pv(v, p)

    def lazy_body(c, carry):
        lazy_step(2 * c)
        lazy_step(2 * c + 1)
        return carry

    lax.fori_loop(0, n_blk // 2, lazy_body, 0)
    l = l_ref[...]
    sums_ok = jnp.min(jnp.where(jnp.abs(acc_ref[...]) < F32_HUGE, 1.0, 0.0))
    sums_ok = sums_ok * jnp.min(jnp.where((l > 0.0) & (l < F32_HUGE), 1.0, 0.0))
    exceeded = sums_ok < 0.5

    @pl.when(jnp.logical_not(exceeded))
    def _():
        write_out()

    @pl.when(exceeded)
    def _():
        _dense_online(scores, values, pv, write_out, st_ref, m_ref, l_ref, acc_ref, n_blk)


def _dense_online(scores, values, pv, write_out, st_ref, m_ref, l_ref, acc_ref, n_blk):
    m_ref[...] = jnp.full(m_ref.shape, -jnp.inf, F32)
    l_ref[...] = jnp.zeros(l_ref.shape, F32)
    acc_ref[...] = jnp.zeros(acc_ref.shape, F32)
    st_ref[0] = scores(0)

    def step(blk, cur, nxt):
        st_ref[nxt] = scores(jnp.minimum(blk + 1, n_blk - 1))
        st = st_ref[cur]
        m_prev = m_ref[...]
        m_new = jnp.maximum(m_prev, jnp.max(st, axis=0, keepdims=True))
        alpha = jnp.exp2(m_prev - m_new)
        p = jnp.exp2(st - m_new)
        l_ref[...] = alpha * l_ref[...] + jnp.sum(p, axis=0, keepdims=True)
        acc_ref[...] = alpha * acc_ref[...] + pv(values(blk), p)
        m_ref[...] = m_new

    def body(c, carry):
        step(2 * c, 0, 1)
        step(2 * c + 1, 1, 0)
        return carry

    lax.fori_loop(0, n_blk // 2, body, 0)
    write_out()


def _dense_attention(arr):
    bsz, seq, _ = arr.shape
    tq = _tile(seq, DENSE_Q_TILE)
    tk = _tile(seq // 2, DENSE_K_TILE)
    assert (seq // tk) % 2 == 0
    group = B_Q_HEADS // B_KV_HEADS
    qw = group * HEAD_DIM
    out = pl.pallas_call(
        functools.partial(_dense_kernel, tq=tq, tk=tk, seq=seq),
        grid=(bsz, B_KV_HEADS, seq // tq),
        in_specs=[
            pl.BlockSpec((None, tq, qw), lambda b, h, i: (b, i, h)),
            pl.BlockSpec((None, seq, HEAD_DIM), lambda b, h, i: (b, 0, B_K0 // HEAD_DIM + h)),
            pl.BlockSpec((None, seq, HEAD_DIM), lambda b, h, i: (b, 0, B_V0 // HEAD_DIM + h)),
        ],
        out_specs=pl.BlockSpec((None, tq, qw), lambda b, h, i: (b, i, h)),
        out_shape=jax.ShapeDtypeStruct((bsz, seq, B_OUT), BF16),
        scratch_shapes=[
            pltpu.VMEM((HEAD_DIM, group * tq), BF16),
            pltpu.VMEM((2, tk, group * tq), F32),
            pltpu.VMEM((1, group * tq), F32),
            pltpu.VMEM((1, group * tq), F32),
            pltpu.VMEM((HEAD_DIM, group * tq), F32),
        ],
        compiler_params=_params("parallel", "parallel", "arbitrary"),
        name="dense_attn",
    )(arr, arr, arr)
    return out.reshape(bsz * seq, B_OUT)


def _out_kernel(x_ref, u_ref, wg0_ref, wg1_ref, wg2_ref, oa_ref, ob_ref, oc_ref,
                wa_ref, wb_ref, wc_ref, wo_ref, o_ref):
    j = pl.program_id(1)

    @pl.when(j == 0)
    def _():
        o_ref[...] = x_ref[...]

    u = u_ref[...]

    def branch(wg_ref, br_ref, w_ref):
        gate = jax.nn.sigmoid(jnp.dot(u, wg_ref[...], preferred_element_type=F32))
        return gate * jnp.dot(br_ref[...], w_ref[...], preferred_element_type=F32)

    merged = branch(wg0_ref, oa_ref, wa_ref) + branch(wg1_ref, ob_ref, wb_ref) + branch(wg2_ref, oc_ref, wc_ref)
    o_ref[...] += jnp.dot(merged.astype(BF16), wo_ref[...], preferred_element_type=F32)


def _mixer_out(x, u, w_in, o_a, o_b, o_c, w_a, w_b, w_c, w_o, layer):
    n, d = x.shape
    tm = _tile(n, OUT_ROW_TILE)
    tc = _tile(d, OUT_COL_TILE)
    assert QKV_COLS % tc == 0
    g0 = QKV_COLS // tc
    gstep = d // tc

    def gate_spec(br):
        return pl.BlockSpec((None, d, tc), lambda i, j: (layer, 0, g0 + br * gstep + j))

    return pl.pallas_call(
        _out_kernel,
        grid=(n // tm, d // tc),
        in_specs=[
            pl.BlockSpec((tm, d), lambda i, j: (i, 0)),
            pl.BlockSpec((tm, d), lambda i, j: (i, 0)),
            gate_spec(0), gate_spec(1), gate_spec(2),
            pl.BlockSpec((tm, A_OUT), lambda i, j: (i, 0)),
            pl.BlockSpec((tm, B_OUT), lambda i, j: (i, 0)),
            pl.BlockSpec((tm, C_OUT), lambda i, j: (i, 0)),
            pl.BlockSpec((None, A_OUT, tc), lambda i, j: (layer, 0, j)),
            pl.BlockSpec((None, B_OUT, tc), lambda i, j: (layer, 0, j)),
            pl.BlockSpec((None, C_OUT, tc), lambda i, j: (layer, 0, j)),
            pl.BlockSpec((None, tc, d), lambda i, j: (layer, j, 0)),
        ],
        out_specs=pl.BlockSpec((tm, d), lambda i, j: (i, 0)),
        out_shape=jax.ShapeDtypeStruct((n, d), F32),
        compiler_params=_params("parallel", "arbitrary"),
        name="mixer_out",
    )(x, u, w_in, w_in, w_in, o_a, o_b, o_c, w_a, w_b, w_c, w_o)


def _rope_tables(seq):
    n_freq = HEAD_DIM // 4
    t = lax.broadcasted_iota(jnp.int32, (seq, HEAD_DIM), 0)
    lane = lax.broadcasted_iota(jnp.int32, (seq, HEAD_DIM), 1)
    pair = lane // 2
    pos = jnp.where(pair < n_freq, t // GRID_W, t % GRID_W).astype(F32)
    inv = ROPE_THETA ** (-(pair % n_freq).astype(F32) / n_freq)
    ang = pos * inv
    sign = jnp.where(lane % 2 == 0, -1.0, 1.0).astype(F32)
    return jnp.cos(ang), jnp.sin(ang) * sign


def _mixer(x, lw, cos, sin_signed, bsz, seq, layer):
    a0, a1, a2, bqkv, cqkv, u = _qkv(x, lw["norm_mix"], lw["w_in"], cos, sin_signed, lw["qk_norm_q"],
                                     lw["qk_norm_k"], bsz, seq, layer)
    a_slopes = _alibi_slopes(A_HEADS)
    outs, lses = [], []
    for gi, (arr, (window, r)) in enumerate(zip((a0, a1, a2), A_GROUPS)):
        hs = slice(gi * A_HEADS_PER_GROUP, (gi + 1) * A_HEADS_PER_GROUP)
        o, l = _band_attention(
            arr.reshape(bsz * r, seq // r, PROJ_W), q_col=0, k_col=GROUP_W, v_col=2 * GROUP_W,
            n_heads=A_HEADS_PER_GROUP, n_kv=A_HEADS_PER_GROUP, half_w=(window // 2) // r, slopes=a_slopes[hs],
            dist_scale=r, sink=None, emit_lse=True, out_dtype=F32)
        outs.append(o)
        lses.append(l)
    o_a = _merge_groups(outs, lses, bsz, seq)
    o_b = _dense_attention(bqkv.reshape(bsz, seq, PROJ_W))
    o_c, = _band_attention(
        cqkv.reshape(bsz, seq, PROJ_W), q_col=0, k_col=C_K0, v_col=C_V0, n_heads=C_Q_HEADS, n_kv=C_KV_HEADS,
        half_w=C_HALF_WINDOW, slopes=_alibi_slopes(C_Q_HEADS), dist_scale=1, sink=lw["sink_c"][layer],
        emit_lse=False, out_dtype=BF16)
    return _mixer_out(x, u, lw["w_in"], o_a, o_b, o_c.reshape(bsz * seq, C_OUT), lw["w_br_a"],
                      lw["w_br_b"], lw["w_br_c"], lw["w_out"], layer)


def _trunk(x, lw, norm_final, depth):
    bsz, seq, d = x.shape
    cos, sin_signed = _rope_tables(seq)
    xf = x.reshape(bsz * seq, d)
    for layer in range(depth):
        xf = _ffn(xf, lw["norm_ffn1"], lw["ffn1_w_in"], lw["ffn1_w_out"], layer)
        xf = _mixer(xf, lw, cos, sin_signed, bsz, seq, layer)
        last = layer == depth - 1
        xf = _ffn(xf, lw["norm_ffn2"], lw["ffn2_w_in"], lw["ffn2_w_out"], layer,
                  final_gain=norm_final.reshape(1, d) if last else None)
    return xf.reshape(bsz, seq, d)


def kernel(x_prompt, x_sample, norm_ffn1, ffn1_w_in, ffn1_w_out, norm_mix, w_in, qk_norm_q, qk_norm_k, sink_c,
           w_br_a, w_br_b, w_br_c, w_out, norm_ffn2, ffn2_w_in, ffn2_w_out, norm_final):
    depth = norm_ffn1.shape[0]
    row = lambda v: v.reshape(depth, 1, v.shape[-1])
    layers = {
        "norm_ffn1": row(norm_ffn1), "ffn1_w_in": ffn1_w_in.astype(BF16), "ffn1_w_out": ffn1_w_out.astype(BF16),
        "norm_mix": row(norm_mix), "w_in": w_in.astype(BF16),
        "qk_norm_q": row(qk_norm_q), "qk_norm_k": row(qk_norm_k), "sink_c": row(sink_c),
        "w_br_a": w_br_a.astype(BF16), "w_br_b": w_br_b.astype(BF16), "w_br_c": w_br_c.astype(BF16),
        "w_out": w_out.astype(BF16),
        "norm_ffn2": row(norm_ffn2), "ffn2_w_in": ffn2_w_in.astype(BF16), "ffn2_w_out": ffn2_w_out.astype(BF16),
    }
    assert x_prompt.shape[1:] == x_sample.shape[1:]
    n_prompt = x_prompt.shape[0]
    y = _trunk(jnp.concatenate([x_prompt, x_sample], axis=0), layers, norm_final, depth)
    return (y[:n_prompt], y[n_prompt:])
```

```python
import functools

import jax
import jax.numpy as jnp
from jax import lax
from jax.experimental import pallas as pl
from jax.experimental.pallas import tpu as pltpu

F32 = jnp.float32
BF16 = jnp.bfloat16

HEAD_DIM = 128
SCALE = HEAD_DIM ** -0.5
LOG2_E = 1.4426950408889634
Q_SCALE = SCALE * LOG2_E
NORM_EPS = 1e-6
NEG_INF = -1e30
F32_HUGE = 1e38
LANE_CHUNK_W = 256
ROPE_THETA = 10000.0
GRID_W = 64
A_GROUPS = ((128, 1), (512, 4), (2048, 16))
A_HEADS_PER_GROUP = 4
A_HEADS = 12
B_Q_HEADS = 8
B_KV_HEADS = 2
C_Q_HEADS = 8
C_KV_HEADS = 2
C_HALF_WINDOW = 128
N_BRANCHES = 3

QKV_COLS = 7680
PROJ_W = 1536
GROUP_W = 4 * HEAD_DIM
A_OUT = A_HEADS_PER_GROUP * HEAD_DIM
B_OUT = B_Q_HEADS * HEAD_DIM
C_OUT = C_Q_HEADS * HEAD_DIM
B_K0, B_V0 = B_OUT, B_OUT + B_KV_HEADS * HEAD_DIM
C_K0, C_V0 = C_OUT, C_OUT + C_KV_HEADS * HEAD_DIM

V7X_VMEM_LIMIT_BYTES = 56 * 1024 * 1024

FFN_ROW_TILE, FFN_HIDDEN_TILE = 1024, 512
FFN_NORM_CHUNKS = 4
PROJ_ROW_TILE = 512
OUT_ROW_TILE, OUT_COL_TILE = 512, 512
DENSE_Q_TILE, DENSE_K_TILE = 2048, 512
DENSE_ONLINE_K_TILE = 128
BAND_Q_TILE = 1024
MERGE_ROW_TILE = 1024


def _tile(n, pref):
    t = min(pref, n)
    while n % t:
        t //= 2
    return t


def _params(*sem):
    return pltpu.CompilerParams(dimension_semantics=sem, vmem_limit_bytes=V7X_VMEM_LIMIT_BYTES)


def _rms(x, g):
    ms = jnp.mean(x * x, axis=-1, keepdims=True)
    return x * lax.rsqrt(ms + NORM_EPS) * g


def _alibi_slopes(n):
    return tuple(2.0 ** (-8.0 * i / n) for i in range(1, n + 1))


def _ffn_kernel(x_ref, g_ref, wa_ref, wb_ref, wo_ref, *rest, final_norm):
    o_ref, xn_ref = rest[-2:]
    j = pl.program_id(1)

    def half_swiglu(xn):
        a = jnp.dot(xn, wa_ref[...], preferred_element_type=F32)
        b = jnp.dot(xn, wb_ref[...], preferred_element_type=F32)
        h = (0.5 * a * jax.nn.sigmoid(a) * b).astype(BF16)
        return jnp.dot(h, wo_ref[...], preferred_element_type=F32)

    @pl.when(j == 0)
    def _():
        rows = x_ref.shape[0] // FFN_NORM_CHUNKS
        for c in range(FFN_NORM_CHUNKS):
            sl = slice(c * rows, (c + 1) * rows)
            x = x_ref[sl, :]
            xn = _rms(x, g_ref[...]).astype(BF16)
            xn_ref[sl, :] = xn
            o_ref[sl, :] = x + half_swiglu(xn)

    @pl.when(j > 0)
    def _():
        o_ref[...] += half_swiglu(xn_ref[...])

    if final_norm:
        @pl.when(j == pl.num_programs(1) - 1)
        def _():
            o_ref[...] = _rms(o_ref[...], rest[0][...])


def _ffn(x, g, w_in, w_out, layer, final_gain=None):
    n, d = x.shape
    d_ff = w_out.shape[1]
    tm = _tile(n, FFN_ROW_TILE)
    tf = _tile(d_ff, FFN_HIDDEN_TILE)
    nf = d_ff // tf
    in_specs = [
        pl.BlockSpec((tm, d), lambda i, j: (i, 0)),
        pl.BlockSpec((None, 1, d), lambda i, j: (layer, 0, 0)),
        pl.BlockSpec((None, d, tf), lambda i, j: (layer, 0, j)),
        pl.BlockSpec((None, d, tf), lambda i, j: (layer, 0, j + nf)),
        pl.BlockSpec((None, tf, d), lambda i, j: (layer, j, 0)),
    ]
    args = [x, g, w_in, w_in, w_out]
    if final_gain is not None:
        in_specs.append(pl.BlockSpec((1, d), lambda i, j: (0, 0)))
        args.append(final_gain)
    return pl.pallas_call(
        functools.partial(_ffn_kernel, final_norm=final_gain is not None),
        grid=(n // tm, nf),
        in_specs=in_specs,
        out_specs=pl.BlockSpec((tm, d), lambda i, j: (i, 0)),
        out_shape=jax.ShapeDtypeStruct((n, d), F32),
        scratch_shapes=[pltpu.VMEM((tm, d), BF16)],
        compiler_params=_params("parallel", "arbitrary"),
        name="ffn",
    )(*args)


def _norm_rope(y, gain, cos, sin_signed):
    yn = _rms(y, gain)
    lane = lax.broadcasted_iota(jnp.int32, yn.shape, 1)
    partner = jnp.where((lane & 1) == 0, pltpu.roll(yn, HEAD_DIM - 1, 1), pltpu.roll(yn, 1, 1))
    return yn * cos + partner * sin_signed


def _qkv_kernel(x_ref, g_ref, w_ref, cos_ref, sin_ref, qg_ref, kg_ref,
                a0_ref, a1_ref, a2_ref, b_ref, c_ref, u_ref, split_ref):
    j = pl.program_id(1)
    tm = x_ref.shape[0]

    @pl.when(j == 0)
    def _():
        u_ref[...] = _rms(x_ref[...], g_ref[...]).astype(BF16)

    def proj(c0, width):
        return jnp.dot(u_ref[...], w_ref[:, c0:c0 + width], preferred_element_type=F32)

    @pl.when(j < 3)
    def _():
        sc = jnp.where(j == 0, Q_SCALE, 1.0).astype(F32)
        a0_ref[...] = (proj(0, GROUP_W) * sc).astype(BF16)
        for gi, dst_ref in ((1, a1_ref), (2, a2_ref)):
            r = A_GROUPS[gi][1]
            y = proj(gi * GROUP_W, GROUP_W) * sc
            for h in range(A_HEADS_PER_GROUP):
                split_ref[(gi - 1) * A_HEADS_PER_GROUP + h] = y[:, h * HEAD_DIM:(h + 1) * HEAD_DIM]
            for p in range(r):
                for h in range(A_HEADS_PER_GROUP):
                    rows = split_ref[(gi - 1) * A_HEADS_PER_GROUP + h, pl.ds(p, tm // r, stride=r), :]
                    dst_ref[p, :, h * HEAD_DIM:(h + 1) * HEAD_DIM] = rows.astype(BF16)

    @pl.when(j == 3)
    def _():
        pair = 2 * HEAD_DIM
        for c0 in range(0, B_V0, pair):
            acc = proj(c0, pair)
            for h0 in range(0, pair, HEAD_DIM):
                is_q = c0 + h0 < B_K0
                y = _norm_rope(acc[:, h0:h0 + HEAD_DIM], (qg_ref if is_q else kg_ref)[...], cos_ref[...],
                               sin_ref[...])
                b_ref[:, c0 + h0:c0 + h0 + HEAD_DIM] = ((y * Q_SCALE) if is_q else y).astype(BF16)
        b_ref[:, B_V0:] = proj(B_V0, PROJ_W - B_V0).astype(BF16)

    @pl.when(j == 4)
    def _():
        for c0 in range(0, C_OUT, GROUP_W):
            c_ref[:, c0:c0 + GROUP_W] = (proj(c0, GROUP_W) * Q_SCALE).astype(BF16)
        c_ref[:, C_OUT:] = proj(C_OUT, PROJ_W - C_OUT).astype(BF16)


def _qkv(x, g, w_in, cos, sin_signed, qg, kg, bsz, seq, layer):
    n, d = x.shape
    tm = _tile(seq, PROJ_ROW_TILE)
    nseq = seq // tm
    r1, r2 = A_GROUPS[1][1], A_GROUPS[2][1]
    assert tm % (16 * r2) == 0
    a_part = lambda i, j: jnp.minimum(j, 2)

    def split_spec(r):
        return pl.BlockSpec((None, r, tm // r, GROUP_W), lambda i, j: (i // nseq, 0, i % nseq, a_part(i, j)))

    return pl.pallas_call(
        _qkv_kernel,
        grid=(n // tm, QKV_COLS // PROJ_W),
        in_specs=[
            pl.BlockSpec((tm, d), lambda i, j: (i, 0)),
            pl.BlockSpec((None, 1, d), lambda i, j: (layer, 0, 0)),
            pl.BlockSpec((None, d, PROJ_W), lambda i, j: (layer, 0, j)),
            pl.BlockSpec((tm, HEAD_DIM), lambda i, j: (i % nseq, 0)),
            pl.BlockSpec((tm, HEAD_DIM), lambda i, j: (i % nseq, 0)),
            pl.BlockSpec((None, 1, HEAD_DIM), lambda i, j: (layer, 0, 0)),
            pl.BlockSpec((None, 1, HEAD_DIM), lambda i, j: (layer, 0, 0)),
        ],
        out_specs=[
            pl.BlockSpec((tm, GROUP_W), lambda i, j: (i, a_part(i, j))),
            split_spec(r1),
            split_spec(r2),
            pl.BlockSpec((tm, PROJ_W), lambda i, j: (i, 0)),
            pl.BlockSpec((tm, PROJ_W), lambda i, j: (i, 0)),
            pl.BlockSpec((tm, d), lambda i, j: (i, 0)),
        ],
        out_shape=[
            jax.ShapeDtypeStruct((n, PROJ_W), BF16),
            jax.ShapeDtypeStruct((bsz, r1, seq // r1, PROJ_W), BF16),
            jax.ShapeDtypeStruct((bsz, r2, seq // r2, PROJ_W), BF16),
            jax.ShapeDtypeStruct((n, PROJ_W), BF16),
            jax.ShapeDtypeStruct((n, PROJ_W), BF16),
            jax.ShapeDtypeStruct((n, d), BF16),
        ],
        scratch_shapes=[pltpu.VMEM(((PROJ_W - GROUP_W) // HEAD_DIM, tm, HEAD_DIM), F32)],
        compiler_params=_params("parallel", "arbitrary"),
        name="qkv_proj",
    )(x, g, w_in, cos, sin_signed, qg, kg)


def _band_kernel(*refs, tq, sub, hb, half_w, n_sub, slopes, dist_scale, n_kv, has_sink, emit_lse):
    if has_sink:
        sink_ref, refs = refs[0], refs[1:]
    q_ref, kp_ref, km_ref, kn_ref, vp_ref, vm_ref, vn_ref, o_ref = refs[:8]
    n_heads = len(slopes)
    group = n_heads // n_kv
    qi = pl.program_id(1)
    span = sub + 2 * hb
    row = lax.broadcasted_iota(jnp.int32, (sub, span), 0)
    col = lax.broadcasted_iota(jnp.int32, (sub, span), 1)
    arel = jnp.abs(col - hb - row)
    in_band = arel <= half_w
    dist = (arel * dist_scale).astype(F32)
    biases = [jnp.where(in_band, (-slopes[h] * LOG2_E) * dist, NEG_INF) for h in range(n_heads)]
    kcat, vcat = [], []
    for kv in range(n_kv):
        ksl = slice(kv * HEAD_DIM, (kv + 1) * HEAD_DIM)
        kcat.append(jnp.concatenate([kp_ref[:, ksl], km_ref[:, ksl], kn_ref[:, ksl]], axis=0))
        vcat.append(jnp.concatenate([vp_ref[:, ksl], vm_ref[:, ksl], vn_ref[:, ksl]], axis=0))
    if emit_lse:
        lane = lax.broadcasted_iota(jnp.int32, (sub, HEAD_DIM), 1)
    for s in range(tq // sub):
        rows = slice(s * sub, (s + 1) * sub)
        kpos = qi * tq + (s * sub - hb) + col
        in_seq = (kpos >= 0) & (kpos < n_sub)
        lse_tile = jnp.zeros((sub, HEAD_DIM), F32) if emit_lse else None
        for h in range(n_heads):
            sl = slice(h * HEAD_DIM, (h + 1) * HEAD_DIM)
            ks = kcat[h // group][s * sub:s * sub + span]
            vs = vcat[h // group][s * sub:s * sub + span]
            sc = lax.dot_general(q_ref[rows, sl], ks, (((1,), (1,)), ((), ())), preferred_element_type=F32)
            sc = jnp.where(in_seq, sc + biases[h], NEG_INF)
            m = jnp.max(sc, axis=-1, keepdims=True)
            if has_sink:
                sk = sink_ref[0, h] * LOG2_E
                m = jnp.maximum(m, sk)
            p = jnp.exp2(sc - m)
            den = jnp.sum(p, axis=-1, keepdims=True)
            if has_sink:
                den = den + jnp.exp2(sk - m)
            o = jnp.dot(p.astype(BF16), vs, preferred_element_type=F32) * (1.0 / den)
            o_ref[rows, sl] = o.astype(o_ref.dtype)
            if emit_lse:
                lse_tile = jnp.where(lane == h, m + jnp.log2(den), lse_tile)
        if emit_lse:
            refs[8][rows, :] = lse_tile


def _band_gqa_kernel(*refs, tq, sub, hb, half_w, n_sub, slopes, dist_scale, n_kv, has_sink, emit_lse):
    assert not emit_lse
    if has_sink:
        sink_ref, refs = refs[0], refs[1:]
    q_ref, kp_ref, km_ref, kn_ref, vp_ref, vm_ref, vn_ref, o_ref = refs
    n_heads = len(slopes)
    group = n_heads // n_kv
    qi = pl.program_id(1)
    span = sub + 2 * hb
    head_sl = lambda h: slice(h * HEAD_DIM, (h + 1) * HEAD_DIM)
    key = lax.broadcasted_iota(jnp.int32, (span, sub), 0)
    qry = lax.broadcasted_iota(jnp.int32, (span, sub), 1)
    arel = jnp.abs(key - hb - qry)
    in_band = arel <= half_w
    dist = (arel * dist_scale).astype(F32)
    key_wide = lax.broadcasted_iota(jnp.int32, (span, group * sub), 0)
    biases, sinks, kcat, vcat = [], [], [], []
    for kv in range(n_kv):
        heads = range(kv * group, (kv + 1) * group)
        biases.append(jnp.concatenate(
            [jnp.where(in_band, (-slopes[h] * LOG2_E) * dist, NEG_INF) for h in heads], axis=1))
        if has_sink:
            sinks.append(jnp.concatenate(
                [jnp.full((1, sub), sink_ref[0, h] * LOG2_E, F32) for h in heads], axis=1))
        ksl = head_sl(kv)
        kcat.append(jnp.concatenate([kp_ref[:, ksl], km_ref[:, ksl], kn_ref[:, ksl]], axis=0))
        vcat.append(jnp.concatenate([vp_ref[:, ksl], vm_ref[:, ksl], vn_ref[:, ksl]], axis=0))
    for s in range(tq // sub):
        rows = slice(s * sub, (s + 1) * sub)
        kpos = qi * tq + (s * sub - hb) + key_wide
        in_seq = (kpos >= 0) & (kpos < n_sub)
        for kv in range(n_kv):
            heads = range(kv * group, (kv + 1) * group)
            ks = kcat[kv][s * sub:s * sub + span]
            vs = vcat[kv][s * sub:s * sub + span]
            q = jnp.concatenate([q_ref[rows, head_sl(h)] for h in heads], axis=0)
            st = lax.dot_general(ks, q, (((1,), (1,)), ((), ())), preferred_element_type=F32)
            st = jnp.where(in_seq, st + biases[kv], NEG_INF)
            m = jnp.max(st, axis=0, keepdims=True)
            if has_sink:
                m = jnp.maximum(m, sinks[kv])
            p = jnp.exp2(st - m)
            den = jnp.sum(p, axis=0, keepdims=True)
            if has_sink:
                den = den + jnp.exp2(sinks[kv] - m)
            ot = lax.dot_general(vs, p.astype(BF16), (((0,), (0,)), ((), ())), preferred_element_type=F32)
            ot = ot * (1.0 / den)
            for gi, h in enumerate(heads):
                o_ref[rows, head_sl(h)] = ot[:, gi * sub:(gi + 1) * sub].T.astype(o_ref.dtype)


def _band_attention(arr, *, q_col, k_col, v_col, n_heads, n_kv, half_w, slopes, dist_scale, sink, emit_lse,
                    out_dtype):
    nb, n_sub, _ = arr.shape
    hb = half_w
    sub = min(HEAD_DIM, n_sub)
    tq = _tile(n_sub, BAND_Q_TILE)
    assert sub % hb == 0 and tq % sub == 0 and n_sub % tq == 0
    per = tq // hb
    n_hb = n_sub // hb
    qw, kw = n_heads * HEAD_DIM, n_kv * HEAD_DIM
    assert q_col % qw == 0 and k_col % kw == 0 and v_col % kw == 0

    def halo_specs(c):
        return [
            pl.BlockSpec((None, hb, kw), lambda b, i: (b, jnp.maximum(i * per - 1, 0), c // kw)),
            pl.BlockSpec((None, tq, kw), lambda b, i: (b, i, c // kw)),
            pl.BlockSpec((None, hb, kw), lambda b, i: (b, jnp.minimum((i + 1) * per, n_hb - 1), c // kw)),
        ]

    in_specs = [pl.BlockSpec((None, tq, qw), lambda b, i: (b, i, q_col // qw))] + halo_specs(k_col) + halo_specs(v_col)
    args = [arr] * 7
    if sink is not None:
        in_specs = [pl.BlockSpec(memory_space=pltpu.SMEM)] + in_specs
        args = [sink] + args
    out_specs = [pl.BlockSpec((None, tq, qw), lambda b, i: (b, i, 0))]
    out_shape = [jax.ShapeDtypeStruct((nb, n_sub, qw), out_dtype)]
    if emit_lse:
        out_specs.append(pl.BlockSpec((None, tq, HEAD_DIM), lambda b, i: (b, i, 0)))
        out_shape.append(jax.ShapeDtypeStruct((nb, n_sub, HEAD_DIM), F32))
    body = _band_gqa_kernel if n_kv < n_heads else _band_kernel
    return pl.pallas_call(
        functools.partial(body, tq=tq, sub=sub, hb=hb, half_w=half_w, n_sub=n_sub, slopes=slopes,
                          dist_scale=dist_scale, n_kv=n_kv, has_sink=sink is not None, emit_lse=emit_lse),
        grid=(nb, n_sub // tq),
        in_specs=in_specs,
        out_specs=out_specs,
        out_shape=out_shape,
        compiler_params=_params("parallel", "arbitrary"),
        name=f"band_attn_n{n_sub}_w{half_w}",
    )(*args)


def _merge_kernel(o0_ref, o1_ref, o2_ref, l0_ref, l1_ref, l2_ref, out_ref, ob1_ref, ob2_ref, lb1_ref, lb2_ref):
    tm = out_ref.shape[0]
    for src_ref, dst_ref in ((o1_ref, ob1_ref), (o2_ref, ob2_ref), (l1_ref, lb1_ref), (l2_ref, lb2_ref)):
        r = src_ref.shape[0]
        for p in range(r):
            for h in range(dst_ref.shape[0]):
                dst_ref[h, pl.ds(p, tm // r, stride=r), :] = src_ref[p, :, h * HEAD_DIM:(h + 1) * HEAD_DIM]
    l0, l1, l2 = l0_ref[...], lb1_ref[0], lb2_ref[0]
    mx = jnp.maximum(jnp.maximum(l0, l1), l2)
    w0, w1, w2 = jnp.exp2(l0 - mx), jnp.exp2(l1 - mx), jnp.exp2(l2 - mx)
    den = w0 + w1 + w2
    a0, a1, a2 = w0 / den, w1 / den, w2 / den
    for h in range(A_HEADS_PER_GROUP):
        sl = slice(h * HEAD_DIM, (h + 1) * HEAD_DIM)
        out = a0[:, h:h + 1] * o0_ref[:, sl] + a1[:, h:h + 1] * ob1_ref[h] + a2[:, h:h + 1] * ob2_ref[h]
        out_ref[:, sl] = out.astype(BF16)


def _merge_groups(outs, lses, bsz, seq):
    n = bsz * seq
    tm = _tile(seq, MERGE_ROW_TILE)
    nseq = seq // tm
    r1, r2 = A_GROUPS[1][1], A_GROUPS[2][1]

    def specs(w):
        return [
            pl.BlockSpec((tm, w), lambda i: (i, 0)),
            pl.BlockSpec((None, r1, tm // r1, w), lambda i: (i // nseq, 0, i % nseq, 0)),
            pl.BlockSpec((None, r2, tm // r2, w), lambda i: (i // nseq, 0, i % nseq, 0)),
        ]

    def views(xs, w):
        return [xs[0].reshape(n, w), xs[1].reshape(bsz, r1, seq // r1, w), xs[2].reshape(bsz, r2, seq // r2, w)]

    return pl.pallas_call(
        _merge_kernel,
        grid=(n // tm,),
        in_specs=specs(A_OUT) + specs(HEAD_DIM),
        out_specs=pl.BlockSpec((tm, A_OUT), lambda i: (i, 0)),
        out_shape=jax.ShapeDtypeStruct((n, A_OUT), BF16),
        scratch_shapes=[pltpu.VMEM((A_HEADS_PER_GROUP, tm, HEAD_DIM), F32),
                        pltpu.VMEM((A_HEADS_PER_GROUP, tm, HEAD_DIM), F32),
                        pltpu.VMEM((1, tm, HEAD_DIM), F32), pltpu.VMEM((1, tm, HEAD_DIM), F32)],
        compiler_params=_params("parallel"),
        name="merge_dilated",
    )(*views(outs, A_OUT), *views(lses, HEAD_DIM))


def _dense_kernel(q_ref, k_ref, v_ref, o_ref, qt_ref, st_ref, m_ref, l_ref, acc_ref, *, tq, tk, seq):
    group = B_Q_HEADS // B_KV_HEADS
    n_blk = seq // tk
    for g in range(group):
        qg = q_ref[:, g * HEAD_DIM:(g + 1) * HEAD_DIM].astype(F32)
        qt_ref[:, g * tq:(g + 1) * tq] = qg.T.astype(BF16)

    def scores(blk, size):
        k0 = pl.multiple_of(blk * size, size)
        return jnp.dot(k_ref[pl.ds(k0, size), :], qt_ref[...], preferred_element_type=F32)

    def values(blk, size):
        v0 = pl.multiple_of(blk * size, size)
        return v_ref[pl.ds(v0, size), :]

    def pv(v, p):
        return lax.dot_general(v, p.astype(BF16), (((0,), (0,)), ((), ())), preferred_element_type=F32)

    def write_out():
        out = acc_ref[...] / l_ref[...]
        for g in range(group):
            o_ref[:, g * HEAD_DIM:(g + 1) * HEAD_DIM] = out[:, g * tq:(g + 1) * tq].T.astype(BF16)

    first = jnp.dot(k_ref[pl.ds(0, min(tk, HEAD_DIM)), :], qt_ref[...], preferred_element_type=F32)
    m_ref[...] = jnp.max(first, axis=0, keepdims=True)
    l_ref[...] = jnp.zeros(l_ref.shape, F32)
    acc_ref[...] = jnp.zeros(acc_ref.shape, F32)

    def lazy_step(blk):
        k0 = pl.multiple_of(blk * tk, tk)
        k = k_ref[pl.ds(k0, tk), :]
        v = values(blk, tk)
        width = LANE_CHUNK_W
        chunks = [slice(c * width, (c + 1) * width) for c in range(group * tq // width)]
        sts = [jnp.dot(k, qt_ref[:, sl], preferred_element_type=F32) for sl in chunks]
        for sl, st in zip(chunks, sts):
            p = jnp.exp2(st - m_ref[:, sl])
            l_ref[:, sl] += jnp.sum(p, axis=0, keepdims=True)
            acc_ref[:, sl] += pv(v, p)

    def lazy_body(c, carry):
        lazy_step(2 * c)
        lazy_step(2 * c + 1)
        return carry

    lax.fori_loop(0, n_blk // 2, lazy_body, 0)
    l = l_ref[...]
    sums_ok = jnp.min(jnp.where(jnp.abs(acc_ref[...]) < F32_HUGE, 1.0, 0.0))
    sums_ok = sums_ok * jnp.min(jnp.where((l > 0.0) & (l < F32_HUGE), 1.0, 0.0))
    exceeded = sums_ok < 0.5

    @pl.when(jnp.logical_not(exceeded))
    def _():
        write_out()

    @pl.when(exceeded)
    def _():
        _dense_online(scores, values, pv, write_out, st_ref, m_ref, l_ref, acc_ref, seq)


def _dense_online(scores, values, pv, write_out, st_ref, m_ref, l_ref, acc_ref, seq):
    size = st_ref.shape[1]
    n_blk = seq // size
    m_ref[...] = jnp.full(m_ref.shape, -jnp.inf, F32)
    l_ref[...] = jnp.zeros(l_ref.shape, F32)
    acc_ref[...] = jnp.zeros(acc_ref.shape, F32)
    st_ref[0] = scores(0, size)

    def step(blk, cur, nxt):
        st_ref[nxt] = scores(jnp.minimum(blk + 1, n_blk - 1), size)
        st = st_ref[cur]
        m_prev = m_ref[...]
        m_new = jnp.maximum(m_prev, jnp.max(st, axis=0, keepdims=True))
        alpha = jnp.exp2(m_prev - m_new)
        p = jnp.exp2(st - m_new)
        l_ref[...] = alpha * l_ref[...] + jnp.sum(p, axis=0, keepdims=True)
        acc_ref[...] = alpha * acc_ref[...] + pv(values(blk, size), p)
        m_ref[...] = m_new

    def body(c, carry):
        step(2 * c, 0, 1)
        step(2 * c + 1, 1, 0)
        return carry

    lax.fori_loop(0, n_blk // 2, body, 0)
    write_out()


def _dense_attention(arr):
    bsz, seq, _ = arr.shape
    tq = _tile(seq, DENSE_Q_TILE)
    tk = _tile(seq // 2, DENSE_K_TILE)
    assert (seq // tk) % 2 == 0
    group = B_Q_HEADS // B_KV_HEADS
    qw = group * HEAD_DIM
    out = pl.pallas_call(
        functools.partial(_dense_kernel, tq=tq, tk=tk, seq=seq),
        grid=(bsz, B_KV_HEADS, seq // tq),
        in_specs=[
            pl.BlockSpec((None, tq, qw), lambda b, h, i: (b, i, h)),
            pl.BlockSpec((None, seq, HEAD_DIM), lambda b, h, i: (b, 0, B_K0 // HEAD_DIM + h)),
            pl.BlockSpec((None, seq, HEAD_DIM), lambda b, h, i: (b, 0, B_V0 // HEAD_DIM + h)),
        ],
        out_specs=pl.BlockSpec((None, tq, qw), lambda b, h, i: (b, i, h)),
        out_shape=jax.ShapeDtypeStruct((bsz, seq, B_OUT), BF16),
        scratch_shapes=[
            pltpu.VMEM((HEAD_DIM, group * tq), BF16),
            pltpu.VMEM((2, _tile(seq // 2, DENSE_ONLINE_K_TILE), group * tq), F32),
            pltpu.VMEM((1, group * tq), F32),
            pltpu.VMEM((1, group * tq), F32),
            pltpu.VMEM((HEAD_DIM, group * tq), F32),
        ],
        compiler_params=_params("parallel", "parallel", "arbitrary"),
        name="dense_attn",
    )(arr, arr, arr)
    return out.reshape(bsz * seq, B_OUT)


def _out_kernel(x_ref, u_ref, wg0_ref, wg1_ref, wg2_ref, oa_ref, ob_ref, oc_ref,
                wa_ref, wb_ref, wc_ref, wo_ref, o_ref):
    j = pl.program_id(1)

    @pl.when(j == 0)
    def _():
        o_ref[...] = x_ref[...]

    u = u_ref[...]

    def branch(wg_ref, br_ref, w_ref):
        gate = jax.nn.sigmoid(jnp.dot(u, wg_ref[...], preferred_element_type=F32))
        return gate * jnp.dot(br_ref[...], w_ref[...], preferred_element_type=F32)

    merged = branch(wg0_ref, oa_ref, wa_ref) + branch(wg1_ref, ob_ref, wb_ref) + branch(wg2_ref, oc_ref, wc_ref)
    o_ref[...] += jnp.dot(merged.astype(BF16), wo_ref[...], preferred_element_type=F32)


def _mixer_out(x, u, w_in, o_a, o_b, o_c, w_a, w_b, w_c, w_o, layer):
    n, d = x.shape
    tm = _tile(n, OUT_ROW_TILE)
    tc = _tile(d, OUT_COL_TILE)
    assert QKV_COLS % tc == 0
    g0 = QKV_COLS // tc
    gstep = d // tc

    def gate_spec(br):
        return pl.BlockSpec((None, d, tc), lambda i, j: (layer, 0, g0 + br * gstep + j))

    return pl.pallas_call(
        _out_kernel,
        grid=(n // tm, d // tc),
        in_specs=[
            pl.BlockSpec((tm, d), lambda i, j: (i, 0)),
            pl.BlockSpec((tm, d), lambda i, j: (i, 0)),
            gate_spec(0), gate_spec(1), gate_spec(2),
            pl.BlockSpec((tm, A_OUT), lambda i, j: (i, 0)),
            pl.BlockSpec((tm, B_OUT), lambda i, j: (i, 0)),
            pl.BlockSpec((tm, C_OUT), lambda i, j: (i, 0)),
            pl.BlockSpec((None, A_OUT, tc), lambda i, j: (layer, 0, j)),
            pl.BlockSpec((None, B_OUT, tc), lambda i, j: (layer, 0, j)),
            pl.BlockSpec((None, C_OUT, tc), lambda i, j: (layer, 0, j)),
            pl.BlockSpec((None, tc, d), lambda i, j: (layer, j, 0)),
        ],
        out_specs=pl.BlockSpec((tm, d), lambda i, j: (i, 0)),
        out_shape=jax.ShapeDtypeStruct((n, d), F32),
        compiler_params=_params("parallel", "arbitrary"),
        name="mixer_out",
    )(x, u, w_in, w_in, w_in, o_a, o_b, o_c, w_a, w_b, w_c, w_o)


def _rope_tables(seq):
    n_freq = HEAD_DIM // 4
    t = lax.broadcasted_iota(jnp.int32, (seq, HEAD_DIM), 0)
    lane = lax.broadcasted_iota(jnp.int32, (seq, HEAD_DIM), 1)
    pair = lane // 2
    pos = jnp.where(pair < n_freq, t // GRID_W, t % GRID_W).astype(F32)
    inv = ROPE_THETA ** (-(pair % n_freq).astype(F32) / n_freq)
    ang = pos * inv
    sign = jnp.where(lane % 2 == 0, -1.0, 1.0).astype(F32)
    return jnp.cos(ang), jnp.sin(ang) * sign


def _mixer(x, lw, cos, sin_signed, bsz, seq, layer):
    a0, a1, a2, bqkv, cqkv, u = _qkv(x, lw["norm_mix"], lw["w_in"], cos, sin_signed, lw["qk_norm_q"],
                                     lw["qk_norm_k"], bsz, seq, layer)
    a_slopes = _alibi_slopes(A_HEADS)
    outs, lses = [], []
    for gi, (arr, (window, r)) in enumerate(zip((a0, a1, a2), A_GROUPS)):
        hs = slice(gi * A_HEADS_PER_GROUP, (gi + 1) * A_HEADS_PER_GROUP)
        o, l = _band_attention(
            arr.reshape(bsz * r, seq // r, PROJ_W), q_col=0, k_col=GROUP_W, v_col=2 * GROUP_W,
            n_heads=A_HEADS_PER_GROUP, n_kv=A_HEADS_PER_GROUP, half_w=(window // 2) // r, slopes=a_slopes[hs],
            dist_scale=r, sink=None, emit_lse=True, out_dtype=F32)
        outs.append(o)
        lses.append(l)
    o_a = _merge_groups(outs, lses, bsz, seq)
    o_b = _dense_attention(bqkv.reshape(bsz, seq, PROJ_W))
    o_c, = _band_attention(
        cqkv.reshape(bsz, seq, PROJ_W), q_col=0, k_col=C_K0, v_col=C_V0, n_heads=C_Q_HEADS, n_kv=C_KV_HEADS,
        half_w=C_HALF_WINDOW, slopes=_alibi_slopes(C_Q_HEADS), dist_scale=1, sink=lw["sink_c"][layer],
        emit_lse=False, out_dtype=BF16)
    return _mixer_out(x, u, lw["w_in"], o_a, o_b, o_c.reshape(bsz * seq, C_OUT), lw["w_br_a"],
                      lw["w_br_b"], lw["w_br_c"], lw["w_out"], layer)


def _trunk(x, lw, norm_final, depth):
    bsz, seq, d = x.shape
    cos, sin_signed = _rope_tables(seq)
    xf = x.reshape(bsz * seq, d)
    for layer in range(depth):
        xf = _ffn(xf, lw["norm_ffn1"], lw["ffn1_w_in"], lw["ffn1_w_out"], layer)
        xf = _mixer(xf, lw, cos, sin_signed, bsz, seq, layer)
        last = layer == depth - 1
        xf = _ffn(xf, lw["norm_ffn2"], lw["ffn2_w_in"], lw["ffn2_w_out"], layer,
                  final_gain=norm_final.reshape(1, d) if last else None)
    return xf.reshape(bsz, seq, d)


def kernel(x_prompt, x_sample, norm_ffn1, ffn1_w_in, ffn1_w_out, norm_mix, w_in, qk_norm_q, qk_norm_k, sink_c,
           w_br_a, w_br_b, w_br_c, w_out, norm_ffn2, ffn2_w_in, ffn2_w_out, norm_final):
    depth = norm_ffn1.shape[0]
    row = lambda v: v.reshape(depth, 1, v.shape[-1])
    layers = {
        "norm_ffn1": row(norm_ffn1), "ffn1_w_in": ffn1_w_in.astype(BF16), "ffn1_w_out": ffn1_w_out.astype(BF16),
        "norm_mix": row(norm_mix), "w_in": w_in.astype(BF16),
        "qk_norm_q": row(qk_norm_q), "qk_norm_k": row(qk_norm_k), "sink_c": row(sink_c),
        "w_br_a": w_br_a.astype(BF16), "w_br_b": w_br_b.astype(BF16), "w_br_c": w_br_c.astype(BF16),
        "w_out": w_out.astype(BF16),
        "norm_ffn2": row(norm_ffn2), "ffn2_w_in": ffn2_w_in.astype(BF16), "ffn2_w_out": ffn2_w_out.astype(BF16),
    }
    assert x_prompt.shape[1:] == x_sample.shape[1:]
    n_prompt = x_prompt.shape[0]
    y = _trunk(jnp.concatenate([x_prompt, x_sample], axis=0), layers, norm_final, depth)
    return (y[:n_prompt], y[n_prompt:])
```

```python
import functools

import jax
import jax.numpy as jnp
from jax import lax
from jax.experimental import pallas as pl
from jax.experimental.pallas import tpu as pltpu

F32 = jnp.float32
BF16 = jnp.bfloat16

HEAD_DIM = 128
SCALE = HEAD_DIM ** -0.5
LOG2_E = 1.4426950408889634
Q_SCALE = SCALE * LOG2_E
NORM_EPS = 1e-6
NEG_INF = -1e30
F32_HUGE = 1e38
LANE_CHUNK_W = 256
ROPE_THETA = 10000.0
GRID_W = 64
A_GROUPS = ((128, 1), (512, 4), (2048, 16))
A_HEADS_PER_GROUP = 4
A_HEADS = 12
B_Q_HEADS = 8
B_KV_HEADS = 2
C_Q_HEADS = 8
C_KV_HEADS = 2
C_HALF_WINDOW = 128
N_BRANCHES = 3

QKV_COLS = 7680
PROJ_W = 1536
GROUP_W = 4 * HEAD_DIM
A_OUT = A_HEADS_PER_GROUP * HEAD_DIM
B_OUT = B_Q_HEADS * HEAD_DIM
C_OUT = C_Q_HEADS * HEAD_DIM
B_K0, B_V0 = B_OUT, B_OUT + B_KV_HEADS * HEAD_DIM
C_K0, C_V0 = C_OUT, C_OUT + C_KV_HEADS * HEAD_DIM

V7X_VMEM_LIMIT_BYTES = 56 * 1024 * 1024

FFN_ROW_TILE, FFN_HIDDEN_TILE = 1024, 512
FFN_NORM_CHUNKS = 4
PROJ_ROW_TILE = 512
OUT_ROW_TILE, OUT_COL_TILE = 512, 512
DENSE_Q_TILE, DENSE_K_TILE = 2048, 512
DENSE_ONLINE_K_TILE = 128
BAND_Q_TILE = 2048
MERGE_ROW_TILE = 1024


def _tile(n, pref):
    t = min(pref, n)
    while n % t:
        t //= 2
    return t


def _params(*sem):
    return pltpu.CompilerParams(dimension_semantics=sem, vmem_limit_bytes=V7X_VMEM_LIMIT_BYTES)


def _rms(x, g):
    ms = jnp.mean(x * x, axis=-1, keepdims=True)
    return x * lax.rsqrt(ms + NORM_EPS) * g


def _alibi_slopes(n):
    return tuple(2.0 ** (-8.0 * i / n) for i in range(1, n + 1))


def _ffn_kernel(x_ref, g_ref, wa_ref, wb_ref, wo_ref, *rest, final_norm):
    o_ref, xn_ref = rest[-2:]
    j = pl.program_id(1)

    def half_swiglu(xn):
        a = jnp.dot(xn, wa_ref[...], preferred_element_type=F32)
        b = jnp.dot(xn, wb_ref[...], preferred_element_type=F32)
        h = (0.5 * a * jax.nn.sigmoid(a) * b).astype(BF16)
        return jnp.dot(h, wo_ref[...], preferred_element_type=F32)

    @pl.when(j == 0)
    def _():
        rows = x_ref.shape[0] // FFN_NORM_CHUNKS
        for c in range(FFN_NORM_CHUNKS):
            sl = slice(c * rows, (c + 1) * rows)
            x = x_ref[sl, :]
            xn = _rms(x, g_ref[...]).astype(BF16)
            xn_ref[sl, :] = xn
            o_ref[sl, :] = x + half_swiglu(xn)

    @pl.when(j > 0)
    def _():
        o_ref[...] += half_swiglu(xn_ref[...])

    if final_norm:
        @pl.when(j == pl.num_programs(1) - 1)
        def _():
            o_ref[...] = _rms(o_ref[...], rest[0][...])


def _ffn(x, g, w_in, w_out, layer, final_gain=None):
    n, d = x.shape
    d_ff = w_out.shape[1]
    tm = _tile(n, FFN_ROW_TILE)
    tf = _tile(d_ff, FFN_HIDDEN_TILE)
    nf = d_ff // tf
    in_specs = [
        pl.BlockSpec((tm, d), lambda i, j: (i, 0)),
        pl.BlockSpec((None, 1, d), lambda i, j: (layer, 0, 0)),
        pl.BlockSpec((None, d, tf), lambda i, j: (layer, 0, j)),
        pl.BlockSpec((None, d, tf), lambda i, j: (layer, 0, j + nf)),
        pl.BlockSpec((None, tf, d), lambda i, j: (layer, j, 0)),
    ]
    args = [x, g, w_in, w_in, w_out]
    if final_gain is not None:
        in_specs.append(pl.BlockSpec((1, d), lambda i, j: (0, 0)))
        args.append(final_gain)
    return pl.pallas_call(
        functools.partial(_ffn_kernel, final_norm=final_gain is not None),
        grid=(n // tm, nf),
        in_specs=in_specs,
        out_specs=pl.BlockSpec((tm, d), lambda i, j: (i, 0)),
        out_shape=jax.ShapeDtypeStruct((n, d), F32),
        scratch_shapes=[pltpu.VMEM((tm, d), BF16)],
        compiler_params=_params("parallel", "arbitrary"),
        name="ffn",
    )(*args)


def _norm_rope(y, gain, cos, sin_signed):
    yn = _rms(y, gain)
    lane = lax.broadcasted_iota(jnp.int32, yn.shape, 1)
    partner = jnp.where((lane & 1) == 0, pltpu.roll(yn, HEAD_DIM - 1, 1), pltpu.roll(yn, 1, 1))
    return yn * cos + partner * sin_signed


def _qkv_kernel(x_ref, g_ref, w_ref, cos_ref, sin_ref, qg_ref, kg_ref,
                a0_ref, a1_ref, a2_ref, b_ref, c_ref, u_ref, split_ref):
    j = pl.program_id(1)
    tm = x_ref.shape[0]

    @pl.when(j == 0)
    def _():
        u_ref[...] = _rms(x_ref[...], g_ref[...]).astype(BF16)

    def proj(c0, width):
        return jnp.dot(u_ref[...], w_ref[:, c0:c0 + width], preferred_element_type=F32)

    @pl.when(j < 3)
    def _():
        sc = jnp.where(j == 0, Q_SCALE, 1.0).astype(F32)
        a0_ref[...] = (proj(0, GROUP_W) * sc).astype(BF16)
        for gi, dst_ref in ((1, a1_ref), (2, a2_ref)):
            r = A_GROUPS[gi][1]
            y = proj(gi * GROUP_W, GROUP_W) * sc
            for h in range(A_HEADS_PER_GROUP):
                split_ref[(gi - 1) * A_HEADS_PER_GROUP + h] = y[:, h * HEAD_DIM:(h + 1) * HEAD_DIM]
            for p in range(r):
                for h in range(A_HEADS_PER_GROUP):
                    rows = split_ref[(gi - 1) * A_HEADS_PER_GROUP + h, pl.ds(p, tm // r, stride=r), :]
                    dst_ref[p, :, h * HEAD_DIM:(h + 1) * HEAD_DIM] = rows.astype(BF16)

    @pl.when(j == 3)
    def _():
        pair = 2 * HEAD_DIM
        for c0 in range(0, B_V0, pair):
            acc = proj(c0, pair)
            for h0 in range(0, pair, HEAD_DIM):
                is_q = c0 + h0 < B_K0
                y = _norm_rope(acc[:, h0:h0 + HEAD_DIM], (qg_ref if is_q else kg_ref)[...], cos_ref[...],
                               sin_ref[...])
                b_ref[:, c0 + h0:c0 + h0 + HEAD_DIM] = ((y * Q_SCALE) if is_q else y).astype(BF16)
        b_ref[:, B_V0:] = proj(B_V0, PROJ_W - B_V0).astype(BF16)

    @pl.when(j == 4)
    def _():
        for c0 in range(0, C_OUT, GROUP_W):
            c_ref[:, c0:c0 + GROUP_W] = (proj(c0, GROUP_W) * Q_SCALE).astype(BF16)
        c_ref[:, C_OUT:] = proj(C_OUT, PROJ_W - C_OUT).astype(BF16)


def _qkv(x, g, w_in, cos, sin_signed, qg, kg, bsz, seq, layer):
    n, d = x.shape
    tm = _tile(seq, PROJ_ROW_TILE)
    nseq = seq // tm
    r1, r2 = A_GROUPS[1][1], A_GROUPS[2][1]
    assert tm % (16 * r2) == 0
    a_part = lambda i, j: jnp.minimum(j, 2)

    def split_spec(r):
        return pl.BlockSpec((None, r, tm // r, GROUP_W), lambda i, j: (i // nseq, 0, i % nseq, a_part(i, j)))

    return pl.pallas_call(
        _qkv_kernel,
        grid=(n // tm, QKV_COLS // PROJ_W),
        in_specs=[
            pl.BlockSpec((tm, d), lambda i, j: (i, 0)),
            pl.BlockSpec((None, 1, d), lambda i, j: (layer, 0, 0)),
            pl.BlockSpec((None, d, PROJ_W), lambda i, j: (layer, 0, j)),
            pl.BlockSpec((tm, HEAD_DIM), lambda i, j: (i % nseq, 0)),
            pl.BlockSpec((tm, HEAD_DIM), lambda i, j: (i % nseq, 0)),
            pl.BlockSpec((None, 1, HEAD_DIM), lambda i, j: (layer, 0, 0)),
            pl.BlockSpec((None, 1, HEAD_DIM), lambda i, j: (layer, 0, 0)),
        ],
        out_specs=[
            pl.BlockSpec((tm, GROUP_W), lambda i, j: (i, a_part(i, j))),
            split_spec(r1),
            split_spec(r2),
            pl.BlockSpec((tm, PROJ_W), lambda i, j: (i, 0)),
            pl.BlockSpec((tm, PROJ_W), lambda i, j: (i, 0)),
            pl.BlockSpec((tm, d), lambda i, j: (i, 0)),
        ],
        out_shape=[
            jax.ShapeDtypeStruct((n, PROJ_W), BF16),
            jax.ShapeDtypeStruct((bsz, r1, seq // r1, PROJ_W), BF16),
            jax.ShapeDtypeStruct((bsz, r2, seq // r2, PROJ_W), BF16),
            jax.ShapeDtypeStruct((n, PROJ_W), BF16),
            jax.ShapeDtypeStruct((n, PROJ_W), BF16),
            jax.ShapeDtypeStruct((n, d), BF16),
        ],
        scratch_shapes=[pltpu.VMEM(((PROJ_W - GROUP_W) // HEAD_DIM, tm, HEAD_DIM), F32)],
        compiler_params=_params("parallel", "arbitrary"),
        name="qkv_proj",
    )(x, g, w_in, cos, sin_signed, qg, kg)


def _band_kernel(*refs, tq, sub, hb, half_w, n_sub, slopes, dist_scale, n_kv, has_sink, emit_lse):
    if has_sink:
        sink_ref, refs = refs[0], refs[1:]
    q_ref, kp_ref, km_ref, kn_ref, vp_ref, vm_ref, vn_ref, o_ref = refs[:8]
    n_heads = len(slopes)
    group = n_heads // n_kv
    qi = pl.program_id(1)
    span = sub + 2 * hb
    row = lax.broadcasted_iota(jnp.int32, (sub, span), 0)
    col = lax.broadcasted_iota(jnp.int32, (sub, span), 1)
    arel = jnp.abs(col - hb - row)
    in_band = arel <= half_w
    dist = (arel * dist_scale).astype(F32)
    biases = [jnp.where(in_band, (-slopes[h] * LOG2_E) * dist, NEG_INF) for h in range(n_heads)]
    kcat, vcat = [], []
    for kv in range(n_kv):
        ksl = slice(kv * HEAD_DIM, (kv + 1) * HEAD_DIM)
        kcat.append(jnp.concatenate([kp_ref[:, ksl], km_ref[:, ksl], kn_ref[:, ksl]], axis=0))
        vcat.append(jnp.concatenate([vp_ref[:, ksl], vm_ref[:, ksl], vn_ref[:, ksl]], axis=0))
    if emit_lse:
        lane = lax.broadcasted_iota(jnp.int32, (sub, HEAD_DIM), 1)
    for s in range(tq // sub):
        rows = slice(s * sub, (s + 1) * sub)
        kpos = qi * tq + (s * sub - hb) + col
        in_seq = (kpos >= 0) & (kpos < n_sub)
        lse_tile = jnp.zeros((sub, HEAD_DIM), F32) if emit_lse else None
        for h in range(n_heads):
            sl = slice(h * HEAD_DIM, (h + 1) * HEAD_DIM)
            ks = kcat[h // group][s * sub:s * sub + span]
            vs = vcat[h // group][s * sub:s * sub + span]
            sc = lax.dot_general(q_ref[rows, sl], ks, (((1,), (1,)), ((), ())), preferred_element_type=F32)
            sc = jnp.where(in_seq, sc + biases[h], NEG_INF)
            m = jnp.max(sc, axis=-1, keepdims=True)
            if has_sink:
                sk = sink_ref[0, h] * LOG2_E
                m = jnp.maximum(m, sk)
            p = jnp.exp2(sc - m)
            den = jnp.sum(p, axis=-1, keepdims=True)
            if has_sink:
                den = den + jnp.exp2(sk - m)
            o = jnp.dot(p.astype(BF16), vs, preferred_element_type=F32) * (1.0 / den)
            o_ref[rows, sl] = o.astype(o_ref.dtype)
            if emit_lse:
                lse_tile = jnp.where(lane == h, m + jnp.log2(den), lse_tile)
        if emit_lse:
            refs[8][rows, :] = lse_tile


def _band_gqa_kernel(*refs, tq, sub, hb, half_w, n_sub, slopes, dist_scale, n_kv, has_sink, emit_lse):
    assert not emit_lse
    if has_sink:
        sink_ref, refs = refs[0], refs[1:]
    q_ref, kp_ref, km_ref, kn_ref, vp_ref, vm_ref, vn_ref, o_ref = refs
    n_heads = len(slopes)
    group = n_heads // n_kv
    qi = pl.program_id(1)
    span = sub + 2 * hb
    head_sl = lambda h: slice(h * HEAD_DIM, (h + 1) * HEAD_DIM)
    key = lax.broadcasted_iota(jnp.int32, (span, sub), 0)
    qry = lax.broadcasted_iota(jnp.int32, (span, sub), 1)
    arel = jnp.abs(key - hb - qry)
    in_band = arel <= half_w
    dist = (arel * dist_scale).astype(F32)
    key_wide = lax.broadcasted_iota(jnp.int32, (span, group * sub), 0)
    biases, sinks, kcat, vcat = [], [], [], []
    for kv in range(n_kv):
        heads = range(kv * group, (kv + 1) * group)
        biases.append(jnp.concatenate(
            [jnp.where(in_band, (-slopes[h] * LOG2_E) * dist, NEG_INF) for h in heads], axis=1))
        if has_sink:
            sinks.append(jnp.concatenate(
                [jnp.full((1, sub), sink_ref[0, h] * LOG2_E, F32) for h in heads], axis=1))
        ksl = head_sl(kv)
        kcat.append(jnp.concatenate([kp_ref[:, ksl], km_ref[:, ksl], kn_ref[:, ksl]], axis=0))
        vcat.append(jnp.concatenate([vp_ref[:, ksl], vm_ref[:, ksl], vn_ref[:, ksl]], axis=0))
    for s in range(tq // sub):
        rows = slice(s * sub, (s + 1) * sub)
        kpos = qi * tq + (s * sub - hb) + key_wide
        in_seq = (kpos >= 0) & (kpos < n_sub)
        for kv in range(n_kv):
            heads = range(kv * group, (kv + 1) * group)
            ks = kcat[kv][s * sub:s * sub + span]
            vs = vcat[kv][s * sub:s * sub + span]
            q = jnp.concatenate([q_ref[rows, head_sl(h)] for h in heads], axis=0)
            st = lax.dot_general(ks, q, (((1,), (1,)), ((), ())), preferred_element_type=F32)
            st = jnp.where(in_seq, st + biases[kv], NEG_INF)
            m = jnp.max(st, axis=0, keepdims=True)
            if has_sink:
                m = jnp.maximum(m, sinks[kv])
            p = jnp.exp2(st - m)
            den = jnp.sum(p, axis=0, keepdims=True)
            if has_sink:
                den = den + jnp.exp2(sinks[kv] - m)
            ot = lax.dot_general(vs, p.astype(BF16), (((0,), (0,)), ((), ())), preferred_element_type=F32)
            ot = ot * (1.0 / den)
            for gi, h in enumerate(heads):
                o_ref[rows, head_sl(h)] = ot[:, gi * sub:(gi + 1) * sub].T.astype(o_ref.dtype)


def _band_attention(arr, *, q_col, k_col, v_col, n_heads, n_kv, half_w, slopes, dist_scale, sink, emit_lse,
                    out_dtype):
    nb, n_sub, _ = arr.shape
    hb = half_w
    sub = min(HEAD_DIM, n_sub)
    tq = _tile(n_sub, BAND_Q_TILE)
    assert sub % hb == 0 and tq % sub == 0 and n_sub % tq == 0
    per = tq // hb
    n_hb = n_sub // hb
    qw, kw = n_heads * HEAD_DIM, n_kv * HEAD_DIM
    assert q_col % qw == 0 and k_col % kw == 0 and v_col % kw == 0

    def halo_specs(c):
        return [
            pl.BlockSpec((None, hb, kw), lambda b, i: (b, jnp.maximum(i * per - 1, 0), c // kw)),
            pl.BlockSpec((None, tq, kw), lambda b, i: (b, i, c // kw)),
            pl.BlockSpec((None, hb, kw), lambda b, i: (b, jnp.minimum((i + 1) * per, n_hb - 1), c // kw)),
        ]

    in_specs = [pl.BlockSpec((None, tq, qw), lambda b, i: (b, i, q_col // qw))] + halo_specs(k_col) + halo_specs(v_col)
    args = [arr] * 7
    if sink is not None:
        in_specs = [pl.BlockSpec(memory_space=pltpu.SMEM)] + in_specs
        args = [sink] + args
    out_specs = [pl.BlockSpec((None, tq, qw), lambda b, i: (b, i, 0))]
    out_shape = [jax.ShapeDtypeStruct((nb, n_sub, qw), out_dtype)]
    if emit_lse:
        out_specs.append(pl.BlockSpec((None, tq, HEAD_DIM), lambda b, i: (b, i, 0)))
        out_shape.append(jax.ShapeDtypeStruct((nb, n_sub, HEAD_DIM), F32))
    body = _band_gqa_kernel if n_kv < n_heads else _band_kernel
    return pl.pallas_call(
        functools.partial(body, tq=tq, sub=sub, hb=hb, half_w=half_w, n_sub=n_sub, slopes=slopes,
                          dist_scale=dist_scale, n_kv=n_kv, has_sink=sink is not None, emit_lse=emit_lse),
        grid=(nb, n_sub // tq),
        in_specs=in_specs,
        out_specs=out_specs,
        out_shape=out_shape,
        compiler_params=_params("parallel", "arbitrary"),
        name=f"band_attn_n{n_sub}_w{half_w}",
    )(*args)


def _merge_kernel(o0_ref, o1_ref, o2_ref, l0_ref, l1_ref, l2_ref, out_ref, ob1_ref, ob2_ref, lb1_ref, lb2_ref):
    tm = out_ref.shape[0]
    for src_ref, dst_ref in ((o1_ref, ob1_ref), (o2_ref, ob2_ref), (l1_ref, lb1_ref), (l2_ref, lb2_ref)):
        r = src_ref.shape[0]
        for p in range(r):
            for h in range(dst_ref.shape[0]):
                dst_ref[h, pl.ds(p, tm // r, stride=r), :] = src_ref[p, :, h * HEAD_DIM:(h + 1) * HEAD_DIM]
    l0, l1, l2 = l0_ref[...], lb1_ref[0], lb2_ref[0]
    mx = jnp.maximum(jnp.maximum(l0, l1), l2)
    w0, w1, w2 = jnp.exp2(l0 - mx), jnp.exp2(l1 - mx), jnp.exp2(l2 - mx)
    den = w0 + w1 + w2
    a0, a1, a2 = w0 / den, w1 / den, w2 / den
    for h in range(A_HEADS_PER_GROUP):
        sl = slice(h * HEAD_DIM, (h + 1) * HEAD_DIM)
        out = a0[:, h:h + 1] * o0_ref[:, sl] + a1[:, h:h + 1] * ob1_ref[h] + a2[:, h:h + 1] * ob2_ref[h]
        out_ref[:, sl] = out.astype(BF16)


def _merge_groups(outs, lses, bsz, seq):
    n = bsz * seq
    tm = _tile(seq, MERGE_ROW_TILE)
    nseq = seq // tm
    r1, r2 = A_GROUPS[1][1], A_GROUPS[2][1]

    def specs(w):
        return [
            pl.BlockSpec((tm, w), lambda i: (i, 0)),
            pl.BlockSpec((None, r1, tm // r1, w), lambda i: (i // nseq, 0, i % nseq, 0)),
            pl.BlockSpec((None, r2, tm // r2, w), lambda i: (i // nseq, 0, i % nseq, 0)),
        ]

    def views(xs, w):
        return [xs[0].reshape(n, w), xs[1].reshape(bsz, r1, seq // r1, w), xs[2].reshape(bsz, r2, seq // r2, w)]

    return pl.pallas_call(
        _merge_kernel,
        grid=(n // tm,),
        in_specs=specs(A_OUT) + specs(HEAD_DIM),
        out_specs=pl.BlockSpec((tm, A_OUT), lambda i: (i, 0)),
        out_shape=jax.ShapeDtypeStruct((n, A_OUT), BF16),
        scratch_shapes=[pltpu.VMEM((A_HEADS_PER_GROUP, tm, HEAD_DIM), F32),
                        pltpu.VMEM((A_HEADS_PER_GROUP, tm, HEAD_DIM), F32),
                        pltpu.VMEM((1, tm, HEAD_DIM), F32), pltpu.VMEM((1, tm, HEAD_DIM), F32)],
        compiler_params=_params("parallel"),
        name="merge_dilated",
    )(*views(outs, A_OUT), *views(lses, HEAD_DIM))


def _dense_kernel(q_ref, k_ref, v_ref, o_ref, qt_ref, st_ref, m_ref, l_ref, acc_ref, *, tq, tk, seq):
    group = B_Q_HEADS // B_KV_HEADS
    n_blk = seq // tk
    for g in range(group):
        qg = q_ref[:, g * HEAD_DIM:(g + 1) * HEAD_DIM].astype(F32)
        qt_ref[:, g * tq:(g + 1) * tq] = qg.T.astype(BF16)

    def scores(blk, size):
        k0 = pl.multiple_of(blk * size, size)
        return jnp.dot(k_ref[pl.ds(k0, size), :], qt_ref[...], preferred_element_type=F32)

    def values(blk, size):
        v0 = pl.multiple_of(blk * size, size)
        return v_ref[pl.ds(v0, size), :]

    def pv(v, p):
        return lax.dot_general(v, p.astype(BF16), (((0,), (0,)), ((), ())), preferred_element_type=F32)

    def write_out():
        out = acc_ref[...] / l_ref[...]
        for g in range(group):
            o_ref[:, g * HEAD_DIM:(g + 1) * HEAD_DIM] = out[:, g * tq:(g + 1) * tq].T.astype(BF16)

    first = jnp.dot(k_ref[pl.ds(0, min(tk, HEAD_DIM)), :], qt_ref[...], preferred_element_type=F32)
    m_ref[...] = jnp.max(first, axis=0, keepdims=True)
    l_ref[...] = jnp.zeros(l_ref.shape, F32)
    acc_ref[...] = jnp.zeros(acc_ref.shape, F32)

    def lazy_step(blk):
        k0 = pl.multiple_of(blk * tk, tk)
        k = k_ref[pl.ds(k0, tk), :]
        v = values(blk, tk)
        width = LANE_CHUNK_W
        chunks = [slice(c * width, (c + 1) * width) for c in range(group * tq // width)]
        sts = [jnp.dot(k, qt_ref[:, sl], preferred_element_type=F32) for sl in chunks]
        for sl, st in zip(chunks, sts):
            p = jnp.exp2(st - m_ref[:, sl])
            l_ref[:, sl] += jnp.sum(p, axis=0, keepdims=True)
            acc_ref[:, sl] += pv(v, p)

    def lazy_body(c, carry):
        lazy_step(2 * c)
        lazy_step(2 * c + 1)
        return carry

    lax.fori_loop(0, n_blk // 2, lazy_body, 0)
    l = l_ref[...]
    sums_ok = jnp.min(jnp.where(jnp.abs(acc_ref[...]) < F32_HUGE, 1.0, 0.0))
    sums_ok = sums_ok * jnp.min(jnp.where((l > 0.0) & (l < F32_HUGE), 1.0, 0.0))
    exceeded = sums_ok < 0.5

    @pl.when(jnp.logical_not(exceeded))
    def _():
        write_out()

    @pl.when(exceeded)
    def _():
        _dense_online(scores, values, pv, write_out, st_ref, m_ref, l_ref, acc_ref, seq)


def _dense_online(scores, values, pv, write_out, st_ref, m_ref, l_ref, acc_ref, seq):
    size = st_ref.shape[1]
    n_blk = seq // size
    m_ref[...] = jnp.full(m_ref.shape, -jnp.inf, F32)
    l_ref[...] = jnp.zeros(l_ref.shape, F32)
    acc_ref[...] = jnp.zeros(acc_ref.shape, F32)
    st_ref[0] = scores(0, size)

    def step(blk, cur, nxt):
        st_ref[nxt] = scores(jnp.minimum(blk + 1, n_blk - 1), size)
        st = st_ref[cur]
        m_prev = m_ref[...]
        m_new = jnp.maximum(m_prev, jnp.max(st, axis=0, keepdims=True))
        alpha = jnp.exp2(m_prev - m_new)
        p = jnp.exp2(st - m_new)
        l_ref[...] = alpha * l_ref[...] + jnp.sum(p, axis=0, keepdims=True)
        acc_ref[...] = alpha * acc_ref[...] + pv(values(blk, size), p)
        m_ref[...] = m_new

    def body(c, carry):
        step(2 * c, 0, 1)
        step(2 * c + 1, 1, 0)
        return carry

    lax.fori_loop(0, n_blk // 2, body, 0)
    write_out()


def _dense_attention(arr):
    bsz, seq, _ = arr.shape
    tq = _tile(seq, DENSE_Q_TILE)
    tk = _tile(seq // 2, DENSE_K_TILE)
    assert (seq // tk) % 2 == 0
    group = B_Q_HEADS // B_KV_HEADS
    qw = group * HEAD_DIM
    out = pl.pallas_call(
        functools.partial(_dense_kernel, tq=tq, tk=tk, seq=seq),
        grid=(bsz, B_KV_HEADS, seq // tq),
        in_specs=[
            pl.BlockSpec((None, tq, qw), lambda b, h, i: (b, i, h)),
            pl.BlockSpec((None, seq, HEAD_DIM), lambda b, h, i: (b, 0, B_K0 // HEAD_DIM + h)),
            pl.BlockSpec((None, seq, HEAD_DIM), lambda b, h, i: (b, 0, B_V0 // HEAD_DIM + h)),
        ],
        out_specs=pl.BlockSpec((None, tq, qw), lambda b, h, i: (b, i, h)),
        out_shape=jax.ShapeDtypeStruct((bsz, seq, B_OUT), BF16),
        scratch_shapes=[
            pltpu.VMEM((HEAD_DIM, group * tq), BF16),
            pltpu.VMEM((2, _tile(seq // 2, DENSE_ONLINE_K_TILE), group * tq), F32),
            pltpu.VMEM((1, group * tq), F32),
            pltpu.VMEM((1, group * tq), F32),
            pltpu.VMEM((HEAD_DIM, group * tq), F32),
        ],
        compiler_params=_params("parallel", "parallel", "arbitrary"),
        name="dense_attn",
    )(arr, arr, arr)
    return out.reshape(bsz * seq, B_OUT)


def _out_kernel(x_ref, u_ref, wg0_ref, wg1_ref, wg2_ref, oa_ref, ob_ref, oc_ref,
                wa_ref, wb_ref, wc_ref, wo_ref, o_ref):
    j = pl.program_id(1)

    @pl.when(j == 0)
    def _():
        o_ref[...] = x_ref[...]

    u = u_ref[...]

    def branch(wg_ref, br_ref, w_ref):
        gate = jax.nn.sigmoid(jnp.dot(u, wg_ref[...], preferred_element_type=F32))
        return gate * jnp.dot(br_ref[...], w_ref[...], preferred_element_type=F32)

    merged = branch(wg0_ref, oa_ref, wa_ref) + branch(wg1_ref, ob_ref, wb_ref) + branch(wg2_ref, oc_ref, wc_ref)
    o_ref[...] += jnp.dot(merged.astype(BF16), wo_ref[...], preferred_element_type=F32)


def _mixer_out(x, u, w_in, o_a, o_b, o_c, w_a, w_b, w_c, w_o, layer):
    n, d = x.shape
    tm = _tile(n, OUT_ROW_TILE)
    tc = _tile(d, OUT_COL_TILE)
    assert QKV_COLS % tc == 0
    g0 = QKV_COLS // tc
    gstep = d // tc

    def gate_spec(br):
        return pl.BlockSpec((None, d, tc), lambda i, j: (layer, 0, g0 + br * gstep + j))

    return pl.pallas_call(
        _out_kernel,
        grid=(n // tm, d // tc),
        in_specs=[
            pl.BlockSpec((tm, d), lambda i, j: (i, 0)),
            pl.BlockSpec((tm, d), lambda i, j: (i, 0)),
            gate_spec(0), gate_spec(1), gate_spec(2),
            pl.BlockSpec((tm, A_OUT), lambda i, j: (i, 0)),
            pl.BlockSpec((tm, B_OUT), lambda i, j: (i, 0)),
            pl.BlockSpec((tm, C_OUT), lambda i, j: (i, 0)),
            pl.BlockSpec((None, A_OUT, tc), lambda i, j: (layer, 0, j)),
            pl.BlockSpec((None, B_OUT, tc), lambda i, j: (layer, 0, j)),
            pl.BlockSpec((None, C_OUT, tc), lambda i, j: (layer, 0, j)),
            pl.BlockSpec((None, tc, d), lambda i, j: (layer, j, 0)),
        ],
        out_specs=pl.BlockSpec((tm, d), lambda i, j: (i, 0)),
        out_shape=jax.ShapeDtypeStruct((n, d), F32),
        compiler_params=_params("parallel", "arbitrary"),
        name="mixer_out",
    )(x, u, w_in, w_in, w_in, o_a, o_b, o_c, w_a, w_b, w_c, w_o)


def _rope_tables(seq):
    n_freq = HEAD_DIM // 4
    t = lax.broadcasted_iota(jnp.int32, (seq, HEAD_DIM), 0)
    lane = lax.broadcasted_iota(jnp.int32, (seq, HEAD_DIM), 1)
    pair = lane // 2
    pos = jnp.where(pair < n_freq, t // GRID_W, t % GRID_W).astype(F32)
    inv = ROPE_THETA ** (-(pair % n_freq).astype(F32) / n_freq)
    ang = pos * inv
    sign = jnp.where(lane % 2 == 0, -1.0, 1.0).astype(F32)
    return jnp.cos(ang), jnp.sin(ang) * sign


def _mixer(x, lw, cos, sin_signed, bsz, seq, layer):
    a0, a1, a2, bqkv, cqkv, u = _qkv(x, lw["norm_mix"], lw["w_in"], cos, sin_signed, lw["qk_norm_q"],
                                     lw["qk_norm_k"], bsz, seq, layer)
    a_slopes = _alibi_slopes(A_HEADS)
    outs, lses = [], []
    for gi, (arr, (window, r)) in enumerate(zip((a0, a1, a2), A_GROUPS)):
        hs = slice(gi * A_HEADS_PER_GROUP, (gi + 1) * A_HEADS_PER_GROUP)
        o, l = _band_attention(
            arr.reshape(bsz * r, seq // r, PROJ_W), q_col=0, k_col=GROUP_W, v_col=2 * GROUP_W,
            n_heads=A_HEADS_PER_GROUP, n_kv=A_HEADS_PER_GROUP, half_w=(window // 2) // r, slopes=a_slopes[hs],
            dist_scale=r, sink=None, emit_lse=True, out_dtype=F32)
        outs.append(o)
        lses.append(l)
    o_a = _merge_groups(outs, lses, bsz, seq)
    o_b = _dense_attention(bqkv.reshape(bsz, seq, PROJ_W))
    o_c, = _band_attention(
        cqkv.reshape(bsz, seq, PROJ_W), q_col=0, k_col=C_K0, v_col=C_V0, n_heads=C_Q_HEADS, n_kv=C_KV_HEADS,
        half_w=C_HALF_WINDOW, slopes=_alibi_slopes(C_Q_HEADS), dist_scale=1, sink=lw["sink_c"][layer],
        emit_lse=False, out_dtype=BF16)
    return _mixer_out(x, u, lw["w_in"], o_a, o_b, o_c.reshape(bsz * seq, C_OUT), lw["w_br_a"],
                      lw["w_br_b"], lw["w_br_c"], lw["w_out"], layer)


def _trunk(x, lw, norm_final, depth):
    bsz, seq, d = x.shape
    cos, sin_signed = _rope_tables(seq)
    xf = x.reshape(bsz * seq, d)
    for layer in range(depth):
        xf = _ffn(xf, lw["norm_ffn1"], lw["ffn1_w_in"], lw["ffn1_w_out"], layer)
        xf = _mixer(xf, lw, cos, sin_signed, bsz, seq, layer)
        last = layer == depth - 1
        xf = _ffn(xf, lw["norm_ffn2"], lw["ffn2_w_in"], lw["ffn2_w_out"], layer,
                  final_gain=norm_final.reshape(1, d) if last else None)
    return xf.reshape(bsz, seq, d)


def kernel(x_prompt, x_sample, norm_ffn1, ffn1_w_in, ffn1_w_out, norm_mix, w_in, qk_norm_q, qk_norm_k, sink_c,
           w_br_a, w_br_b, w_br_c, w_out, norm_ffn2, ffn2_w_in, ffn2_w_out, norm_final):
    depth = norm_ffn1.shape[0]
    row = lambda v: v.reshape(depth, 1, v.shape[-1])
    layers = {
        "norm_ffn1": row(norm_ffn1), "ffn1_w_in": ffn1_w_in.astype(BF16), "ffn1_w_out": ffn1_w_out.astype(BF16),
        "norm_mix": row(norm_mix), "w_in": w_in.astype(BF16),
        "qk_norm_q": row(qk_norm_q), "qk_norm_k": row(qk_norm_k), "sink_c": row(sink_c),
        "w_br_a": w_br_a.astype(BF16), "w_br_b": w_br_b.astype(BF16), "w_br_c": w_br_c.astype(BF16),
        "w_out": w_out.astype(BF16),
        "norm_ffn2": row(norm_ffn2), "ffn2_w_in": ffn2_w_in.astype(BF16), "ffn2_w_out": ffn2_w_out.astype(BF16),
    }
    assert x_prompt.shape[1:] == x_sample.shape[1:]
    n_prompt = x_prompt.shape[0]
    y = _trunk(jnp.concatenate([x_prompt, x_sample], axis=0), layers, norm_final, depth)
    return (y[:n_prompt], y[n_prompt:])
```

```python
import functools

import jax
import jax.numpy as jnp
from jax import lax
from jax.experimental import pallas as pl
from jax.experimental.pallas import tpu as pltpu

F32 = jnp.float32
BF16 = jnp.bfloat16

HEAD_DIM = 128
SCALE = HEAD_DIM ** -0.5
LOG2_E = 1.4426950408889634
Q_SCALE = SCALE * LOG2_E
NORM_EPS = 1e-6
NEG_INF = -1e30
F32_HUGE = 1e38
LANE_CHUNK_W = 256
ROPE_THETA = 10000.0
GRID_W = 64
A_GROUPS = ((128, 1), (512, 4), (2048, 16))
A_HEADS_PER_GROUP = 4
A_HEADS = 12
B_Q_HEADS = 8
B_KV_HEADS = 2
C_Q_HEADS = 8
C_KV_HEADS = 2
C_HALF_WINDOW = 128
N_BRANCHES = 3

QKV_COLS = 7680
PROJ_W = 1536
GROUP_W = 4 * HEAD_DIM
A_OUT = A_HEADS_PER_GROUP * HEAD_DIM
B_OUT = B_Q_HEADS * HEAD_DIM
C_OUT = C_Q_HEADS * HEAD_DIM
B_K0, B_V0 = B_OUT, B_OUT + B_KV_HEADS * HEAD_DIM
C_K0, C_V0 = C_OUT, C_OUT + C_KV_HEADS * HEAD_DIM

V7X_VMEM_LIMIT_BYTES = 56 * 1024 * 1024

FFN_ROW_TILE, FFN_HIDDEN_TILE = 1024, 512
FFN_NORM_CHUNKS = 4
PROJ_ROW_TILE = 512
OUT_ROW_TILE, OUT_COL_TILE = 512, 512
DENSE_Q_TILE, DENSE_K_TILE = 2048, 512
DENSE_ONLINE_K_TILE = 128
BAND_Q_TILE = 2048
MERGE_ROW_TILE = 1024


def _tile(n, pref):
    t = min(pref, n)
    while n % t:
        t //= 2
    return t


def _params(*sem):
    return pltpu.CompilerParams(dimension_semantics=sem, vmem_limit_bytes=V7X_VMEM_LIMIT_BYTES)


def _rms(x, g):
    ms = jnp.mean(x * x, axis=-1, keepdims=True)
    return x * lax.rsqrt(ms + NORM_EPS) * g


def _alibi_slopes(n):
    return tuple(2.0 ** (-8.0 * i / n) for i in range(1, n + 1))


def _ffn_kernel(x_ref, g_ref, wa_ref, wb_ref, wo_ref, *rest, final_norm):
    o_ref, xn_ref = rest[-2:]
    j = pl.program_id(1)

    def half_swiglu(xn):
        a = jnp.dot(xn, wa_ref[...], preferred_element_type=F32)
        b = jnp.dot(xn, wb_ref[...], preferred_element_type=F32)
        h = (0.5 * a * jax.nn.sigmoid(a) * b).astype(BF16)
        return jnp.dot(h, wo_ref[...], preferred_element_type=F32)

    @pl.when(j == 0)
    def _():
        rows = x_ref.shape[0] // FFN_NORM_CHUNKS
        for c in range(FFN_NORM_CHUNKS):
            sl = slice(c * rows, (c + 1) * rows)
            x = x_ref[sl, :]
            xn = _rms(x, g_ref[...]).astype(BF16)
            xn_ref[sl, :] = xn
            o_ref[sl, :] = x + half_swiglu(xn)

    @pl.when(j > 0)
    def _():
        o_ref[...] += half_swiglu(xn_ref[...])

    if final_norm:
        @pl.when(j == pl.num_programs(1) - 1)
        def _():
            o_ref[...] = _rms(o_ref[...], rest[0][...])


def _ffn(x, g, w_in, w_out, layer, final_gain=None):
    n, d = x.shape
    d_ff = w_out.shape[1]
    tm = _tile(n, FFN_ROW_TILE)
    tf = _tile(d_ff, FFN_HIDDEN_TILE)
    nf = d_ff // tf
    in_specs = [
        pl.BlockSpec((tm, d), lambda i, j: (i, 0)),
        pl.BlockSpec((None, 1, d), lambda i, j: (layer, 0, 0)),
        pl.BlockSpec((None, d, tf), lambda i, j: (layer, 0, j)),
        pl.BlockSpec((None, d, tf), lambda i, j: (layer, 0, j + nf)),
        pl.BlockSpec((None, tf, d), lambda i, j: (layer, j, 0)),
    ]
    args = [x, g, w_in, w_in, w_out]
    if final_gain is not None:
        in_specs.append(pl.BlockSpec((1, d), lambda i, j: (0, 0)))
        args.append(final_gain)
    return pl.pallas_call(
        functools.partial(_ffn_kernel, final_norm=final_gain is not None),
        grid=(n // tm, nf),
        in_specs=in_specs,
        out_specs=pl.BlockSpec((tm, d), lambda i, j: (i, 0)),
        out_shape=jax.ShapeDtypeStruct((n, d), F32),
        scratch_shapes=[pltpu.VMEM((tm, d), BF16)],
        compiler_params=_params("parallel", "arbitrary"),
        name="ffn",
    )(*args)


def _norm_rope(y, gain, cos, sin_signed):
    yn = _rms(y, gain)
    lane = lax.broadcasted_iota(jnp.int32, yn.shape, 1)
    partner = jnp.where((lane & 1) == 0, pltpu.roll(yn, HEAD_DIM - 1, 1), pltpu.roll(yn, 1, 1))
    return yn * cos + partner * sin_signed


def _qkv_kernel(x_ref, g_ref, w_ref, cos_ref, sin_ref, qg_ref, kg_ref,
                a0_ref, a1_ref, a2_ref, b_ref, c_ref, u_ref, split_ref):
    j = pl.program_id(1)
    tm = x_ref.shape[0]

    @pl.when(j == 0)
    def _():
        u_ref[...] = _rms(x_ref[...], g_ref[...]).astype(BF16)

    def proj(c0, width):
        return jnp.dot(u_ref[...], w_ref[:, c0:c0 + width], preferred_element_type=F32)

    @pl.when(j < 3)
    def _():
        sc = jnp.where(j == 0, Q_SCALE, 1.0).astype(F32)
        a0_ref[...] = (proj(0, GROUP_W) * sc).astype(BF16)
        for gi, dst_ref in ((1, a1_ref), (2, a2_ref)):
            r = A_GROUPS[gi][1]
            y = proj(gi * GROUP_W, GROUP_W) * sc
            for h in range(A_HEADS_PER_GROUP):
                split_ref[(gi - 1) * A_HEADS_PER_GROUP + h] = y[:, h * HEAD_DIM:(h + 1) * HEAD_DIM]
            for p in range(r):
                for h in range(A_HEADS_PER_GROUP):
                    rows = split_ref[(gi - 1) * A_HEADS_PER_GROUP + h, pl.ds(p, tm // r, stride=r), :]
                    dst_ref[p, :, h * HEAD_DIM:(h + 1) * HEAD_DIM] = rows.astype(BF16)

    @pl.when(j == 3)
    def _():
        pair = 2 * HEAD_DIM
        for c0 in range(0, B_V0, pair):
            acc = proj(c0, pair)
            for h0 in range(0, pair, HEAD_DIM):
                is_q = c0 + h0 < B_K0
                y = _norm_rope(acc[:, h0:h0 + HEAD_DIM], (qg_ref if is_q else kg_ref)[...], cos_ref[...],
                               sin_ref[...])
                b_ref[:, c0 + h0:c0 + h0 + HEAD_DIM] = ((y * Q_SCALE) if is_q else y).astype(BF16)
        b_ref[:, B_V0:] = proj(B_V0, PROJ_W - B_V0).astype(BF16)

    @pl.when(j == 4)
    def _():
        for c0 in range(0, C_OUT, GROUP_W):
            c_ref[:, c0:c0 + GROUP_W] = (proj(c0, GROUP_W) * Q_SCALE).astype(BF16)
        c_ref[:, C_OUT:] = proj(C_OUT, PROJ_W - C_OUT).astype(BF16)


def _qkv(x, g, w_in, cos, sin_signed, qg, kg, bsz, seq, layer):
    n, d = x.shape
    tm = _tile(seq, PROJ_ROW_TILE)
    nseq = seq // tm
    r1, r2 = A_GROUPS[1][1], A_GROUPS[2][1]
    assert tm % (16 * r2) == 0
    a_part = lambda i, j: jnp.minimum(j, 2)

    def split_spec(r):
        return pl.BlockSpec((None, r, tm // r, GROUP_W), lambda i, j: (i // nseq, 0, i % nseq, a_part(i, j)))

    return pl.pallas_call(
        _qkv_kernel,
        grid=(n // tm, QKV_COLS // PROJ_W),
        in_specs=[
            pl.BlockSpec((tm, d), lambda i, j: (i, 0)),
            pl.BlockSpec((None, 1, d), lambda i, j: (layer, 0, 0)),
            pl.BlockSpec((None, d, PROJ_W), lambda i, j: (layer, 0, j)),
            pl.BlockSpec((tm, HEAD_DIM), lambda i, j: (i % nseq, 0)),
            pl.BlockSpec((tm, HEAD_DIM), lambda i, j: (i % nseq, 0)),
            pl.BlockSpec((None, 1, HEAD_DIM), lambda i, j: (layer, 0, 0)),
            pl.BlockSpec((None, 1, HEAD_DIM), lambda i, j: (layer, 0, 0)),
        ],
        out_specs=[
            pl.BlockSpec((tm, GROUP_W), lambda i, j: (i, a_part(i, j))),
            split_spec(r1),
            split_spec(r2),
            pl.BlockSpec((tm, PROJ_W), lambda i, j: (i, 0)),
            pl.BlockSpec((tm, PROJ_W), lambda i, j: (i, 0)),
            pl.BlockSpec((tm, d), lambda i, j: (i, 0)),
        ],
        out_shape=[
            jax.ShapeDtypeStruct((n, PROJ_W), BF16),
            jax.ShapeDtypeStruct((bsz, r1, seq // r1, PROJ_W), BF16),
            jax.ShapeDtypeStruct((bsz, r2, seq // r2, PROJ_W), BF16),
            jax.ShapeDtypeStruct((n, PROJ_W), BF16),
            jax.ShapeDtypeStruct((n, PROJ_W), BF16),
            jax.ShapeDtypeStruct((n, d), BF16),
        ],
        scratch_shapes=[pltpu.VMEM(((PROJ_W - GROUP_W) // HEAD_DIM, tm, HEAD_DIM), F32)],
        compiler_params=_params("parallel", "arbitrary"),
        name="qkv_proj",
    )(x, g, w_in, cos, sin_signed, qg, kg)


def _band_kernel(*refs, tq, sub, hb, half_w, n_sub, slopes, dist_scale, n_kv, has_sink, emit_lse):
    if has_sink:
        sink_ref, refs = refs[0], refs[1:]
    q_ref, kp_ref, km_ref, kn_ref, vp_ref, vm_ref, vn_ref, o_ref = refs[:8]
    n_heads = len(slopes)
    group = n_heads // n_kv
    qi = pl.program_id(1)
    span = sub + 2 * hb
    row = lax.broadcasted_iota(jnp.int32, (sub, span), 0)
    col = lax.broadcasted_iota(jnp.int32, (sub, span), 1)
    arel = jnp.abs(col - hb - row)
    in_band = arel <= half_w
    dist = (arel * dist_scale).astype(F32)
    biases = [jnp.where(in_band, (-slopes[h] * LOG2_E) * dist, NEG_INF) for h in range(n_heads)]
    kcat, vcat = [], []
    for kv in range(n_kv):
        ksl = slice(kv * HEAD_DIM, (kv + 1) * HEAD_DIM)
        kcat.append(jnp.concatenate([kp_ref[:, ksl], km_ref[:, ksl], kn_ref[:, ksl]], axis=0))
        vcat.append(jnp.concatenate([vp_ref[:, ksl], vm_ref[:, ksl], vn_ref[:, ksl]], axis=0))
    if emit_lse:
        lane = lax.broadcasted_iota(jnp.int32, (sub, HEAD_DIM), 1)
    for s in range(tq // sub):
        rows = slice(s * sub, (s + 1) * sub)
        kpos = qi * tq + (s * sub - hb) + col
        in_seq = (kpos >= 0) & (kpos < n_sub)
        lse_tile = jnp.zeros((sub, HEAD_DIM), F32) if emit_lse else None
        for h in range(n_heads):
            sl = slice(h * HEAD_DIM, (h + 1) * HEAD_DIM)
            ks = kcat[h // group][s * sub:s * sub + span]
            vs = vcat[h // group][s * sub:s * sub + span]
            sc = lax.dot_general(q_ref[rows, sl], ks, (((1,), (1,)), ((), ())), preferred_element_type=F32)
            sc = jnp.where(in_seq, sc + biases[h], NEG_INF)
            m = jnp.max(sc, axis=-1, keepdims=True)
            if has_sink:
                sk = sink_ref[0, h] * LOG2_E
                m = jnp.maximum(m, sk)
            p = jnp.exp2(sc - m)
            den = jnp.sum(p, axis=-1, keepdims=True)
            if has_sink:
                den = den + jnp.exp2(sk - m)
            o = jnp.dot(p.astype(BF16), vs, preferred_element_type=F32) * (1.0 / den)
            o_ref[rows, sl] = o.astype(o_ref.dtype)
            if emit_lse:
                lse_tile = jnp.where(lane == h, m + jnp.log2(den), lse_tile)
        if emit_lse:
            refs[8][rows, :] = lse_tile


def _band_gqa_kernel(*refs, tq, sub, hb, half_w, n_sub, slopes, dist_scale, n_kv, has_sink, emit_lse):
    assert not emit_lse
    if has_sink:
        sink_ref, refs = refs[0], refs[1:]
    q_ref, kp_ref, km_ref, kn_ref, vp_ref, vm_ref, vn_ref, o_ref = refs
    n_heads = len(slopes)
    group = n_heads // n_kv
    qi = pl.program_id(1)
    span = sub + 2 * hb
    head_sl = lambda h: slice(h * HEAD_DIM, (h + 1) * HEAD_DIM)
    key = lax.broadcasted_iota(jnp.int32, (span, sub), 0)
    qry = lax.broadcasted_iota(jnp.int32, (span, sub), 1)
    arel = jnp.abs(key - hb - qry)
    in_band = arel <= half_w
    dist = (arel * dist_scale).astype(F32)
    key_wide = lax.broadcasted_iota(jnp.int32, (span, group * sub), 0)
    biases, sinks, kcat, vcat = [], [], [], []
    for kv in range(n_kv):
        heads = range(kv * group, (kv + 1) * group)
        biases.append(jnp.concatenate(
            [jnp.where(in_band, (-slopes[h] * LOG2_E) * dist, NEG_INF) for h in heads], axis=1))
        if has_sink:
            sinks.append(jnp.concatenate(
                [jnp.full((1, sub), sink_ref[0, h] * LOG2_E, F32) for h in heads], axis=1))
        ksl = head_sl(kv)
        kcat.append(jnp.concatenate([kp_ref[:, ksl], km_ref[:, ksl], kn_ref[:, ksl]], axis=0))
        vcat.append(jnp.concatenate([vp_ref[:, ksl], vm_ref[:, ksl], vn_ref[:, ksl]], axis=0))
    for s in range(tq // sub):
        rows = slice(s * sub, (s + 1) * sub)
        kpos = qi * tq + (s * sub - hb) + key_wide
        in_seq = (kpos >= 0) & (kpos < n_sub)
        for kv in range(n_kv):
            heads = range(kv * group, (kv + 1) * group)
            ks = kcat[kv][s * sub:s * sub + span]
            vs = vcat[kv][s * sub:s * sub + span]
            q = jnp.concatenate([q_ref[rows, head_sl(h)] for h in heads], axis=0)
            st = lax.dot_general(ks, q, (((1,), (1,)), ((), ())), preferred_element_type=F32)
            st = jnp.where(in_seq, st + biases[kv], NEG_INF)
            m = jnp.max(st, axis=0, keepdims=True)
            if has_sink:
                m = jnp.maximum(m, sinks[kv])
            p = jnp.exp2(st - m)
            den = jnp.sum(p, axis=0, keepdims=True)
            if has_sink:
                den = den + jnp.exp2(sinks[kv] - m)
            ot = lax.dot_general(vs, p.astype(BF16), (((0,), (0,)), ((), ())), preferred_element_type=F32)
            ot = ot * (1.0 / den)
            for gi, h in enumerate(heads):
                o_ref[rows, head_sl(h)] = ot[:, gi * sub:(gi + 1) * sub].T.astype(o_ref.dtype)


def _band_attention(arr, *, q_col, k_col, v_col, n_heads, n_kv, half_w, slopes, dist_scale, sink, emit_lse,
                    out_dtype):
    nb, n_sub, _ = arr.shape
    hb = half_w
    sub = min(HEAD_DIM, n_sub)
    tq = _tile(n_sub, BAND_Q_TILE)
    assert sub % hb == 0 and tq % sub == 0 and n_sub % tq == 0
    per = tq // hb
    n_hb = n_sub // hb
    qw, kw = n_heads * HEAD_DIM, n_kv * HEAD_DIM
    assert q_col % qw == 0 and k_col % kw == 0 and v_col % kw == 0

    def halo_specs(c):
        return [
            pl.BlockSpec((None, hb, kw), lambda b, i: (b, jnp.maximum(i * per - 1, 0), c // kw)),
            pl.BlockSpec((None, tq, kw), lambda b, i: (b, i, c // kw)),
            pl.BlockSpec((None, hb, kw), lambda b, i: (b, jnp.minimum((i + 1) * per, n_hb - 1), c // kw)),
        ]

    in_specs = [pl.BlockSpec((None, tq, qw), lambda b, i: (b, i, q_col // qw))] + halo_specs(k_col) + halo_specs(v_col)
    args = [arr] * 7
    if sink is not None:
        in_specs = [pl.BlockSpec(memory_space=pltpu.SMEM)] + in_specs
        args = [sink] + args
    out_specs = [pl.BlockSpec((None, tq, qw), lambda b, i: (b, i, 0))]
    out_shape = [jax.ShapeDtypeStruct((nb, n_sub, qw), out_dtype)]
    if emit_lse:
        out_specs.append(pl.BlockSpec((None, tq, HEAD_DIM), lambda b, i: (b, i, 0)))
        out_shape.append(jax.ShapeDtypeStruct((nb, n_sub, HEAD_DIM), F32))
    body = _band_gqa_kernel if n_kv < n_heads else _band_kernel
    return pl.pallas_call(
        functools.partial(body, tq=tq, sub=sub, hb=hb, half_w=half_w, n_sub=n_sub, slopes=slopes,
                          dist_scale=dist_scale, n_kv=n_kv, has_sink=sink is not None, emit_lse=emit_lse),
        grid=(nb, n_sub // tq),
        in_specs=in_specs,
        out_specs=out_specs,
        out_shape=out_shape,
        compiler_params=_params("parallel", "arbitrary"),
        name=f"band_attn_n{n_sub}_w{half_w}",
    )(*args)


def _merge_kernel(o0_ref, o1_ref, o2_ref, l0_ref, l1_ref, l2_ref, out_ref, ob1_ref, ob2_ref, lb1_ref, lb2_ref):
    tm = out_ref.shape[0]
    for src_ref, dst_ref in ((o1_ref, ob1_ref), (o2_ref, ob2_ref), (l1_ref, lb1_ref), (l2_ref, lb2_ref)):
        r = src_ref.shape[0]
        for p in range(r):
            for h in range(dst_ref.shape[0]):
                dst_ref[h, pl.ds(p, tm // r, stride=r), :] = src_ref[p, :, h * HEAD_DIM:(h + 1) * HEAD_DIM]
    l0, l1, l2 = l0_ref[...], lb1_ref[0], lb2_ref[0]
    mx = jnp.maximum(jnp.maximum(l0, l1), l2)
    w0, w1, w2 = jnp.exp2(l0 - mx), jnp.exp2(l1 - mx), jnp.exp2(l2 - mx)
    den = w0 + w1 + w2
    a0, a1, a2 = w0 / den, w1 / den, w2 / den
    for h in range(A_HEADS_PER_GROUP):
        sl = slice(h * HEAD_DIM, (h + 1) * HEAD_DIM)
        out = a0[:, h:h + 1] * o0_ref[:, sl] + a1[:, h:h + 1] * ob1_ref[h] + a2[:, h:h + 1] * ob2_ref[h]
        out_ref[:, sl] = out.astype(BF16)


def _merge_groups(outs, lses, bsz, seq):
    n = bsz * seq
    tm = _tile(seq, MERGE_ROW_TILE)
    nseq = seq // tm
    r1, r2 = A_GROUPS[1][1], A_GROUPS[2][1]

    def specs(w):
        return [
            pl.BlockSpec((tm, w), lambda i: (i, 0)),
            pl.BlockSpec((None, r1, tm // r1, w), lambda i: (i // nseq, 0, i % nseq, 0)),
            pl.BlockSpec((None, r2, tm // r2, w), lambda i: (i // nseq, 0, i % nseq, 0)),
        ]

    def views(xs, w):
        return [xs[0].reshape(n, w), xs[1].reshape(bsz, r1, seq // r1, w), xs[2].reshape(bsz, r2, seq // r2, w)]

    return pl.pallas_call(
        _merge_kernel,
        grid=(n // tm,),
        in_specs=specs(A_OUT) + specs(HEAD_DIM),
        out_specs=pl.BlockSpec((tm, A_OUT), lambda i: (i, 0)),
        out_shape=jax.ShapeDtypeStruct((n, A_OUT), BF16),
        scratch_shapes=[pltpu.VMEM((A_HEADS_PER_GROUP, tm, HEAD_DIM), F32),
                        pltpu.VMEM((A_HEADS_PER_GROUP, tm, HEAD_DIM), F32),
                        pltpu.VMEM((1, tm, HEAD_DIM), F32), pltpu.VMEM((1, tm, HEAD_DIM), F32)],
        compiler_params=_params("parallel"),
        name="merge_dilated",
    )(*views(outs, A_OUT), *views(lses, HEAD_DIM))


def _dense_kernel(q_ref, k_ref, v_ref, o_ref, qt_ref, st_ref, m_ref, l_ref, acc_ref, *, tq, tk, seq):
    group = B_Q_HEADS // B_KV_HEADS
    n_blk = seq // tk
    for g in range(group):
        qg = q_ref[:, g * HEAD_DIM:(g + 1) * HEAD_DIM].astype(F32)
        qt_ref[:, g * tq:(g + 1) * tq] = qg.T.astype(BF16)

    def scores(blk, size):
        k0 = pl.multiple_of(blk * size, size)
        return jnp.dot(k_ref[pl.ds(k0, size), :], qt_ref[...], preferred_element_type=F32)

    def values(blk, size):
        v0 = pl.multiple_of(blk * size, size)
        return v_ref[pl.ds(v0, size), :]

    def pv(v, p):
        return lax.dot_general(v, p.astype(BF16), (((0,), (0,)), ((), ())), preferred_element_type=F32)

    def write_out():
        out = acc_ref[...] / l_ref[...]
        for g in range(group):
            o_ref[:, g * HEAD_DIM:(g + 1) * HEAD_DIM] = out[:, g * tq:(g + 1) * tq].T.astype(BF16)

    first = jnp.dot(k_ref[pl.ds(0, min(tk, HEAD_DIM)), :], qt_ref[...], preferred_element_type=F32)
    m_ref[...] = jnp.max(first, axis=0, keepdims=True)
    l_ref[...] = jnp.zeros(l_ref.shape, F32)
    acc_ref[...] = jnp.zeros(acc_ref.shape, F32)

    def lazy_step(blk):
        k0 = pl.multiple_of(blk * tk, tk)
        k = k_ref[pl.ds(k0, tk), :]
        v = values(blk, tk)
        width = LANE_CHUNK_W
        chunks = [slice(c * width, (c + 1) * width) for c in range(group * tq // width)]
        sts = [jnp.dot(k, qt_ref[:, sl], preferred_element_type=F32) for sl in chunks]
        for sl, st in zip(chunks, sts):
            p = jnp.exp2(st - m_ref[:, sl])
            l_ref[:, sl] += jnp.sum(p, axis=0, keepdims=True)
            acc_ref[:, sl] += pv(v, p)

    def lazy_body(c, carry):
        lazy_step(2 * c)
        lazy_step(2 * c + 1)
        return carry

    lax.fori_loop(0, n_blk // 2, lazy_body, 0)
    l = l_ref[...]
    sums_ok = jnp.min(jnp.where(jnp.abs(acc_ref[...]) < F32_HUGE, 1.0, 0.0))
    sums_ok = sums_ok * jnp.min(jnp.where((l > 0.0) & (l < F32_HUGE), 1.0, 0.0))
    exceeded = sums_ok < 0.5

    @pl.when(jnp.logical_not(exceeded))
    def _():
        write_out()

    @pl.when(exceeded)
    def _():
        _dense_online(scores, values, pv, write_out, st_ref, m_ref, l_ref, acc_ref, seq)


def _dense_online(scores, values, pv, write_out, st_ref, m_ref, l_ref, acc_ref, seq):
    size = st_ref.shape[1]
    n_blk = seq // size
    m_ref[...] = jnp.full(m_ref.shape, -jnp.inf, F32)
    l_ref[...] = jnp.zeros(l_ref.shape, F32)
    acc_ref[...] = jnp.zeros(acc_ref.shape, F32)
    st_ref[0] = scores(0, size)

    def step(blk, cur, nxt):
        st_ref[nxt] = scores(jnp.minimum(blk + 1, n_blk - 1), size)
        st = st_ref[cur]
        m_prev = m_ref[...]
        m_new = jnp.maximum(m_prev, jnp.max(st, axis=0, keepdims=True))
        alpha = jnp.exp2(m_prev - m_new)
        p = jnp.exp2(st - m_new)
        l_ref[...] = alpha * l_ref[...] + jnp.sum(p, axis=0, keepdims=True)
        acc_ref[...] = alpha * acc_ref[...] + pv(values(blk, size), p)
        m_ref[...] = m_new

    def body(c, carry):
        step(2 * c, 0, 1)
        step(2 * c + 1, 1, 0)
        return carry

    lax.fori_loop(0, n_blk // 2, body, 0)
    write_out()


def _dense_attention(arr):
    bsz, seq, _ = arr.shape
    tq = _tile(seq, DENSE_Q_TILE)
    tk = _tile(seq // 2, DENSE_K_TILE)
    assert (seq // tk) % 2 == 0
    group = B_Q_HEADS // B_KV_HEADS
    qw = group * HEAD_DIM
    out = pl.pallas_call(
        functools.partial(_dense_kernel, tq=tq, tk=tk, seq=seq),
        grid=(bsz, B_KV_HEADS, seq // tq),
        in_specs=[
            pl.BlockSpec((None, tq, qw), lambda b, h, i: (b, i, h)),
            pl.BlockSpec((None, seq, HEAD_DIM), lambda b, h, i: (b, 0, B_K0 // HEAD_DIM + h),
                         pipeline_mode=pl.Buffered(1)),
            pl.BlockSpec((None, seq, HEAD_DIM), lambda b, h, i: (b, 0, B_V0 // HEAD_DIM + h),
                         pipeline_mode=pl.Buffered(1)),
        ],
        out_specs=pl.BlockSpec((None, tq, qw), lambda b, h, i: (b, i, h)),
        out_shape=jax.ShapeDtypeStruct((bsz, seq, B_OUT), BF16),
        scratch_shapes=[
            pltpu.VMEM((HEAD_DIM, group * tq), BF16),
            pltpu.VMEM((2, _tile(seq // 2, DENSE_ONLINE_K_TILE), group * tq), F32),
            pltpu.VMEM((1, group * tq), F32),
            pltpu.VMEM((1, group * tq), F32),
            pltpu.VMEM((HEAD_DIM, group * tq), F32),
        ],
        compiler_params=_params("parallel", "parallel", "arbitrary"),
        name="dense_attn",
    )(arr, arr, arr)
    return out.reshape(bsz * seq, B_OUT)


def _out_kernel(x_ref, u_ref, wg0_ref, wg1_ref, wg2_ref, oa_ref, ob_ref, oc_ref,
                wa_ref, wb_ref, wc_ref, wo_ref, o_ref):
    j = pl.program_id(1)

    @pl.when(j == 0)
    def _():
        o_ref[...] = x_ref[...]

    u = u_ref[...]

    def branch(wg_ref, br_ref, w_ref):
        gate = jax.nn.sigmoid(jnp.dot(u, wg_ref[...], preferred_element_type=F32))
        return gate * jnp.dot(br_ref[...], w_ref[...], preferred_element_type=F32)

    merged = branch(wg0_ref, oa_ref, wa_ref) + branch(wg1_ref, ob_ref, wb_ref) + branch(wg2_ref, oc_ref, wc_ref)
    o_ref[...] += jnp.dot(merged.astype(BF16), wo_ref[...], preferred_element_type=F32)


def _mixer_out(x, u, w_in, o_a, o_b, o_c, w_a, w_b, w_c, w_o, layer):
    n, d = x.shape
    tm = _tile(n, OUT_ROW_TILE)
    tc = _tile(d, OUT_COL_TILE)
    assert QKV_COLS % tc == 0
    g0 = QKV_COLS // tc
    gstep = d // tc

    def gate_spec(br):
        return pl.BlockSpec((None, d, tc), lambda i, j: (layer, 0, g0 + br * gstep + j))

    return pl.pallas_call(
        _out_kernel,
        grid=(n // tm, d // tc),
        in_specs=[
            pl.BlockSpec((tm, d), lambda i, j: (i, 0)),
            pl.BlockSpec((tm, d), lambda i, j: (i, 0)),
            gate_spec(0), gate_spec(1), gate_spec(2),
            pl.BlockSpec((tm, A_OUT), lambda i, j: (i, 0)),
            pl.BlockSpec((tm, B_OUT), lambda i, j: (i, 0)),
            pl.BlockSpec((tm, C_OUT), lambda i, j: (i, 0)),
            pl.BlockSpec((None, A_OUT, tc), lambda i, j: (layer, 0, j)),
            pl.BlockSpec((None, B_OUT, tc), lambda i, j: (layer, 0, j)),
            pl.BlockSpec((None, C_OUT, tc), lambda i, j: (layer, 0, j)),
            pl.BlockSpec((None, tc, d), lambda i, j: (layer, j, 0)),
        ],
        out_specs=pl.BlockSpec((tm, d), lambda i, j: (i, 0)),
        out_shape=jax.ShapeDtypeStruct((n, d), F32),
        compiler_params=_params("parallel", "arbitrary"),
        name="mixer_out",
    )(x, u, w_in, w_in, w_in, o_a, o_b, o_c, w_a, w_b, w_c, w_o)


def _rope_tables(seq):
    n_freq = HEAD_DIM // 4
    t = lax.broadcasted_iota(jnp.int32, (seq, HEAD_DIM), 0)
    lane = lax.broadcasted_iota(jnp.int32, (seq, HEAD_DIM), 1)
    pair = lane // 2
    pos = jnp.where(pair < n_freq, t // GRID_W, t % GRID_W).astype(F32)
    inv = ROPE_THETA ** (-(pair % n_freq).astype(F32) / n_freq)
    ang = pos * inv
    sign = jnp.where(lane % 2 == 0, -1.0, 1.0).astype(F32)
    return jnp.cos(ang), jnp.sin(ang) * sign


def _mixer(x, lw, cos, sin_signed, bsz, seq, layer):
    a0, a1, a2, bqkv, cqkv, u = _qkv(x, lw["norm_mix"], lw["w_in"], cos, sin_signed, lw["qk_norm_q"],
                                     lw["qk_norm_k"], bsz, seq, layer)
    a_slopes = _alibi_slopes(A_HEADS)
    outs, lses = [], []
    for gi, (arr, (window, r)) in enumerate(zip((a0, a1, a2), A_GROUPS)):
        hs = slice(gi * A_HEADS_PER_GROUP, (gi + 1) * A_HEADS_PER_GROUP)
        o, l = _band_attention(
            arr.reshape(bsz * r, seq // r, PROJ_W), q_col=0, k_col=GROUP_W, v_col=2 * GROUP_W,
            n_heads=A_HEADS_PER_GROUP, n_kv=A_HEADS_PER_GROUP, half_w=(window // 2) // r, slopes=a_slopes[hs],
            dist_scale=r, sink=None, emit_lse=True, out_dtype=F32)
        outs.append(o)
        lses.append(l)
    o_a = _merge_groups(outs, lses, bsz, seq)
    o_b = _dense_attention(bqkv.reshape(bsz, seq, PROJ_W))
    o_c, = _band_attention(
        cqkv.reshape(bsz, seq, PROJ_W), q_col=0, k_col=C_K0, v_col=C_V0, n_heads=C_Q_HEADS, n_kv=C_KV_HEADS,
        half_w=C_HALF_WINDOW, slopes=_alibi_slopes(C_Q_HEADS), dist_scale=1, sink=lw["sink_c"][layer],
        emit_lse=False, out_dtype=BF16)
    return _mixer_out(x, u, lw["w_in"], o_a, o_b, o_c.reshape(bsz * seq, C_OUT), lw["w_br_a"],
                      lw["w_br_b"], lw["w_br_c"], lw["w_out"], layer)


def _trunk(x, lw, norm_final, depth):
    bsz, seq, d = x.shape
    cos, sin_signed = _rope_tables(seq)
    xf = x.reshape(bsz * seq, d)
    for layer in range(depth):
        xf = _ffn(xf, lw["norm_ffn1"], lw["ffn1_w_in"], lw["ffn1_w_out"], layer)
        xf = _mixer(xf, lw, cos, sin_signed, bsz, seq, layer)
        last = layer == depth - 1
        xf = _ffn(xf, lw["norm_ffn2"], lw["ffn2_w_in"], lw["ffn2_w_out"], layer,
                  final_gain=norm_final.reshape(1, d) if last else None)
    return xf.reshape(bsz, seq, d)


def kernel(x_prompt, x_sample, norm_ffn1, ffn1_w_in, ffn1_w_out, norm_mix, w_in, qk_norm_q, qk_norm_k, sink_c,
           w_br_a, w_br_b, w_br_c, w_out, norm_ffn2, ffn2_w_in, ffn2_w_out, norm_final):
    depth = norm_ffn1.shape[0]
    row = lambda v: v.reshape(depth, 1, v.shape[-1])
    layers = {
        "norm_ffn1": row(norm_ffn1), "ffn1_w_in": ffn1_w_in.astype(BF16), "ffn1_w_out": ffn1_w_out.astype(BF16),
        "norm_mix": row(norm_mix), "w_in": w_in.astype(BF16),
        "qk_norm_q": row(qk_norm_q), "qk_norm_k": row(qk_norm_k), "sink_c": row(sink_c),
        "w_br_a": w_br_a.astype(BF16), "w_br_b": w_br_b.astype(BF16), "w_br_c": w_br_c.astype(BF16),
        "w_out": w_out.astype(BF16),
        "norm_ffn2": row(norm_ffn2), "ffn2_w_in": ffn2_w_in.astype(BF16), "ffn2_w_out": ffn2_w_out.astype(BF16),
    }
    assert x_prompt.shape[1:] == x_sample.shape[1:]
    n_prompt = x_prompt.shape[0]
    y = _trunk(jnp.concatenate([x_prompt, x_sample], axis=0), layers, norm_final, depth)
    return (y[:n_prompt], y[n_prompt:])
```
